```python
import jax, jax.numpy as jnp
from jax import lax
import numpy as np

D_MODEL = 1024
BATCH = 16
SEQ = 256
DEPTH = 4
DEC_BATCH = 8
DEC_SEQ = 4096
PAST_LEN = 512

GRID_W = 64
EPS = 1e-6
ROPE_THETA = 10000.0
D_FF = 4 * D_MODEL
CONV_W = 3
Q_BLOCK = 128
N_EVEN = (DEPTH + 1) // 2
N_ODD = DEPTH // 2
GDN_HEADS = D_MODEL // 256
GDN_DK = 128
GDN_DV = 128
GDN_CHUNK = 64
GQA_HEADS = D_MODEL // 256
GQA_KV_HEADS = GQA_HEADS // 2
GQA_DH = 128
SSD_HEADS = D_MODEL // 128
SSD_P = 64
SSD_N = 128
SSD_GROUPS = 2
SSD_CHUNK = 64
SSD_D_INNER = SSD_HEADS * SSD_P
NA_HEADS = D_MODEL // 256
NA_DH = 128
NA_WIN_R = 8
NA_WIN_C = 16

EV_WIDTHS = (GDN_HEADS * GDN_DK, GDN_HEADS * GDN_DK, GDN_HEADS * GDN_DV, GDN_HEADS * GDN_DV,
             2 * GDN_HEADS, 2 * GDN_HEADS,
             GQA_HEADS * GQA_DH, GQA_KV_HEADS * GQA_DH, GQA_KV_HEADS * GQA_DH)
EV_IN = sum(EV_WIDTHS)
EV_MIX = GDN_HEADS * GDN_DV + GQA_HEADS * GQA_DH
OD_WIDTHS = (SSD_D_INNER, SSD_D_INNER, SSD_GROUPS * SSD_N, SSD_GROUPS * SSD_N, 2 * SSD_HEADS,
             NA_HEADS * NA_DH, NA_HEADS * NA_DH, NA_HEADS * NA_DH)
OD_IN = sum(OD_WIDTHS)
OD_MIX = SSD_D_INNER + NA_HEADS * NA_DH

kernel_name = 'hybrid_dit_gdn_gqa_ssd_natten_step'


def rmsnorm(x, w):
    xf = x.astype(jnp.float32)
    y = xf * lax.rsqrt(jnp.mean(xf * xf, axis=-1, keepdims=True) + EPS)
    return (y * w.astype(jnp.float32)).astype(x.dtype)


def l2norm(x):
    xf = x.astype(jnp.float32)
    return (xf * lax.rsqrt(jnp.sum(xf * xf, axis=-1, keepdims=True) + EPS)).astype(x.dtype)


def split_cols(t, widths):
    return jnp.split(t, [int(i) for i in np.cumsum(widths)[:-1]], axis=-1)


def depthwise_conv(x, w):
    pad = CONV_W // 2
    return lax.conv_general_dilated(x, w[:, None, :].astype(x.dtype), (1,), [(pad, pad)],
                                    dimension_numbers=('NWC', 'WIO', 'NWC'),
                                    feature_group_count=x.shape[-1])


def modulate(x, w_pre, shift, scale):
    return rmsnorm(x, w_pre) * (1 + scale) + shift


def rope_angles(n_tokens):
    half = GQA_DH // 2
    inv_freq = ROPE_THETA ** (-jnp.arange(0, half, 2, dtype=jnp.float32) / half)
    t = jnp.arange(n_tokens)
    row = (t // GRID_W).astype(jnp.float32)
    col = (t % GRID_W).astype(jnp.float32)
    return row[:, None] * inv_freq, col[:, None] * inv_freq


def rotate(x, ang):
    m = ang.shape[-1]
    cos = jnp.cos(ang)[None, :, None, :].astype(x.dtype)
    sin = jnp.sin(ang)[None, :, None, :].astype(x.dtype)
    x1, x2 = x[..., :m], x[..., m:]
    return jnp.concatenate([x1 * cos - x2 * sin, x2 * cos + x1 * sin], axis=-1)


def axial_rope(x, ang_r, ang_c):
    half = x.shape[-1] // 2
    return jnp.concatenate([rotate(x[..., :half], ang_r), rotate(x[..., half:], ang_c)], axis=-1)


def block_attention(q, k, v):
    bsz, seq_len, heads, dh = q.shape
    kv_heads = k.shape[2]
    rep = heads // kv_heads
    nb = seq_len // Q_BLOCK
    qb = q.reshape(bsz, nb, Q_BLOCK, kv_heads, rep, dh).transpose(1, 0, 2, 3, 4, 5)

    def one_block(q_blk):
        s = jnp.einsum('bqgrd,bkgd->bgrqk', q_blk, k, preferred_element_type=jnp.float32) * dh ** -0.5
        p = jax.nn.softmax(s, axis=-1).astype(v.dtype)
        return jnp.einsum('bgrqk,bkgd->bqgrd', p, v)

    o = lax.map(one_block, qb)
    return o.transpose(1, 0, 2, 3, 4, 5).reshape(bsz, seq_len, heads, dh)


def neighbourhood_attention(q, k, v, k_ctx, v_ctx, rpb):
    bsz, seq_len, heads, dh = q.shape
    rows = seq_len // GRID_W
    win_r = min(NA_WIN_R, rows)
    qg = q.reshape(bsz, rows, GRID_W, heads, dh)
    kg = k.reshape(bsz, rows, GRID_W, heads, dh)
    vg = v.reshape(bsz, rows, GRID_W, heads, dh)
    col = jnp.arange(GRID_W)
    col_start = jnp.clip(col - NA_WIN_C // 2, 0, GRID_W - NA_WIN_C)
    col_idx = col_start[:, None] + jnp.arange(NA_WIN_C)[None, :]
    col_off = col_idx - col[:, None] + (NA_WIN_C - 1)
    bias_col = rpb.astype(jnp.float32)[:, :, col_off]
    scale = dh ** -0.5

    def one_row(r):
        r0 = jnp.clip(r - win_r // 2, 0, rows - win_r)
        q_r = lax.dynamic_index_in_dim(qg, r, axis=1, keepdims=False)
        k_rows = lax.dynamic_slice_in_dim(kg, r0, win_r, axis=1)
        v_rows = lax.dynamic_slice_in_dim(vg, r0, win_r, axis=1)
        k_nb = k_rows[:, :, col_idx]
        v_nb = v_rows[:, :, col_idx]
        row_off = r0 + jnp.arange(win_r) - r + (NA_WIN_R - 1)
        bias = bias_col[:, row_off].transpose(0, 2, 1, 3)
        s_loc = jnp.einsum('bqhd,brqchd->bhqrc', q_r, k_nb, preferred_element_type=jnp.float32) * scale + bias[None]
        s_ctx = jnp.einsum('bqhd,bkhd->bhqk', q_r, k_ctx, preferred_element_type=jnp.float32) * scale
        n_loc = win_r * NA_WIN_C
        s = jnp.concatenate([s_loc.reshape(bsz, heads, GRID_W, n_loc), s_ctx], axis=-1)
        p = jax.nn.softmax(s, axis=-1).astype(v.dtype)
        p_loc = p[..., :n_loc].reshape(bsz, heads, GRID_W, win_r, NA_WIN_C)
        p_ctx = p[..., n_loc:]
        return jnp.einsum('bhqrc,brqchd->bqhd', p_loc, v_nb) + jnp.einsum('bhqk,bkhd->bqhd', p_ctx, v_ctx)

    o = lax.map(one_row, jnp.arange(rows))
    return o.transpose(1, 0, 2, 3, 4).reshape(bsz, seq_len, heads, dh)


def gated_delta_chunked(q, k, v, g, beta, s0):
    bsz, seq_len, heads, _ = q.shape
    dv = v.shape[-1]
    nc = seq_len // GDN_CHUNK
    f32 = jnp.float32

    def chunks(t):
        return t.astype(f32).reshape(bsz, nc, GDN_CHUNK, heads, -1).transpose(0, 1, 3, 2, 4)

    q, k, v = chunks(q), chunks(k), chunks(v)
    g = g.astype(f32).reshape(bsz, nc, GDN_CHUNK, heads).transpose(0, 1, 3, 2)
    beta = beta.astype(f32).reshape(bsz, nc, GDN_CHUNK, heads).transpose(0, 1, 3, 2)
    gc = jnp.cumsum(g, axis=-1)
    incl = jnp.tril(jnp.ones((GDN_CHUNK, GDN_CHUNK), dtype=bool))
    strict = jnp.tril(jnp.ones((GDN_CHUNK, GDN_CHUNK), dtype=bool), -1)
    decay = jnp.exp(jnp.where(incl, gc[..., :, None] - gc[..., None, :], -jnp.inf))
    kb = k * beta[..., None]
    a_mat = jnp.where(strict, jnp.einsum('bnhik,bnhjk->bnhij', kb, k) * decay, 0.0)
    t_mat = a_mat + jnp.eye(GDN_CHUNK, dtype=f32)
    rhs = jnp.concatenate([v * beta[..., None], kb * jnp.exp(gc)[..., None]], axis=-1)
    sol = lax.linalg.triangular_solve(t_mat, rhs, left_side=True, lower=True, unit_diagonal=True)
    u, w = sol[..., :dv], sol[..., dv:]
    qk = jnp.where(incl, jnp.einsum('bnhik,bnhjk->bnhij', q, k) * decay, 0.0)
    q_dec = q * jnp.exp(gc)[..., None]
    k_dec = k * jnp.exp(gc[..., -1:] - gc)[..., None]
    g_tot = jnp.exp(gc[..., -1])

    def step(s, xs):
        u_c, w_c, qk_c, qd_c, kd_c, gt_c = xs
        v_new = u_c - jnp.einsum('bhck,bhkv->bhcv', w_c, s)
        o = jnp.einsum('bhck,bhkv->bhcv', qd_c, s) + jnp.einsum('bhij,bhjv->bhiv', qk_c, v_new)
        s = s * gt_c[..., None, None] + jnp.einsum('bhck,bhcv->bhkv', kd_c, v_new)
        return s, o

    xs = tuple(jnp.moveaxis(t, 1, 0) for t in (u, w, qk, q_dec, k_dec, g_tot))
    s_fin, o = lax.scan(step, s0.astype(f32), xs)
    o = o.transpose(1, 0, 3, 2, 4).reshape(bsz, seq_len, heads, dv)
    return o, s_fin


def bidir_gated_delta(q, k, v, g, beta, s0):
    flip = lambda t: jnp.flip(t, axis=1)
    o_f, s_f = gated_delta_chunked(q, k, v, g[:, :, 0], beta[:, :, 0], s0[:, 0])
    o_b, s_b = gated_delta_chunked(flip(q), flip(k), flip(v), flip(g[:, :, 1]), flip(beta[:, :, 1]), s0[:, 1])
    return o_f + flip(o_b), jnp.stack([s_f, s_b], axis=1)


def ssd_chunked(x, dt, a, bm, cm, h0):
    bsz, seq_len, heads, p_dim = x.shape
    groups, n_dim = bm.shape[2], bm.shape[3]
    e = heads // groups
    nc = seq_len // SSD_CHUNK
    f32 = jnp.float32
    dtf = dt.astype(f32)
    xf = (x.astype(f32) * dtf[..., None]).reshape(bsz, nc, SSD_CHUNK, groups, e, p_dim)
    la = (dtf * a.astype(f32)).reshape(bsz, nc, SSD_CHUNK, groups, e)
    bc = bm.astype(f32).reshape(bsz, nc, SSD_CHUNK, groups, n_dim)
    cc = cm.astype(f32).reshape(bsz, nc, SSD_CHUNK, groups, n_dim)
    cs = jnp.cumsum(la, axis=2)
    incl = jnp.tril(jnp.ones((SSD_CHUNK, SSD_CHUNK), dtype=bool))[:, :, None, None]
    l_mat = jnp.exp(jnp.where(incl, cs[:, :, :, None] - cs[:, :, None, :], -jnp.inf))
    y_diag = jnp.einsum('bclgn,bcsgn,bclsge,bcsgep->bclgep', cc, bc, l_mat, xf)
    decay_states = jnp.exp(cs[:, :, -1:] - cs)
    states = jnp.einsum('bclgn,bclge,bclgep->bcgepn', bc, decay_states, xf)
    chunk_decay = jnp.exp(cs[:, :, -1])

    def step(h, xs):
        st, cd = xs
        return h * cd[..., None, None] + st, h

    h_fin, h_start = lax.scan(step, h0.astype(f32).reshape(bsz, groups, e, p_dim, n_dim),
                              (jnp.moveaxis(states, 1, 0), jnp.moveaxis(chunk_decay, 1, 0)))
    h_start = jnp.moveaxis(h_start, 0, 1)
    y_off = jnp.einsum('bclgn,bcgepn,bclge->bclgep', cc, h_start, jnp.exp(cs))
    y = (y_diag + y_off).reshape(bsz, seq_len, heads, p_dim)
    return y, h_fin.reshape(bsz, heads, p_dim, n_dim)


def bidir_ssd(x, dt, a, bm, cm, h0):
    flip = lambda t: jnp.flip(t, axis=1)
    y_f, h_f = ssd_chunked(x, dt[:, :, 0], a[0], bm, cm, h0[:, 0])
    y_b, h_b = ssd_chunked(flip(x), flip(dt[:, :, 1]), a[1], flip(bm), flip(cm), h0[:, 1])
    return y_f + flip(y_b), jnp.stack([h_f, h_b], axis=1)


def even_mixer(h, w_in, w_out, conv_w, a_log, dt_bias, gdn_norm, q_norm, k_norm, ang=None, ctx=None):
    bsz, seq_len, _ = h.shape
    qa, ka, va, za, ba, aa, qb, kb, vb = split_cols(h @ w_in, EV_WIDTHS)
    qkv = jax.nn.silu(depthwise_conv(jnp.concatenate([qa, ka, va], axis=-1), conv_w))
    qa, ka, va = split_cols(qkv, EV_WIDTHS[:3])
    qa = l2norm(qa.reshape(bsz, seq_len, GDN_HEADS, GDN_DK)) * GDN_DK ** -0.5
    ka = l2norm(ka.reshape(bsz, seq_len, GDN_HEADS, GDN_DK))
    va = va.reshape(bsz, seq_len, GDN_HEADS, GDN_DV)
    beta = jax.nn.sigmoid(ba.astype(jnp.float32)).reshape(bsz, seq_len, 2, GDN_HEADS)
    g = -jnp.exp(a_log.astype(jnp.float32)) * jax.nn.softplus(
        aa.astype(jnp.float32).reshape(bsz, seq_len, 2, GDN_HEADS) + dt_bias.astype(jnp.float32))
    s0 = jnp.zeros((bsz, 2, GDN_HEADS, GDN_DK, GDN_DV), jnp.float32) if ctx is None else ctx[0]
    o_a, s_a = bidir_gated_delta(qa, ka, va, g, beta, s0)
    o_a = rmsnorm(o_a.astype(h.dtype), gdn_norm) * jax.nn.silu(za.reshape(bsz, seq_len, GDN_HEADS, GDN_DV))
    qb = rmsnorm(qb.reshape(bsz, seq_len, GQA_HEADS, GQA_DH), q_norm)
    kb = rmsnorm(kb.reshape(bsz, seq_len, GQA_KV_HEADS, GQA_DH), k_norm)
    vb = vb.reshape(bsz, seq_len, GQA_KV_HEADS, GQA_DH)
    if ctx is None:
        o_b = block_attention(qb, kb, vb)
    else:
        o_b = block_attention(axial_rope(qb, *ang),
                              jnp.concatenate([axial_rope(kb, *ang), ctx[1].astype(kb.dtype)], axis=1),
                              jnp.concatenate([vb, ctx[2].astype(vb.dtype)], axis=1))
    y = jnp.concatenate([o_a.reshape(bsz, seq_len, -1), o_b.reshape(bsz, seq_len, -1)], axis=-1) @ w_out
    if ctx is None:
        return y, s_a, kb, vb
    return y


def odd_mixer(h, w_in, w_out, conv_w, conv_b, a_log, dt_bias, d_skip, ssd_norm, rpb, ctx=None):
    bsz, seq_len, _ = h.shape
    z, xc, bc, cc, dt_raw, qd, kd, vd = split_cols(h @ w_in, OD_WIDTHS)
    xbc = jax.nn.silu(depthwise_conv(jnp.concatenate([xc, bc, cc], axis=-1), conv_w) + conv_b)
    xc, bc, cc = split_cols(xbc, OD_WIDTHS[1:4])
    xh = xc.reshape(bsz, seq_len, SSD_HEADS, SSD_P)
    bm = bc.reshape(bsz, seq_len, SSD_GROUPS, SSD_N)
    cm = cc.reshape(bsz, seq_len, SSD_GROUPS, SSD_N)
    dt = jax.nn.softplus(dt_raw.astype(jnp.float32).reshape(bsz, seq_len, 2, SSD_HEADS) + dt_bias.astype(jnp.float32))
    a = -jnp.exp(a_log.astype(jnp.float32))
    h0 = jnp.zeros((bsz, 2, SSD_HEADS, SSD_P, SSD_N), jnp.float32) if ctx is None else ctx[0]
    y_c, s_c = bidir_ssd(xh, dt, a, bm, cm, h0)
    y_c = y_c + xh.astype(jnp.float32) * d_skip.astype(jnp.float32)[:, None]
    y_c = rmsnorm((y_c.reshape(bsz, seq_len, SSD_D_INNER) * jax.nn.silu(z.astype(jnp.float32))).astype(h.dtype), ssd_norm)
    qd = qd.reshape(bsz, seq_len, NA_HEADS, NA_DH)
    kd = kd.reshape(bsz, seq_len, NA_HEADS, NA_DH)
    vd = vd.reshape(bsz, seq_len, NA_HEADS, NA_DH)
    if ctx is None:
        o_d = block_attention(qd, kd, vd)
    else:
        o_d = neighbourhood_attention(qd, kd, vd, ctx[1].astype(kd.dtype), ctx[2].astype(vd.dtype), rpb)
    y = jnp.concatenate([y_c, o_d.reshape(bsz, seq_len, -1)], axis=-1) @ w_out
    if ctx is None:
        return y, s_c, kd, vd
    return y


def sq_relu_mlp(h, w1, w2):
    return jnp.square(jax.nn.relu(h @ w1)) @ w2


def setup_inputs(seed: int = 0) -> dict:
    key = jax.random.key(seed)
    ks = iter(jax.random.split(key, 48))
    nrm = lambda shape, s: jax.random.normal(next(ks), shape, jnp.float32) * s

    def a_log_init(shape):
        return jnp.log(jax.random.uniform(next(ks), shape, jnp.float32, 1.0, 16.0))

    def dt_bias_init(shape):
        dt = jnp.exp(jax.random.uniform(next(ks), shape, jnp.float32, np.log(1e-3), np.log(1e-1)))
        return dt + jnp.log(-jnp.expm1(-dt))

    return {
        'x_prompt': nrm((BATCH, SEQ, D_MODEL), 1.0),
        'x_sample': nrm((DEC_BATCH, DEC_SEQ, D_MODEL), 1.0),
        'state_gdn': nrm((DEC_BATCH, N_EVEN, 2, GDN_HEADS, GDN_DK, GDN_DV), 0.1),
        'cache_gqa_k': nrm((DEC_BATCH, N_EVEN, PAST_LEN, GQA_KV_HEADS, GQA_DH), 1.0),
        'cache_gqa_v': nrm((DEC_BATCH, N_EVEN, PAST_LEN, GQA_KV_HEADS, GQA_DH), 1.0),
        'state_ssd': nrm((DEC_BATCH, N_ODD, 2, SSD_HEADS, SSD_P, SSD_N), 0.1),
        'cache_na_k': nrm((DEC_BATCH, N_ODD, PAST_LEN, NA_HEADS, NA_DH), 1.0),
        'cache_na_v': nrm((DEC_BATCH, N_ODD, PAST_LEN, NA_HEADS, NA_DH), 1.0),
        'c': nrm((DEC_BATCH, D_MODEL), 1.0),
        'c_ctx': nrm((D_MODEL,), 1.0),
        'ada_w': nrm((DEPTH, D_MODEL, 6 * D_MODEL), D_MODEL ** -0.5),
        'ada_b': nrm((DEPTH, 6 * D_MODEL), 0.01),
        'norm_mix_pre': 1.0 + nrm((DEPTH, D_MODEL), 0.1),
        'norm_mix_post': 1.0 + nrm((DEPTH, D_MODEL), 0.1),
        'norm_mlp_pre': 1.0 + nrm((DEPTH, D_MODEL), 0.1),
        'norm_mlp_post': 1.0 + nrm((DEPTH, D_MODEL), 0.1),
        'mlp_w1': nrm((DEPTH, D_MODEL, D_FF), D_MODEL ** -0.5),
        'mlp_w2': nrm((DEPTH, D_FF, D_MODEL), D_FF ** -0.5),
        'ev_w_in': nrm((N_EVEN, D_MODEL, EV_IN), D_MODEL ** -0.5),
        'ev_w_out': nrm((N_EVEN, EV_MIX, D_MODEL), EV_MIX ** -0.5),
        'gdn_conv': nrm((N_EVEN, CONV_W, 2 * GDN_HEADS * GDN_DK + GDN_HEADS * GDN_DV), CONV_W ** -0.5),
        'gdn_a_log': a_log_init((N_EVEN, 2, GDN_HEADS)),
        'gdn_dt_bias': dt_bias_init((N_EVEN, 2, GDN_HEADS)),
        'gdn_norm': 1.0 + nrm((N_EVEN, GDN_DV), 0.1),
        'gqa_q_norm': 1.0 + nrm((N_EVEN, GQA_DH), 0.1),
        'gqa_k_norm': 1.0 + nrm((N_EVEN, GQA_DH), 0.1),
        'od_w_in': nrm((N_ODD, D_MODEL, OD_IN), D_MODEL ** -0.5),
        'od_w_out': nrm((N_ODD, OD_MIX, D_MODEL), OD_MIX ** -0.5),
        'ssd_conv': nrm((N_ODD, CONV_W, SSD_D_INNER + 2 * SSD_GROUPS * SSD_N), CONV_W ** -0.5),
        'ssd_conv_b': nrm((N_ODD, SSD_D_INNER + 2 * SSD_GROUPS * SSD_N), 0.02),
        'ssd_a_log': a_log_init((N_ODD, 2, SSD_HEADS)),
        'ssd_dt_bias': dt_bias_init((N_ODD, 2, SSD_HEADS)),
        'ssd_d': 1.0 + nrm((N_ODD, SSD_HEADS), 0.1),
        'ssd_norm': 1.0 + nrm((N_ODD, SSD_D_INNER), 0.1),
        'na_rpb': nrm((N_ODD, NA_HEADS, 2 * NA_WIN_R - 1, 2 * NA_WIN_C - 1), 0.1),
    }


def reference(x_prompt, x_sample, state_gdn, cache_gqa_k, cache_gqa_v, state_ssd, cache_na_k, cache_na_v,
              c, c_ctx, ada_w, ada_b, norm_mix_pre, norm_mix_post, norm_mlp_pre, norm_mlp_post,
              mlp_w1, mlp_w2, ev_w_in, ev_w_out, gdn_conv, gdn_a_log, gdn_dt_bias, gdn_norm,
              gqa_q_norm, gqa_k_norm, od_w_in, od_w_out, ssd_conv, ssd_conv_b, ssd_a_log, ssd_dt_bias,
              ssd_d, ssd_norm, na_rpb):
    ang = rope_angles(x_sample.shape[1])
    xp, xs = x_prompt, x_sample
    new_gdn, new_gk, new_gv, new_ssd, new_nk, new_nv = [], [], [], [], [], []
    for i in range(DEPTH):
        m_ctx = (jax.nn.silu(c_ctx) @ ada_w[i] + ada_b[i])[None, None, :]
        m_lat = (jax.nn.silu(c) @ ada_w[i] + ada_b[i])[:, None, :]
        sh1c, sc1c, g1c, sh2c, sc2c, g2c = jnp.split(m_ctx, 6, axis=-1)
        sh1l, sc1l, g1l, sh2l, sc2l, g2l = jnp.split(m_lat, 6, axis=-1)
        hc = modulate(xp, norm_mix_pre[i], sh1c, sc1c)
        hl = modulate(xs, norm_mix_pre[i], sh1l, sc1l)
        j = i // 2
        if i % 2 == 0:
            ew = (ev_w_in[j], ev_w_out[j], gdn_conv[j], gdn_a_log[j], gdn_dt_bias[j], gdn_norm[j],
                  gqa_q_norm[j], gqa_k_norm[j])
            yc, s_a, k_b, v_b = even_mixer(hc, *ew)
            yl = even_mixer(hl, *ew, ang=ang, ctx=(state_gdn[:, j], cache_gqa_k[:, j], cache_gqa_v[:, j]))
            new_gdn.append(s_a)
            new_gk.append(k_b)
            new_gv.append(v_b)
        else:
            ow = (od_w_in[j], od_w_out[j], ssd_conv[j], ssd_conv_b[j], ssd_a_log[j], ssd_dt_bias[j],
                  ssd_d[j], ssd_norm[j], na_rpb[j])
            yc, s_c, k_d, v_d = odd_mixer(hc, *ow)
            yl = odd_mixer(hl, *ow, ctx=(state_ssd[:, j], cache_na_k[:, j], cache_na_v[:, j]))
            new_ssd.append(s_c)
            new_nk.append(k_d)
            new_nv.append(v_d)
        xp = xp + g1c * rmsnorm(yc, norm_mix_post[i])
        xs = xs + g1l * rmsnorm(yl, norm_mix_post[i])
        xp = xp + g2c * rmsnorm(sq_relu_mlp(modulate(xp, norm_mlp_pre[i], sh2c, sc2c), mlp_w1[i], mlp_w2[i]), norm_mlp_post[i])
        xs = xs + g2l * rmsnorm(sq_relu_mlp(modulate(xs, norm_mlp_pre[i], sh2l, sc2l), mlp_w1[i], mlp_w2[i]), norm_mlp_post[i])
    return (xp, xs, jnp.stack(new_gdn, axis=1), jnp.stack(new_gk, axis=1), jnp.stack(new_gv, axis=1),
            jnp.stack(new_ssd, axis=1), jnp.stack(new_nk, axis=1), jnp.stack(new_nv, axis=1))
```

```python
import functools

import numpy as np
import jax
import jax.numpy as jnp
from jax import lax
from jax.experimental import pallas as pl
from jax.experimental.pallas import tpu as pltpu

F32 = jnp.float32
BF16 = jnp.bfloat16

D_MODEL = 1024
DEPTH = 4
GRID_W = 64
EPS = 1e-6
ROPE_THETA = 10000.0
D_FF = 4 * D_MODEL
HEAD = 128
GDN_HEADS = 4
GQA_HEADS = 4
GQA_KV_HEADS = 2
SSD_HEADS = 8
SSD_P = 64
SSD_GROUPS = 2
SSD_HPG = SSD_HEADS // SSD_GROUPS
NA_HEADS = 4
NA_WIN_R = 8
NA_WIN_C = 16
CHUNK = 128
NEG = -1e30
VMEM_LIMIT_BYTES = 56 * 1024 * 1024


def _cparams(*sem):
    return pltpu.CompilerParams(dimension_semantics=sem, vmem_limit_bytes=VMEM_LIMIT_BYTES)


def _dot(a, b):
    return jnp.dot(a.astype(BF16), b.astype(BF16), preferred_element_type=F32)


def _dot_nt(a, b):
    return lax.dot_general(a.astype(BF16), b.astype(BF16), (((1,), (1,)), ((), ())),
                           preferred_element_type=F32)


def _split_bf16(a):
    hi = a.astype(BF16)
    lo = (a - hi.astype(F32)).astype(BF16)
    return hi, lo


def _dot3(a, b):
    ah, al = _split_bf16(a)
    bh, bl = _split_bf16(b)
    d = lambda x, y: jnp.dot(x, y, preferred_element_type=F32)
    return d(ah, bh) + (d(ah, bl) + d(al, bh))


def _silu(x):
    return x * jax.nn.sigmoid(x)


def _softplus(x):
    return jnp.maximum(x, 0.0) + jnp.log(1.0 + jnp.exp(-jnp.abs(x)))


def _rms(x, w):
    return x * lax.rsqrt(jnp.mean(x * x, axis=-1, keepdims=True) + EPS) * w


def _iota(shape, axis):
    return lax.broadcasted_iota(jnp.int32, shape, axis)


def _lane_col(tile, idx):
    return jnp.sum(jnp.where(_iota(tile.shape, 1) == idx, tile, 0.0), axis=-1, keepdims=True)


def _cumsum_rows(t):
    rows = _iota(t.shape, 0)
    s = 1
    while s < t.shape[0]:
        t = t + jnp.where(rows >= s, pltpu.roll(t, s, 0), 0.0)
        s *= 2
    return t


def _row_form(col):
    n = col.shape[0]
    return jnp.transpose(jnp.broadcast_to(col, (n, n)))


def _unit_tri_inverse(a, ri, ci):
    same = lambda sh: jnp.right_shift(ri, sh) == jnp.right_shift(ci, sh)
    p = jnp.where(same(4), a, 0.0)
    m = jnp.where(ri == ci, 1.0, 0.0) - p
    for _ in range(3):
        p = _dot3(p, p)
        m = m + _dot3(m, p)
    for sh in (4, 5, 6):
        off = jnp.where(same(sh + 1), jnp.where(same(sh), 0.0, a), 0.0)
        m = m - _dot3(m, _dot3(off, m))
    return m


def _conv3(ref, r0, c, nchunks, seq_len, w, lead=()):
    x = ref[lead + (pl.ds(r0, CHUNK), slice(None))]
    prev = ref[lead + (pl.ds(jnp.maximum(r0 - 1, 0), 1), slice(None))]
    nxt = ref[lead + (pl.ds(jnp.minimum(r0 + CHUNK, seq_len - 1), 1), slice(None))]
    prev = jnp.where(c == 0, 0.0, prev)
    nxt = jnp.where(c == nchunks - 1, 0.0, nxt)
    rows = _iota(x.shape, 0)
    xm = jnp.where(rows == 0, prev, pltpu.roll(x, 1, 0))
    xp = jnp.where(rows == CHUNK - 1, nxt, pltpu.roll(x, CHUNK - 1, 0))
    return xm * w[0:1, :] + x * w[1:2, :] + xp * w[2:3, :]


def _ada_kernel(c_ref, w_ref, b_ref, o_ref):
    o_ref[0] = _dot(_silu(c_ref[...]), w_ref[0]) + b_ref[0]


def _ada(c16, ada_w, ada_b):
    tn = 1536
    n = ada_w.shape[-1]
    return pl.pallas_call(
        _ada_kernel,
        grid=(DEPTH, n // tn),
        in_specs=[pl.BlockSpec((16, D_MODEL), lambda i, j: (0, 0)),
                  pl.BlockSpec((1, D_MODEL, tn), lambda i, j: (i, 0, j)),
                  pl.BlockSpec((1, 1, tn), lambda i, j: (i, 0, j))],
        out_specs=pl.BlockSpec((1, 16, tn), lambda i, j: (i, 0, j)),
        out_shape=jax.ShapeDtypeStruct((DEPTH, 16, n), F32),
        compiler_params=_cparams("parallel", "parallel"),
        name="ada",
    )(c16, ada_w, ada_b.reshape(DEPTH, 1, n))


def _mod_spec(mod):
    if mod.shape[0] == 1:
        return pl.BlockSpec((1, 6, D_MODEL), lambda b, t: (0, 0, 0))
    return pl.BlockSpec((1, 6, D_MODEL), lambda b, t: (b, 0, 0))


def _inproj_kernel(x_ref, mod_ref, nw_ref, w_ref, *o_refs, splits):
    h = _rms(x_ref[0], nw_ref[...])
    h = h * (1.0 + mod_ref[0, 1:2, :]) + mod_ref[0, 0:1, :]
    hb = h.astype(BF16)
    for o_ref, (a, b) in zip(o_refs, splits):
        o_ref[0] = jnp.dot(hb, w_ref[:, a:b], preferred_element_type=F32)


def _inproj(x, mod, norm_w, w, widths, tm):
    bsz, seq_len, _ = x.shape
    splits, a = [], 0
    for wd in widths:
        splits.append((a, a + wd))
        a += wd
    return pl.pallas_call(
        functools.partial(_inproj_kernel, splits=tuple(splits)),
        grid=(bsz, seq_len // tm),
        in_specs=[pl.BlockSpec((1, tm, D_MODEL), lambda b, t: (b, t, 0)),
                  _mod_spec(mod),
                  pl.BlockSpec((1, D_MODEL), lambda b, t: (0, 0)),
                  pl.BlockSpec(w.shape, lambda b, t: (0, 0))],
        out_specs=[pl.BlockSpec((1, tm, wd), lambda b, t: (b, t, 0)) for wd in widths],
        out_shape=[jax.ShapeDtypeStruct((bsz, seq_len, wd), F32) for wd in widths],
        compiler_params=_cparams("parallel", "parallel"),
        name="inproj",
    )(x, mod, norm_w.reshape(1, D_MODEL), w)


def _outproj_kernel(a_ref, b_ref, x_ref, mod_ref, na_ref, w_ref, nw_ref, o_ref, *, norm_a):
    a = a_ref[0]
    if norm_a:
        a = _rms(a, na_ref[...])
    half = a.shape[-1]
    y = (jnp.dot(a.astype(BF16), w_ref[:half, :], preferred_element_type=F32)
         + jnp.dot(b_ref[0].astype(BF16), w_ref[half:, :], preferred_element_type=F32))
    o_ref[0] = x_ref[0] + mod_ref[0, 2:3, :] * _rms(y, nw_ref[...])


def _outproj(a, b, x, mod, norm_a_w, w, norm_w, tm, norm_a):
    bsz, seq_len, _ = x.shape
    half = a.shape[-1]
    return pl.pallas_call(
        functools.partial(_outproj_kernel, norm_a=norm_a),
        grid=(bsz, seq_len // tm),
        in_specs=[pl.BlockSpec((1, tm, half), lambda bb, t: (bb, t, 0)),
                  pl.BlockSpec((1, tm, half), lambda bb, t: (bb, t, 0)),
                  pl.BlockSpec((1, tm, D_MODEL), lambda bb, t: (bb, t, 0)),
                  _mod_spec(mod),
                  pl.BlockSpec((1, half), lambda bb, t: (0, 0)),
                  pl.BlockSpec(w.shape, lambda bb, t: (0, 0)),
                  pl.BlockSpec((1, D_MODEL), lambda bb, t: (0, 0))],
        out_specs=pl.BlockSpec((1, tm, D_MODEL), lambda bb, t: (bb, t, 0)),
        out_shape=jax.ShapeDtypeStruct(x.shape, F32),
        compiler_params=_cparams("parallel", "parallel"),
        name="outproj",
    )(a, b, x, mod, norm_a_w.reshape(1, half), w, norm_w.reshape(1, D_MODEL))


def _mlp_kernel(x_ref, mod_ref, npre_ref, w1_ref, w2_ref, npost_ref, o_ref, h_ref, acc_ref):
    f = pl.program_id(2)

    @pl.when(f == 0)
    def _():
        h = _rms(x_ref[0], npre_ref[...])
        h = h * (1.0 + mod_ref[0, 4:5, :]) + mod_ref[0, 3:4, :]
        h_ref[...] = h.astype(BF16)
        acc_ref[...] = jnp.zeros_like(acc_ref)

    u = jnp.maximum(jnp.dot(h_ref[...], w1_ref[...], preferred_element_type=F32), 0.0)
    acc_ref[...] += jnp.dot((u * u).astype(BF16), w2_ref[...], preferred_element_type=F32)

    @pl.when(f == pl.num_programs(2) - 1)
    def _():
        o_ref[0] = x_ref[0] + mod_ref[0, 5:6, :] * _rms(acc_ref[...], npost_ref[...])


def _mlp(x, mod, npre, w1, w2, npost, tm, tf):
    bsz, seq_len, _ = x.shape
    mspec = _mod_spec(mod)
    return pl.pallas_call(
        _mlp_kernel,
        grid=(bsz, seq_len // tm, D_FF // tf),
        in_specs=[pl.BlockSpec((1, tm, D_MODEL), lambda b, t, f: (b, t, 0)),
                  pl.BlockSpec((1, 6, D_MODEL), lambda b, t, f: mspec.index_map(b, t)),
                  pl.BlockSpec((1, D_MODEL), lambda b, t, f: (0, 0)),
                  pl.BlockSpec((D_MODEL, tf), lambda b, t, f: (0, f)),
                  pl.BlockSpec((tf, D_MODEL), lambda b, t, f: (f, 0)),
                  pl.BlockSpec((1, D_MODEL), lambda b, t, f: (0, 0))],
        out_specs=pl.BlockSpec((1, tm, D_MODEL), lambda b, t, f: (b, t, 0)),
        out_shape=jax.ShapeDtypeStruct(x.shape, F32),
        scratch_shapes=[pltpu.VMEM((tm, D_MODEL), BF16), pltpu.VMEM((tm, D_MODEL), F32)],
        compiler_params=_cparams("parallel", "parallel", "arbitrary"),
        name="mlp",
    )(x, mod, npre.reshape(1, D_MODEL), w1, w2, npost.reshape(1, D_MODEL))


def _qkprep_kernel(x_ref, cos_ref, sin_ref, qn_ref, kn_ref, q_ref, k_ref, v_ref, kf_ref):
    cos, sin = cos_ref[...], sin_ref[...]
    lane = _iota(cos.shape, 1)
    first = (lane & 32) == 0

    def prep(xh, w):
        y = _rms(xh, w)
        swapped = jnp.where(first, pltpu.roll(y, HEAD - 32, 1), pltpu.roll(y, 32, 1))
        return y, y * cos + swapped * sin

    nq = GQA_HEADS * HEAD
    for hd in range(GQA_HEADS):
        _, r = prep(x_ref[0, :, hd * HEAD:(hd + 1) * HEAD], qn_ref[...])
        q_ref[0, :, hd * HEAD:(hd + 1) * HEAD] = r.astype(BF16)
    for hd in range(GQA_KV_HEADS):
        y, r = prep(x_ref[0, :, nq + hd * HEAD:nq + (hd + 1) * HEAD], kn_ref[...])
        k_ref[0, :, hd * HEAD:(hd + 1) * HEAD] = r.astype(BF16)
        kf_ref[0, :, hd * HEAD:(hd + 1) * HEAD] = y
    nk = GQA_KV_HEADS * HEAD
    v_ref[0] = x_ref[0, :, nq + nk:nq + 2 * nk].astype(BF16)


def _qkprep(x, cos, sin, q_norm, k_norm, tm):
    bsz, seq_len, width = x.shape
    nq, nk = GQA_HEADS * HEAD, GQA_KV_HEADS * HEAD
    row = lambda wd: pl.BlockSpec((1, tm, wd), lambda b, t: (b, t, 0))
    return pl.pallas_call(
        _qkprep_kernel,
        grid=(bsz, seq_len // tm),
        in_specs=[row(width),
                  pl.BlockSpec((tm, HEAD), lambda b, t: (t, 0)),
                  pl.BlockSpec((tm, HEAD), lambda b, t: (t, 0)),
                  pl.BlockSpec((1, HEAD), lambda b, t: (0, 0)),
                  pl.BlockSpec((1, HEAD), lambda b, t: (0, 0))],
        out_specs=[row(nq), row(nk), row(nk), row(nk)],
        out_shape=[jax.ShapeDtypeStruct((bsz, seq_len, nq), BF16),
                   jax.ShapeDtypeStruct((bsz, seq_len, nk), BF16),
                   jax.ShapeDtypeStruct((bsz, seq_len, nk), BF16),
                   jax.ShapeDtypeStruct((bsz, seq_len, nk), F32)],
        compiler_params=_cparams("parallel", "parallel"),
        name="qkprep",
    )(x, cos, sin, q_norm.reshape(1, HEAD), k_norm.reshape(1, HEAD))


def _flash_kernel(q_ref, k_ref, v_ref, o_ref, *, rep, tk):
    tq = q_ref.shape[1]
    q = jnp.concatenate([q_ref[0, :, r * HEAD:(r + 1) * HEAD] for r in range(rep)], axis=0)
    nk = k_ref.shape[1] // tk
    scale = HEAD ** -0.5

    def body(i, carry):
        m, l, acc = carry
        r0 = pl.multiple_of(i * tk, tk)
        s = _dot_nt(q, k_ref[0, pl.ds(r0, tk), :]) * scale
        m_new = jnp.maximum(m, jnp.max(s, axis=-1, keepdims=True))
        alpha = jnp.exp(m - m_new)
        p = jnp.exp(s - m_new)
        l = alpha * l + jnp.sum(p, axis=-1, keepdims=True)
        acc = alpha * acc + _dot(p, v_ref[0, pl.ds(r0, tk), :])
        return m_new, l, acc

    rows = rep * tq
    init = (jnp.full((rows, 1), NEG, F32), jnp.zeros((rows, 1), F32), jnp.zeros((rows, HEAD), F32))
    _, l, acc = lax.fori_loop(0, nk, body, init)
    o = acc / l
    for r in range(rep):
        o_ref[0, :, r * HEAD:(r + 1) * HEAD] = o[r * tq:(r + 1) * tq]


def _flash(q, k, v, kv_heads, tq, tk):
    bsz, seq_len, qw = q.shape
    rep = qw // HEAD // kv_heads
    m_len = k.shape[1]
    return pl.pallas_call(
        functools.partial(_flash_kernel, rep=rep, tk=tk),
        grid=(bsz, kv_heads, seq_len // tq),
        in_specs=[pl.BlockSpec((1, tq, rep * HEAD), lambda b, g, t: (b, t, g)),
                  pl.BlockSpec((1, m_len, HEAD), lambda b, g, t: (b, 0, g)),
                  pl.BlockSpec((1, m_len, HEAD), lambda b, g, t: (b, 0, g))],
        out_specs=pl.BlockSpec((1, tq, rep * HEAD), lambda b, g, t: (b, t, g)),
        out_shape=jax.ShapeDtypeStruct((bsz, seq_len, qw), F32),
        compiler_params=_cparams("parallel", "parallel", "parallel"),
        name="flash",
    )(q, k, v)


def _natten_kernel(q_ref, k_ref, v_ref, kc_ref, vc_ref, bias_ref, o_ref):
    rows = q_ref.shape[1] // GRID_W
    scale = HEAD ** -0.5
    nwin = NA_WIN_R * GRID_W
    kc = kc_ref[0].astype(BF16)
    vc = vc_ref[0].astype(BF16)

    def body(r, _):
        r0 = jnp.clip(r - NA_WIN_R // 2, 0, rows - NA_WIN_R)
        q = q_ref[0, pl.ds(pl.multiple_of(r * GRID_W, GRID_W), GRID_W), :]
        w0 = pl.multiple_of(r0 * GRID_W, GRID_W)
        s_loc = _dot_nt(q, k_ref[0, pl.ds(w0, nwin), :]) * scale + bias_ref[0, r0 - r + NA_WIN_R - 1]
        s_ctx = _dot_nt(q, kc) * scale
        m = jnp.maximum(jnp.max(s_loc, axis=-1, keepdims=True), jnp.max(s_ctx, axis=-1, keepdims=True))
        p_loc = jnp.exp(s_loc - m)
        p_ctx = jnp.exp(s_ctx - m)
        l = jnp.sum(p_loc, axis=-1, keepdims=True) + jnp.sum(p_ctx, axis=-1, keepdims=True)
        o = _dot(p_loc, v_ref[0, pl.ds(w0, nwin), :]) + _dot(p_ctx, vc)
        o_ref[0, pl.ds(pl.multiple_of(r * GRID_W, GRID_W), GRID_W), :] = o / l
        return 0

    lax.fori_loop(0, rows, body, 0)


def _natten(qkv, kc, vc, bias):
    bsz, seq_len, _ = qkv.shape
    m_len = kc.shape[1]
    col = lambda off: pl.BlockSpec((1, seq_len, HEAD), lambda b, h: (b, 0, off + h))
    ctx = pl.BlockSpec((1, m_len, HEAD), lambda b, h: (b, 0, h))
    return pl.pallas_call(
        _natten_kernel,
        grid=(bsz, NA_HEADS),
        in_specs=[col(0), col(NA_HEADS), col(2 * NA_HEADS), ctx, ctx,
                  pl.BlockSpec((1,) + bias.shape[1:], lambda b, h: (h, 0, 0, 0))],
        out_specs=pl.BlockSpec((1, seq_len, HEAD), lambda b, h: (b, 0, h)),
        out_shape=jax.ShapeDtypeStruct((bsz, seq_len, NA_HEADS * HEAD), F32),
        compiler_params=_cparams("parallel", "parallel"),
        name="natten",
    )(qkv, qkv, qkv, kc, vc, bias)


def _natten_bias(rpb):
    col = np.arange(GRID_W)
    col_start = np.clip(col - NA_WIN_C // 2, 0, GRID_W - NA_WIN_C)
    kc = np.arange(GRID_W)
    valid = (kc[None, :] >= col_start[:, None]) & (kc[None, :] < col_start[:, None] + NA_WIN_C)
    coff = np.clip(kc[None, :] - col[:, None] + NA_WIN_C - 1, 0, 2 * NA_WIN_C - 2)
    s = np.arange(NA_WIN_R)[:, None] + np.arange(NA_WIN_R)[None, :]
    t = rpb.astype(F32)[:, s][:, :, :, coff]
    t = jnp.where(valid[None, None, None], t, NEG)
    return t.transpose(0, 1, 3, 2, 4).reshape(NA_HEADS, NA_WIN_R, GRID_W, NA_WIN_R * GRID_W)


def _gdn_kernel(q_ref, k_ref, v_ref, z_ref, g_ref, cwq_ref, cwk_ref, cwv_ref, par_ref, nrm_ref, s0_ref,
                o_ref, sfin_ref, qs, ks, vs, oacc):
    seq_len = q_ref.shape[1]
    nchunks = seq_len // CHUNK
    h = pl.program_id(1)

    def prep(c, _):
        r0 = pl.multiple_of(c * CHUNK, CHUNK)
        for src, cw, dst, kind in ((q_ref, cwq_ref, qs, "q"), (k_ref, cwk_ref, ks, "k"), (v_ref, cwv_ref, vs, "v")):
            y = _silu(_conv3(src, r0, c, nchunks, seq_len, cw, lead=(0,)))
            if kind != "v":
                y = y * lax.rsqrt(jnp.sum(y * y, axis=-1, keepdims=True) + EPS)
            if kind == "q":
                y = y * HEAD ** -0.5
            dst[pl.ds(r0, CHUNK), :] = y
        return 0

    lax.fori_loop(0, nchunks, prep, 0)

    ri = _iota((CHUNK, CHUNK), 0)
    ci = _iota((CHUNK, CHUNK), 1)
    neg_a = -jnp.exp(par_ref[0:1, :])
    dt_bias = par_ref[1:2, :]

    def chunk(c, d, state):
        r0 = pl.multiple_of(c * CHUNK, CHUNK)
        q, k, v = qs[pl.ds(r0, CHUNK), :], ks[pl.ds(r0, CHUNK), :], vs[pl.ds(r0, CHUNK), :]
        gates = g_ref[0, pl.ds(r0, CHUNK), :]
        beta = _lane_col(jax.nn.sigmoid(gates), d * GDN_HEADS + h)
        g_t = neg_a * _softplus(gates + dt_bias)
        gc_t = _cumsum_rows(g_t)
        tot_t = jnp.broadcast_to(gc_t[CHUNK - 1:CHUNK, :], gc_t.shape)
        if d == 1:
            gc_t = tot_t - gc_t + g_t
        gidx = 2 * GDN_HEADS + d * GDN_HEADS + h
        gc = _lane_col(gc_t, gidx)
        tot = _lane_col(tot_t, gidx)
        incl = (ri >= ci) if d == 0 else (ri <= ci)
        strict = (ri > ci) if d == 0 else (ri < ci)
        decay = jnp.exp(jnp.where(incl, gc - _row_form(gc), NEG))
        kb = k * beta
        a = jnp.where(strict, _dot_nt(kb, k) * decay, 0.0)
        egc = jnp.exp(gc)
        x = jnp.concatenate([v * beta, kb * egc], axis=-1)
        x = _dot3(_unit_tri_inverse(a, ri, ci), x)
        u, w = x[:, :HEAD], x[:, HEAD:]
        qk = jnp.where(incl, _dot_nt(q, k) * decay, 0.0)
        v_new = u - _dot(w, state)
        o = _dot(q * egc, state) + _dot(qk, v_new)
        k_dec = k * jnp.exp(tot - gc)
        state = state * jnp.exp(tot) + _dot(jnp.transpose(k_dec), v_new)
        oacc[d, pl.ds(r0, CHUNK), :] = o
        return state

    def step(i, carry):
        sf, sb = carry
        return chunk(i, 0, sf), chunk(nchunks - 1 - i, 1, sb)

    sf, sb = lax.fori_loop(0, nchunks, step, (s0_ref[0, 0, 0], s0_ref[0, 1, 0]))
    sfin_ref[0, 0, 0] = sf
    sfin_ref[0, 1, 0] = sb

    def fin(c, _):
        r0 = pl.multiple_of(c * CHUNK, CHUNK)
        o = oacc[0, pl.ds(r0, CHUNK), :] + oacc[1, pl.ds(r0, CHUNK), :]
        o_ref[0, pl.ds(r0, CHUNK), :] = _rms(o, nrm_ref[...]) * _silu(z_ref[0, pl.ds(r0, CHUNK), :])
        return 0

    lax.fori_loop(0, nchunks, fin, 0)


def _gdn(x, gates, conv_w, par, norm_w, s0):
    bsz, seq_len, _ = x.shape
    col = lambda off: pl.BlockSpec((1, seq_len, HEAD), lambda b, h: (b, 0, off + h))
    cw = lambda off: pl.BlockSpec((3, HEAD), lambda b, h: (0, off + h))
    st = pl.BlockSpec((1, 2, 1, HEAD, HEAD), lambda b, h: (b, 0, h, 0, 0))
    return pl.pallas_call(
        _gdn_kernel,
        grid=(bsz, GDN_HEADS),
        in_specs=[col(0), col(GDN_HEADS), col(2 * GDN_HEADS), col(3 * GDN_HEADS),
                  pl.BlockSpec((1, seq_len, HEAD), lambda b, h: (b, 0, 0)),
                  cw(0), cw(GDN_HEADS), cw(2 * GDN_HEADS),
                  pl.BlockSpec((8, HEAD), lambda b, h: (0, 0)),
                  pl.BlockSpec((1, HEAD), lambda b, h: (0, 0)),
                  st],
        out_specs=[pl.BlockSpec((1, seq_len, HEAD), lambda b, h: (b, 0, h)), st],
        out_shape=[jax.ShapeDtypeStruct((bsz, seq_len, GDN_HEADS * HEAD), F32),
                   jax.ShapeDtypeStruct((bsz, 2, GDN_HEADS, HEAD, HEAD), F32)],
        scratch_shapes=[pltpu.VMEM((seq_len, HEAD), F32)] * 3 + [pltpu.VMEM((2, seq_len, HEAD), F32)],
        compiler_params=_cparams("parallel", "parallel"),
        name="gdn",
    )(x, x, x, x, gates, conv_w, conv_w, conv_w, par, norm_w.reshape(1, HEAD), s0)


def _ssd_kernel(x_ref, bm_ref, cm_ref, z_ref, dt_ref, cwx_ref, cwb_ref, cwc_ref, cbx_ref, cbb_ref, cbc_ref,
                par_ref, dsk_ref, h0_ref, y_ref, hfin_ref, xs, bs, cs):
    seq_len = x_ref.shape[1]
    nchunks = seq_len // CHUNK
    grp = pl.program_id(1)
    width = SSD_HPG * SSD_P

    def prep(c, _):
        r0 = pl.multiple_of(c * CHUNK, CHUNK)
        for src, cw, cb, dst in ((x_ref, cwx_ref, cbx_ref, xs), (bm_ref, cwb_ref, cbb_ref, bs),
                                 (cm_ref, cwc_ref, cbc_ref, cs)):
            dst[pl.ds(r0, CHUNK), :] = _silu(_conv3(src, r0, c, nchunks, seq_len, cw, lead=(0,)) + cb[...])
        return 0

    lax.fori_loop(0, nchunks, prep, 0)

    ri = _iota((CHUNK, CHUNK), 0)
    ci = _iota((CHUNK, CHUNK), 1)
    head_of_lane = jnp.right_shift(_iota((CHUNK, width), 1), 6)
    neg_a = -jnp.exp(par_ref[0:1, :])
    dt_bias = par_ref[1:2, :]

    def expand(cols):
        out = jnp.broadcast_to(cols[0], (CHUNK, width))
        for e in range(1, SSD_HPG):
            out = jnp.where(head_of_lane == e, cols[e], out)
        return out

    def chunk(c, d, hstate):
        r0 = pl.multiple_of(c * CHUNK, CHUNK)
        x, bm, cm = xs[pl.ds(r0, CHUNK), :], bs[pl.ds(r0, CHUNK), :], cs[pl.ds(r0, CHUNK), :]
        dt_t = _softplus(dt_ref[0, pl.ds(r0, CHUNK), :] + dt_bias)
        la_t = dt_t * neg_a
        cs_t = _cumsum_rows(la_t)
        tot_row = cs_t[CHUNK - 1:CHUNK, :]
        tot_t = jnp.broadcast_to(tot_row, cs_t.shape)
        if d == 1:
            cs_t = tot_t - cs_t + la_t
        incl = (ri >= ci) if d == 0 else (ri <= ci)
        cb = _dot_nt(cm, bm)
        dts, css, tots, mats = [], [], [], []
        for e in range(SSD_HPG):
            idx = d * SSD_HEADS + grp * SSD_HPG + e
            dts.append(_lane_col(dt_t, idx))
            cse = _lane_col(cs_t, idx)
            css.append(cse)
            tots.append(_lane_col(tot_t, idx))
            mats.append(cb * jnp.exp(jnp.where(incl, cse - _row_form(cse), NEG)))
        xdt = x * expand(dts)
        cs_full = expand(css)
        tot_full = expand(tots)
        y = _dot_nt(cm, hstate) * jnp.exp(cs_full)
        for e in range(SSD_HPG):
            y = y + _dot(mats[e], jnp.where(head_of_lane == e, xdt, 0.0))
        states = _dot(jnp.transpose(xdt * jnp.exp(tot_full - cs_full)), bm)
        tot_p = jnp.broadcast_to(tot_row, (SSD_P, HEAD))
        cd = jnp.concatenate(
            [jnp.broadcast_to(jnp.exp(_lane_col(tot_p, d * SSD_HEADS + grp * SSD_HPG + e)), (SSD_P, HEAD))
             for e in range(SSD_HPG)], axis=0)
        return y, hstate * cd + states

    def fwd(i, hstate):
        y, hstate = chunk(i, 0, hstate)
        y_ref[0, pl.ds(pl.multiple_of(i * CHUNK, CHUNK), CHUNK), :] = y
        return hstate

    hf = lax.fori_loop(0, nchunks, fwd, h0_ref[0, 0].reshape(width, HEAD))
    hfin_ref[0, 0] = hf.reshape(SSD_HPG, SSD_P, HEAD)

    def bwd(i, hstate):
        c = nchunks - 1 - i
        y, hstate = chunk(c, 1, hstate)
        r0 = pl.multiple_of(c * CHUNK, CHUNK)
        tot = y_ref[0, pl.ds(r0, CHUNK), :] + y + xs[pl.ds(r0, CHUNK), :] * dsk_ref[...]
        y_ref[0, pl.ds(r0, CHUNK), :] = tot * _silu(z_ref[0, pl.ds(r0, CHUNK), :])
        return hstate

    hb = lax.fori_loop(0, nchunks, bwd, h0_ref[0, 1].reshape(width, HEAD))
    hfin_ref[0, 1] = hb.reshape(SSD_HPG, SSD_P, HEAD)


def _ssd(x, gates, conv_w, conv_b, par, d_skip, h0):
    bsz, seq_len, _ = x.shape
    width = SSD_HPG * SSD_P
    spec = lambda wd, off: pl.BlockSpec((1, seq_len, wd), lambda b, g: (b, 0, off + g))
    cws = lambda rows, wd, off: pl.BlockSpec((rows, wd), lambda b, g: (0, off + g))
    st = pl.BlockSpec((1, 2, SSD_HPG, SSD_P, HEAD), lambda b, g: (b, 0, g, 0, 0))
    return pl.pallas_call(
        _ssd_kernel,
        grid=(bsz, SSD_GROUPS),
        in_specs=[spec(width, 0), spec(HEAD, 4), spec(HEAD, 6), spec(width, 4),
                  pl.BlockSpec((1, seq_len, HEAD), lambda b, g: (b, 0, 0)),
                  cws(3, width, 0), cws(3, HEAD, 4), cws(3, HEAD, 6),
                  cws(1, width, 0), cws(1, HEAD, 4), cws(1, HEAD, 6),
                  pl.BlockSpec((8, HEAD), lambda b, g: (0, 0)),
                  cws(1, width, 0),
                  st],
        out_specs=[pl.BlockSpec((1, seq_len, width), lambda b, g: (b, 0, g)), st],
        out_shape=[jax.ShapeDtypeStruct((bsz, seq_len, SSD_HEADS * SSD_P), F32),
                   jax.ShapeDtypeStruct((bsz, 2, SSD_HEADS, SSD_P, HEAD), F32)],
        scratch_shapes=[pltpu.VMEM((seq_len, width), F32), pltpu.VMEM((seq_len, HEAD), F32),
                        pltpu.VMEM((seq_len, HEAD), F32)],
        compiler_params=_cparams("parallel", "parallel"),
        name="ssd",
    )(x, x, x, x, gates, conv_w, conv_w, conv_w, conv_b, conv_b, conv_b, par, d_skip, h0)


def _rope_tables(n_tokens):
    half = HEAD // 2
    inv_freq = ROPE_THETA ** (-jnp.arange(0, half, 2, dtype=F32) / half)
    t = jnp.arange(n_tokens)
    ang_r = (t // GRID_W).astype(F32)[:, None] * inv_freq
    ang_c = (t % GRID_W).astype(F32)[:, None] * inv_freq
    cos = jnp.concatenate([jnp.cos(ang_r)] * 2 + [jnp.cos(ang_c)] * 2, axis=-1)
    sin = jnp.concatenate([-jnp.sin(ang_r), jnp.sin(ang_r), -jnp.sin(ang_c), jnp.sin(ang_c)], axis=-1)
    return cos, sin


def _gate_rows(a_log, dt_bias, offset):
    n = a_log.size
    rows = jnp.zeros((8, HEAD), F32)
    rows = rows.at[0, offset:offset + n].set(a_log.reshape(-1).astype(F32))
    return rows.at[1, offset:offset + n].set(dt_bias.reshape(-1).astype(F32))


def _pad_cols(w, width):
    return jnp.pad(w, ((0, 0), (0, width - w.shape[1])))


def kernel(x_prompt, x_sample, state_gdn, cache_gqa_k, cache_gqa_v, state_ssd, cache_na_k, cache_na_v, c, c_ctx, ada_w, ada_b, norm_mix_pre, norm_mix_post, norm_mlp_pre, norm_mlp_post, mlp_w1, mlp_w2, ev_w_in, ev_w_out, gdn_conv, gdn_a_log, gdn_dt_bias, gdn_norm, gqa_q_norm, gqa_k_norm, od_w_in, od_w_out, ssd_conv, ssd_conv_b, ssd_a_log, ssd_dt_bias, ssd_d, ssd_norm, na_rpb):
    nb, ns, _ = x_prompt.shape
    db, dl, _ = x_sample.shape
    past = cache_gqa_k.shape[2]

    c16 = jnp.zeros((16, D_MODEL), F32).at[:db].set(c).at[db].set(c_ctx)
    mods = _ada(c16, ada_w, ada_b).reshape(DEPTH, 16, 6, D_MODEL)

    cos_l, sin_l = _rope_tables(dl)
    cos_c, sin_c = jnp.ones((ns, HEAD), F32), jnp.zeros((ns, HEAD), F32)

    xp = x_prompt.reshape(1, nb * ns, D_MODEL)
    xs = x_sample
    tm_c = min(1024, nb * ns)
    tm_in = min(512, nb * ns, dl)
    ones_half = jnp.ones((D_MODEL // 2,), F32)
    new_gdn, new_gk, new_gv, new_ssd, new_nk, new_nv = [], [], [], [], [], []

    for i in range(DEPTH):
        j = i // 2
        mod_l, mod_c = mods[i, :db], mods[i, db:db + 1]
        if i % 2 == 0:
            wi = ev_w_in[j]
            w_in = jnp.concatenate([wi[:, :2048], wi[:, 2064:], _pad_cols(wi[:, 2048:2064], HEAD)], axis=1).astype(BF16)
            widths = (2048, 1024, HEAD)
            w_out = ev_w_out[j].astype(BF16)
            par = _gate_rows(gdn_a_log[j], gdn_dt_bias[j], 2 * GDN_HEADS)

            gx, ax, gt = _inproj(xp, mod_c, norm_mix_pre[i], w_in, widths, tm_in)
            gx, ax, gt = (t.reshape(nb, ns, -1) for t in (gx, ax, gt))
            o_a, s_a = _gdn(gx, gt, gdn_conv[j], par, gdn_norm[j], jnp.zeros((nb, 2, GDN_HEADS, HEAD, HEAD), F32))
            q, k, v, kf = _qkprep(ax, cos_c, sin_c, gqa_q_norm[j], gqa_k_norm[j], ns)
            o_b = _flash(q, k, v, GQA_KV_HEADS, ns, ns)
            yc = _outproj(o_a.reshape(1, nb * ns, -1), o_b.reshape(1, nb * ns, -1), xp, mod_c, ones_half,
                          w_out, norm_mix_post[i], tm_c, False)
            new_gdn.append(s_a)
            new_gk.append(kf.reshape(nb, ns, GQA_KV_HEADS, HEAD))
            new_gv.append(ax[..., 768:].reshape(nb, ns, GQA_KV_HEADS, HEAD))

            gx, ax, gt = _inproj(xs, mod_l, norm_mix_pre[i], w_in, widths, tm_in)
            o_a, _ = _gdn(gx, gt, gdn_conv[j], par, gdn_norm[j], state_gdn[:, j])
            q, k, v, _ = _qkprep(ax, cos_l, sin_l, gqa_q_norm[j], gqa_k_norm[j], 512)
            kcat = jnp.concatenate([k, cache_gqa_k[:, j].reshape(db, past, -1).astype(BF16)], axis=1)
            vcat = jnp.concatenate([v, cache_gqa_v[:, j].reshape(db, past, -1).astype(BF16)], axis=1)
            o_b = _flash(q, kcat, vcat, GQA_KV_HEADS, 256, 512)
            yl = _outproj(o_a, o_b, xs, mod_l, ones_half, w_out, norm_mix_post[i], 512, False)
        else:
            wi = od_w_in[j]
            w_in = jnp.concatenate([wi[:, 512:1536], wi[:, :512], wi[:, 1552:], _pad_cols(wi[:, 1536:1552], HEAD)], axis=1).astype(BF16)
            widths = (1536, 1536, HEAD)
            w_out = od_w_out[j].astype(BF16)
            par = _gate_rows(ssd_a_log[j], ssd_dt_bias[j], 0)
            dsk = jnp.repeat(ssd_d[j].astype(F32), SSD_P).reshape(1, -1)
            cvw, cvb = ssd_conv[j], ssd_conv_b[j].reshape(1, -1)

            sx, ax, gt = _inproj(xp, mod_c, norm_mix_pre[i], w_in, widths, tm_in)
            sx, ax, gt = (t.reshape(nb, ns, -1) for t in (sx, ax, gt))
            y_c, s_c = _ssd(sx, gt, cvw, cvb, par, dsk, jnp.zeros((nb, 2, SSD_HEADS, SSD_P, HEAD), F32))
            o_d = _flash(ax[..., :512].astype(BF16), ax[..., 512:1024].astype(BF16), ax[..., 1024:].astype(BF16),
                         NA_HEADS, ns, ns)
            yc = _outproj(y_c.reshape(1, nb * ns, -1), o_d.reshape(1, nb * ns, -1), xp, mod_c, ssd_norm[j],
                          w_out, norm_mix_post[i], tm_c, True)
            new_ssd.append(s_c)
            new_nk.append(ax[..., 512:1024].reshape(nb, ns, NA_HEADS, HEAD))
            new_nv.append(ax[..., 1024:].reshape(nb, ns, NA_HEADS, HEAD))

            sx, ax, gt = _inproj(xs, mod_l, norm_mix_pre[i], w_in, widths, tm_in)
            y_c, _ = _ssd(sx, gt, cvw, cvb, par, dsk, state_ssd[:, j])
            o_d = _natten(ax, cache_na_k[:, j].reshape(db, past, -1), cache_na_v[:, j].reshape(db, past, -1),
                          _natten_bias(na_rpb[j]))
            yl = _outproj(y_c, o_d, xs, mod_l, ssd_norm[j], w_out, norm_mix_post[i], 512, True)
        xp, xs = yc, yl
        w1, w2 = mlp_w1[i].astype(BF16), mlp_w2[i].astype(BF16)
        xp = _mlp(xp, mod_c, norm_mlp_pre[i], w1, w2, norm_mlp_post[i], tm_c, 1024)
        xs = _mlp(xs, mod_l, norm_mlp_pre[i], w1, w2, norm_mlp_post[i], 1024, 1024)

    return (xp.reshape(nb, ns, D_MODEL), xs, jnp.stack(new_gdn, axis=1), jnp.stack(new_gk, axis=1),
            jnp.stack(new_gv, axis=1), jnp.stack(new_ssd, axis=1), jnp.stack(new_nk, axis=1),
            jnp.stack(new_nv, axis=1))
```

```python
import functools

import numpy as np
import jax
import jax.numpy as jnp
from jax import lax
from jax.experimental import pallas as pl
from jax.experimental.pallas import tpu as pltpu

F32 = jnp.float32
BF16 = jnp.bfloat16

D_MODEL = 1024
DEPTH = 4
GRID_W = 64
EPS = 1e-6
ROPE_THETA = 10000.0
D_FF = 4 * D_MODEL
HEAD = 128
GDN_HEADS = 4
GQA_HEADS = 4
GQA_KV_HEADS = 2
SSD_HEADS = 8
SSD_P = 64
SSD_GROUPS = 2
SSD_HPG = SSD_HEADS // SSD_GROUPS
NA_HEADS = 4
NA_WIN_R = 8
NA_WIN_C = 16
CHUNK = 128
GDN_UNROLL = 4
TRI_PASSES = (1, 1, 1)
NEG = -1e30
QK_SCALE_LOG2 = HEAD ** -0.5 * float(np.log2(np.e))
VMEM_LIMIT_BYTES = 56 * 1024 * 1024


def _cparams(*sem):
    return pltpu.CompilerParams(dimension_semantics=sem, vmem_limit_bytes=VMEM_LIMIT_BYTES)


def _dot(a, b):
    return jnp.dot(a.astype(BF16), b.astype(BF16), preferred_element_type=F32)


def _dot_nt(a, b):
    return lax.dot_general(a.astype(BF16), b.astype(BF16), (((1,), (1,)), ((), ())),
                           preferred_element_type=F32)


def _split_bf16(a):
    hi = a.astype(BF16)
    lo = (a - hi.astype(F32)).astype(BF16)
    return hi, lo


def _dot3(a, b):
    ah, al = _split_bf16(a)
    bh, bl = _split_bf16(b)
    d = lambda x, y: jnp.dot(x, y, preferred_element_type=F32)
    return d(ah, bh) + (d(ah, bl) + d(al, bh))


def _silu(x):
    return x * jax.nn.sigmoid(x)


def _softplus(x):
    return jnp.maximum(x, 0.0) + jnp.log(1.0 + jnp.exp(-jnp.abs(x)))


def _rms(x, w):
    return x * lax.rsqrt(jnp.mean(x * x, axis=-1, keepdims=True) + EPS) * w


def _iota(shape, axis):
    return lax.broadcasted_iota(jnp.int32, shape, axis)


def _lane_col(tile, idx):
    return jnp.sum(jnp.where(_iota(tile.shape, 1) == idx, tile, 0.0), axis=-1, keepdims=True)


def _cumsum_rows(t):
    rows = _iota(t.shape, 0)
    s = 1
    while s < t.shape[0]:
        t = t + jnp.where(rows >= s, pltpu.roll(t, s, 0), 0.0)
        s *= 2
    return t


def _row_form(col):
    n = col.shape[0]
    return jnp.transpose(jnp.broadcast_to(col, (n, n)))


def _dot3_many(xs, ys):
    xs = [_split_bf16(x) for x in xs]
    ys = [_split_bf16(y) for y in ys]
    d = lambda x, y: jnp.dot(x, y, preferred_element_type=F32)
    hh = [d(x[0], y[0]) for x, y in zip(xs, ys)]
    hl = [d(x[0], y[1]) for x, y in zip(xs, ys)]
    lh = [d(x[1], y[0]) for x, y in zip(xs, ys)]
    return [a + (b + c) for a, b, c in zip(hh, hl, lh)]


def _dots_many(xs, ys, passes):
    if passes == 3:
        return _dot3_many(xs, ys)
    return [_dot(x, y) for x, y in zip(xs, ys)]


def _unit_tri_inverse_many(a_list, ri, ci):
    same = lambda sh: jnp.right_shift(ri, sh) == jnp.right_shift(ci, sh)
    eye = jnp.where(ri == ci, 1.0, 0.0)
    p = [jnp.where(same(4), a, 0.0) for a in a_list]
    m = [eye - x for x in p]
    for _ in range(3):
        p = _dots_many(p, p, TRI_PASSES[0])
        m = [x + y for x, y in zip(m, _dots_many(m, p, TRI_PASSES[0]))]
    for sh in (4, 5, 6):
        off = [jnp.where(same(sh + 1), jnp.where(same(sh), 0.0, a), 0.0) for a in a_list]
        m = [x - y for x, y in zip(m, _dots_many(m, _dots_many(off, m, TRI_PASSES[1]), TRI_PASSES[1]))]
    return m


def _conv3(ref, r0, c, nchunks, seq_len, w, lead=()):
    x = ref[lead + (pl.ds(r0, CHUNK), slice(None))]
    prev = ref[lead + (pl.ds(jnp.maximum(r0 - 1, 0), 1), slice(None))]
    nxt = ref[lead + (pl.ds(jnp.minimum(r0 + CHUNK, seq_len - 1), 1), slice(None))]
    prev = jnp.where(c == 0, 0.0, prev)
    nxt = jnp.where(c == nchunks - 1, 0.0, nxt)
    rows = _iota(x.shape, 0)
    xm = jnp.where(rows == 0, prev, pltpu.roll(x, 1, 0))
    xp = jnp.where(rows == CHUNK - 1, nxt, pltpu.roll(x, CHUNK - 1, 0))
    return xm * w[0:1, :] + x * w[1:2, :] + xp * w[2:3, :]


def _ada_kernel(c_ref, w_ref, b_ref, o_ref):
    o_ref[0] = _dot(_silu(c_ref[...]), w_ref[0]) + b_ref[0]


def _ada(c16, ada_w, ada_b):
    tn = 1536
    n = ada_w.shape[-1]
    return pl.pallas_call(
        _ada_kernel,
        grid=(DEPTH, n // tn),
        in_specs=[pl.BlockSpec((16, D_MODEL), lambda i, j: (0, 0)),
                  pl.BlockSpec((1, D_MODEL, tn), lambda i, j: (i, 0, j)),
                  pl.BlockSpec((1, 1, tn), lambda i, j: (i, 0, j))],
        out_specs=pl.BlockSpec((1, 16, tn), lambda i, j: (i, 0, j)),
        out_shape=jax.ShapeDtypeStruct((DEPTH, 16, n), F32),
        compiler_params=_cparams("parallel", "parallel"),
        name="ada",
    )(c16, ada_w, ada_b.reshape(DEPTH, 1, n))


def _mod_spec(mod):
    if mod.shape[0] == 1:
        return pl.BlockSpec((1, 6, D_MODEL), lambda b, t: (0, 0, 0))
    return pl.BlockSpec((1, 6, D_MODEL), lambda b, t: (b, 0, 0))


def _inproj_kernel(x_ref, mod_ref, nw_ref, w_ref, *o_refs, splits):
    h = _rms(x_ref[0], nw_ref[...])
    h = h * (1.0 + mod_ref[0, 1:2, :]) + mod_ref[0, 0:1, :]
    hb = h.astype(BF16)
    for o_ref, (a, b) in zip(o_refs, splits):
        o_ref[0] = jnp.dot(hb, w_ref[:, a:b], preferred_element_type=F32)


def _inproj(x, mod, norm_w, w, widths, tm):
    bsz, seq_len, _ = x.shape
    splits, a = [], 0
    for wd in widths:
        splits.append((a, a + wd))
        a += wd
    return pl.pallas_call(
        functools.partial(_inproj_kernel, splits=tuple(splits)),
        grid=(bsz, seq_len // tm),
        in_specs=[pl.BlockSpec((1, tm, D_MODEL), lambda b, t: (b, t, 0)),
                  _mod_spec(mod),
                  pl.BlockSpec((1, D_MODEL), lambda b, t: (0, 0)),
                  pl.BlockSpec(w.shape, lambda b, t: (0, 0))],
        out_specs=[pl.BlockSpec((1, tm, wd), lambda b, t: (b, t, 0)) for wd in widths],
        out_shape=[jax.ShapeDtypeStruct((bsz, seq_len, wd), F32) for wd in widths],
        compiler_params=_cparams("parallel", "parallel"),
        name="inproj",
    )(x, mod, norm_w.reshape(1, D_MODEL), w)


def _outproj_kernel(a_ref, b_ref, x_ref, mod_ref, na_ref, w_ref, nw_ref, o_ref, *, norm_a):
    a = a_ref[0]
    if norm_a:
        a = _rms(a, na_ref[...])
    half = a.shape[-1]
    y = (jnp.dot(a.astype(BF16), w_ref[:half, :], preferred_element_type=F32)
         + jnp.dot(b_ref[0].astype(BF16), w_ref[half:, :], preferred_element_type=F32))
    o_ref[0] = x_ref[0] + mod_ref[0, 2:3, :] * _rms(y, nw_ref[...])


def _outproj(a, b, x, mod, norm_a_w, w, norm_w, tm, norm_a):
    bsz, seq_len, _ = x.shape
    half = a.shape[-1]
    return pl.pallas_call(
        functools.partial(_outproj_kernel, norm_a=norm_a),
        grid=(bsz, seq_len // tm),
        in_specs=[pl.BlockSpec((1, tm, half), lambda bb, t: (bb, t, 0)),
                  pl.BlockSpec((1, tm, half), lambda bb, t: (bb, t, 0)),
                  pl.BlockSpec((1, tm, D_MODEL), lambda bb, t: (bb, t, 0)),
                  _mod_spec(mod),
                  pl.BlockSpec((1, half), lambda bb, t: (0, 0)),
                  pl.BlockSpec(w.shape, lambda bb, t: (0, 0)),
                  pl.BlockSpec((1, D_MODEL), lambda bb, t: (0, 0))],
        out_specs=pl.BlockSpec((1, tm, D_MODEL), lambda bb, t: (bb, t, 0)),
        out_shape=jax.ShapeDtypeStruct(x.shape, F32),
        compiler_params=_cparams("parallel", "parallel"),
        name="outproj",
    )(a, b, x, mod, norm_a_w.reshape(1, half), w, norm_w.reshape(1, D_MODEL))


def _mlp_kernel(x_ref, mod_ref, npre_ref, w1_ref, w2_ref, npost_ref, o_ref, h_ref, acc_ref):
    f = pl.program_id(2)

    @pl.when(f == 0)
    def _():
        h = _rms(x_ref[0], npre_ref[...])
        h = h * (1.0 + mod_ref[0, 4:5, :]) + mod_ref[0, 3:4, :]
        h_ref[...] = h.astype(BF16)
        acc_ref[...] = jnp.zeros_like(acc_ref)

    u = jnp.maximum(jnp.dot(h_ref[...], w1_ref[...], preferred_element_type=F32), 0.0)
    acc_ref[...] += jnp.dot((u * u).astype(BF16), w2_ref[...], preferred_element_type=F32)

    @pl.when(f == pl.num_programs(2) - 1)
    def _():
        o_ref[0] = x_ref[0] + mod_ref[0, 5:6, :] * _rms(acc_ref[...], npost_ref[...])


def _mlp(x, mod, npre, w1, w2, npost, tm, tf):
    bsz, seq_len, _ = x.shape
    mspec = _mod_spec(mod)
    return pl.pallas_call(
        _mlp_kernel,
        grid=(bsz, seq_len // tm, D_FF // tf),
        in_specs=[pl.BlockSpec((1, tm, D_MODEL), lambda b, t, f: (b, t, 0)),
                  pl.BlockSpec((1, 6, D_MODEL), lambda b, t, f: mspec.index_map(b, t)),
                  pl.BlockSpec((1, D_MODEL), lambda b, t, f: (0, 0)),
                  pl.BlockSpec((D_MODEL, tf), lambda b, t, f: (0, f)),
                  pl.BlockSpec((tf, D_MODEL), lambda b, t, f: (f, 0)),
                  pl.BlockSpec((1, D_MODEL), lambda b, t, f: (0, 0))],
        out_specs=pl.BlockSpec((1, tm, D_MODEL), lambda b, t, f: (b, t, 0)),
        out_shape=jax.ShapeDtypeStruct(x.shape, F32),
        scratch_shapes=[pltpu.VMEM((tm, D_MODEL), BF16), pltpu.VMEM((tm, D_MODEL), F32)],
        compiler_params=_cparams("parallel", "parallel", "arbitrary"),
        name="mlp",
    )(x, mod, npre.reshape(1, D_MODEL), w1, w2, npost.reshape(1, D_MODEL))


def _qkprep_kernel(x_ref, cos_ref, sin_ref, qn_ref, kn_ref, q_ref, k_ref, v_ref, kf_ref):
    cos, sin = cos_ref[...], sin_ref[...]
    lane = _iota(cos.shape, 1)
    first = (lane & 32) == 0

    def prep(xh, w):
        y = _rms(xh, w)
        swapped = jnp.where(first, pltpu.roll(y, HEAD - 32, 1), pltpu.roll(y, 32, 1))
        return y, y * cos + swapped * sin

    nq = GQA_HEADS * HEAD
    for hd in range(GQA_HEADS):
        _, r = prep(x_ref[0, :, hd * HEAD:(hd + 1) * HEAD], qn_ref[...])
        q_ref[0, :, hd * HEAD:(hd + 1) * HEAD] = (r * QK_SCALE_LOG2).astype(BF16)
    for hd in range(GQA_KV_HEADS):
        y, r = prep(x_ref[0, :, nq + hd * HEAD:nq + (hd + 1) * HEAD], kn_ref[...])
        k_ref[0, :, hd * HEAD:(hd + 1) * HEAD] = r.astype(BF16)
        kf_ref[0, :, hd * HEAD:(hd + 1) * HEAD] = y
    nk = GQA_KV_HEADS * HEAD
    v_ref[0] = x_ref[0, :, nq + nk:nq + 2 * nk].astype(BF16)


def _qkprep(x, cos, sin, q_norm, k_norm, tm):
    bsz, seq_len, width = x.shape
    nq, nk = GQA_HEADS * HEAD, GQA_KV_HEADS * HEAD
    row = lambda wd: pl.BlockSpec((1, tm, wd), lambda b, t: (b, t, 0))
    return pl.pallas_call(
        _qkprep_kernel,
        grid=(bsz, seq_len // tm),
        in_specs=[row(width),
                  pl.BlockSpec((tm, HEAD), lambda b, t: (t, 0)),
                  pl.BlockSpec((tm, HEAD), lambda b, t: (t, 0)),
                  pl.BlockSpec((1, HEAD), lambda b, t: (0, 0)),
                  pl.BlockSpec((1, HEAD), lambda b, t: (0, 0))],
        out_specs=[row(nq), row(nk), row(nk), row(nk)],
        out_shape=[jax.ShapeDtypeStruct((bsz, seq_len, nq), BF16),
                   jax.ShapeDtypeStruct((bsz, seq_len, nk), BF16),
                   jax.ShapeDtypeStruct((bsz, seq_len, nk), BF16),
                   jax.ShapeDtypeStruct((bsz, seq_len, nk), F32)],
        compiler_params=_cparams("parallel", "parallel"),
        name="qkprep",
    )(x, cos, sin, q_norm.reshape(1, HEAD), k_norm.reshape(1, HEAD))


def _flash_kernel(q_ref, k_ref, v_ref, o_ref, *, rep, tk, prescaled):
    tq = q_ref.shape[1]
    if rep == 2:
        qs = [q_ref[0, :, r * HEAD:(r + 1) * HEAD] for r in range(rep)]
    else:
        qs = [q_ref[0, :tq // 2, :], q_ref[0, tq // 2:, :]]
    if not prescaled:
        qs = [(q.astype(F32) * QK_SCALE_LOG2).astype(BF16) for q in qs]
    nk = k_ref.shape[1] // tk
    rows = qs[0].shape[0]

    def lane_chunks(s):
        return [s[:, j * HEAD:(j + 1) * HEAD] for j in range(tk // HEAD)]

    def row_max(i, part):
        k = k_ref[0, pl.ds(pl.multiple_of(i * tk, tk), tk), :]
        s = [_dot_nt(q, k) for q in qs]
        return tuple(functools.reduce(jnp.maximum, lane_chunks(sc), m) for sc, m in zip(s, part))

    part = lax.fori_loop(0, nk, row_max, tuple(jnp.full((rows, HEAD), NEG, F32) for _ in qs))
    mx = [jnp.broadcast_to(jnp.max(m, axis=-1, keepdims=True), (rows, HEAD)) for m in part]

    def accumulate(i, carry):
        r0 = pl.multiple_of(i * tk, tk)
        k, v = k_ref[0, pl.ds(r0, tk), :], v_ref[0, pl.ds(r0, tk), :].astype(BF16)
        s = [_dot_nt(q, k) for q in qs]
        ps, ls = [], []
        for sc, m, (l, _) in zip(s, mx, carry):
            pc = [jnp.exp2(c - m) for c in lane_chunks(sc)]
            ls.append(functools.reduce(jnp.add, pc, l))
            ps.append(jnp.concatenate([c.astype(BF16) for c in pc], axis=-1))
        pv = [jnp.dot(p, v, preferred_element_type=F32) for p in ps]
        return tuple((l, acc + x) for l, (_, acc), x in zip(ls, carry, pv))

    init = tuple((jnp.zeros((rows, HEAD), F32), jnp.zeros((rows, HEAD), F32)) for _ in qs)
    stats = lax.fori_loop(0, nk, accumulate, init)
    outs = [acc / jnp.sum(l, axis=-1, keepdims=True) for l, acc in stats]
    if rep == 2:
        for r in range(rep):
            o_ref[0, :, r * HEAD:(r + 1) * HEAD] = outs[r]
    else:
        o_ref[0, :tq // 2, :] = outs[0]
        o_ref[0, tq // 2:, :] = outs[1]


def _flash(q, k, v, kv_heads, rep, tq, tk, prescaled, head_offsets=(0, 0, 0)):
    bsz, seq_len, _ = q.shape
    assert rep in (1, 2) and head_offsets[0] % rep == 0
    m_len = k.shape[1]
    qo, ko, vo = head_offsets[0] // rep, head_offsets[1], head_offsets[2]
    return pl.pallas_call(
        functools.partial(_flash_kernel, rep=rep, tk=tk, prescaled=prescaled),
        grid=(bsz, kv_heads, seq_len // tq),
        in_specs=[pl.BlockSpec((1, tq, rep * HEAD), lambda b, g, t: (b, t, qo + g)),
                  pl.BlockSpec((1, m_len, HEAD), lambda b, g, t: (b, 0, ko + g)),
                  pl.BlockSpec((1, m_len, HEAD), lambda b, g, t: (b, 0, vo + g))],
        out_specs=pl.BlockSpec((1, tq, rep * HEAD), lambda b, g, t: (b, t, g)),
        out_shape=jax.ShapeDtypeStruct((bsz, seq_len, kv_heads * rep * HEAD), F32),
        compiler_params=_cparams("parallel", "parallel", "parallel"),
        name="flash",
    )(q, k, v)


def _natten_kernel(q_ref, k_ref, v_ref, kc_ref, vc_ref, bias_ref, o_ref):
    rows = q_ref.shape[1] // GRID_W
    scale = HEAD ** -0.5
    nwin = NA_WIN_R * GRID_W
    kc = kc_ref[0].astype(BF16)
    vc = vc_ref[0].astype(BF16)

    def body(r, _):
        r0 = jnp.clip(r - NA_WIN_R // 2, 0, rows - NA_WIN_R)
        q = q_ref[0, pl.ds(pl.multiple_of(r * GRID_W, GRID_W), GRID_W), :]
        w0 = pl.multiple_of(r0 * GRID_W, GRID_W)
        s_loc = _dot_nt(q, k_ref[0, pl.ds(w0, nwin), :]) * scale + bias_ref[0, r0 - r + NA_WIN_R - 1]
        s_ctx = _dot_nt(q, kc) * scale
        m = jnp.maximum(jnp.max(s_loc, axis=-1, keepdims=True), jnp.max(s_ctx, axis=-1, keepdims=True))
        p_loc = jnp.exp(s_loc - m)
        p_ctx = jnp.exp(s_ctx - m)
        l = jnp.sum(p_loc, axis=-1, keepdims=True) + jnp.sum(p_ctx, axis=-1, keepdims=True)
        o = _dot(p_loc, v_ref[0, pl.ds(w0, nwin), :]) + _dot(p_ctx, vc)
        o_ref[0, pl.ds(pl.multiple_of(r * GRID_W, GRID_W), GRID_W), :] = o / l
        return 0

    lax.fori_loop(0, rows, body, 0)


def _natten(qkv, kc, vc, bias):
    bsz, seq_len, _ = qkv.shape
    m_len = kc.shape[1]
    col = lambda off: pl.BlockSpec((1, seq_len, HEAD), lambda b, h: (b, 0, off + h))
    ctx = pl.BlockSpec((1, m_len, HEAD), lambda b, h: (b, 0, h))
    return pl.pallas_call(
        _natten_kernel,
        grid=(bsz, NA_HEADS),
        in_specs=[col(0), col(NA_HEADS), col(2 * NA_HEADS), ctx, ctx,
                  pl.BlockSpec((1,) + bias.shape[1:], lambda b, h: (h, 0, 0, 0))],
        out_specs=pl.BlockSpec((1, seq_len, HEAD), lambda b, h: (b, 0, h)),
        out_shape=jax.ShapeDtypeStruct((bsz, seq_len, NA_HEADS * HEAD), F32),
        compiler_params=_cparams("parallel", "parallel"),
        name="natten",
    )(qkv, qkv, qkv, kc, vc, bias)


def _natten_bias(rpb):
    col = np.arange(GRID_W)
    col_start = np.clip(col - NA_WIN_C // 2, 0, GRID_W - NA_WIN_C)
    kc = np.arange(GRID_W)
    valid = (kc[None, :] >= col_start[:, None]) & (kc[None, :] < col_start[:, None] + NA_WIN_C)
    coff = np.clip(kc[None, :] - col[:, None] + NA_WIN_C - 1, 0, 2 * NA_WIN_C - 2)
    s = np.arange(NA_WIN_R)[:, None] + np.arange(NA_WIN_R)[None, :]
    t = rpb.astype(F32)[:, s][:, :, :, coff]
    t = jnp.where(valid[None, None, None], t, NEG)
    return t.transpose(0, 1, 3, 2, 4).reshape(NA_HEADS, NA_WIN_R, GRID_W, NA_WIN_R * GRID_W)


def _gdn_kernel(q_ref, k_ref, v_ref, z_ref, g_ref, cwq_ref, cwk_ref, cwv_ref, par_ref, nrm_ref, s0_ref,
                o_ref, sfin_ref, qs, ks, vs, oacc):
    seq_len = q_ref.shape[1]
    nchunks = seq_len // CHUNK
    h = pl.program_id(1)

    def prep(c, _):
        r0 = pl.multiple_of(c * CHUNK, CHUNK)
        for src, cw, dst, kind in ((q_ref, cwq_ref, qs, "q"), (k_ref, cwk_ref, ks, "k"), (v_ref, cwv_ref, vs, "v")):
            y = _silu(_conv3(src, r0, c, nchunks, seq_len, cw, lead=(0,)))
            if kind != "v":
                y = y * lax.rsqrt(jnp.sum(y * y, axis=-1, keepdims=True) + EPS)
            if kind == "q":
                y = y * HEAD ** -0.5
            dst[pl.ds(r0, CHUNK), :] = y
        return 0

    lax.fori_loop(0, nchunks, prep, 0)

    ri = _iota((CHUNK, CHUNK), 0)
    ci = _iota((CHUNK, CHUNK), 1)
    neg_a = -jnp.exp(par_ref[0:1, :])
    dt_bias = par_ref[1:2, :]

    def load(c):
        r0 = pl.multiple_of(c * CHUNK, CHUNK)
        return (qs[pl.ds(r0, CHUNK), :], ks[pl.ds(r0, CHUNK), :], vs[pl.ds(r0, CHUNK), :],
                g_ref[0, pl.ds(r0, CHUNK), :])

    def gate_terms(loaded, d):
        q, k, v, gates = loaded
        beta = _lane_col(jax.nn.sigmoid(gates), d * GDN_HEADS + h)
        g_t = neg_a * _softplus(gates + dt_bias)
        gc_t = _cumsum_rows(g_t)
        tot_t = jnp.broadcast_to(gc_t[CHUNK - 1:CHUNK, :], gc_t.shape)
        if d == 1:
            gc_t = tot_t - gc_t + g_t
        gidx = 2 * GDN_HEADS + d * GDN_HEADS + h
        gc = _lane_col(gc_t, gidx)
        tot = _lane_col(tot_t, gidx)
        incl = (ri >= ci) if d == 0 else (ri <= ci)
        decay = jnp.exp(jnp.where(incl, gc - _row_form(gc), NEG))
        egc = jnp.exp(gc)
        kb = k * beta
        return dict(q=q, k=k, kb=kb, incl=incl, strict=(ri > ci) if d == 0 else (ri < ci), decay=decay,
                    x=jnp.concatenate([v * beta, kb * egc], axis=-1), q_dec=q * egc,
                    k_dec=k * jnp.exp(tot - gc), etot=jnp.exp(tot))

    unroll = min(GDN_UNROLL, nchunks)

    def step(i, carry):
        sf, sb = carry
        cf = [i * unroll + j for j in range(unroll)]
        cb = [nchunks - 1 - c for c in cf]
        dirs = [0] * unroll + [1] * unroll
        t = [gate_terms(load(c), d) for c, d in zip(cf + cb, dirs)]
        kk = [_dot_nt(ch["kb"], ch["k"]) for ch in t]
        a = [jnp.where(ch["strict"], m * ch["decay"], 0.0) for ch, m in zip(t, kk)]
        x = _dots_many(_unit_tri_inverse_many(a, ri, ci), [ch["x"] for ch in t], TRI_PASSES[2])
        qk = [_dot_nt(ch["q"], ch["k"]) for ch in t]
        qk = [jnp.where(ch["incl"], m * ch["decay"], 0.0) for ch, m in zip(t, qk)]
        outs = []
        for j in range(unroll):
            idx = (j, unroll + j)
            st = (sf, sb)
            ws = [_dot(x[n][:, HEAD:], s) for n, s in zip(idx, st)]
            v_new = [x[n][:, :HEAD] - w for n, w in zip(idx, ws)]
            o1 = [_dot(t[n]["q_dec"], s) for n, s in zip(idx, st)]
            o2 = [_dot(qk[n], vn) for n, vn in zip(idx, v_new)]
            kv = [_dot(jnp.transpose(t[n]["k_dec"]), vn) for n, vn in zip(idx, v_new)]
            sf, sb = [s * t[n]["etot"] + m for n, s, m in zip(idx, st, kv)]
            outs.append([a1 + a2 for a1, a2 in zip(o1, o2)])
        for j in range(unroll):
            oacc[0, pl.ds(pl.multiple_of(cf[j] * CHUNK, CHUNK), CHUNK), :] = outs[j][0]
            oacc[1, pl.ds(pl.multiple_of(cb[j] * CHUNK, CHUNK), CHUNK), :] = outs[j][1]
        return sf, sb

    sf, sb = lax.fori_loop(0, nchunks // unroll, step, (s0_ref[0, 0, 0], s0_ref[0, 1, 0]))
    sfin_ref[0, 0, 0] = sf
    sfin_ref[0, 1, 0] = sb

    def fin(c, _):
        r0 = pl.multiple_of(c * CHUNK, CHUNK)
        o = oacc[0, pl.ds(r0, CHUNK), :] + oacc[1, pl.ds(r0, CHUNK), :]
        o_ref[0, pl.ds(r0, CHUNK), :] = _rms(o, nrm_ref[...]) * _silu(z_ref[0, pl.ds(r0, CHUNK), :])
        return 0

    lax.fori_loop(0, nchunks, fin, 0)


def _gdn(x, gates, conv_w, par, norm_w, s0):
    bsz, seq_len, _ = x.shape
    col = lambda off: pl.BlockSpec((1, seq_len, HEAD), lambda b, h: (b, 0, off + h))
    cw = lambda off: pl.BlockSpec((3, HEAD), lambda b, h: (0, off + h))
    st = pl.BlockSpec((1, 2, 1, HEAD, HEAD), lambda b, h: (b, 0, h, 0, 0))
    return pl.pallas_call(
        _gdn_kernel,
        grid=(bsz, GDN_HEADS),
        in_specs=[col(0), col(GDN_HEADS), col(2 * GDN_HEADS), col(3 * GDN_HEADS),
                  pl.BlockSpec((1, seq_len, HEAD), lambda b, h: (b, 0, 0)),
                  cw(0), cw(GDN_HEADS), cw(2 * GDN_HEADS),
                  pl.BlockSpec((8, HEAD), lambda b, h: (0, 0)),
                  pl.BlockSpec((1, HEAD), lambda b, h: (0, 0)),
                  st],
        out_specs=[pl.BlockSpec((1, seq_len, HEAD), lambda b, h: (b, 0, h)), st],
        out_shape=[jax.ShapeDtypeStruct((bsz, seq_len, GDN_HEADS * HEAD), F32),
                   jax.ShapeDtypeStruct((bsz, 2, GDN_HEADS, HEAD, HEAD), F32)],
        scratch_shapes=[pltpu.VMEM((seq_len, HEAD), F32)] * 3 + [pltpu.VMEM((2, seq_len, HEAD), F32)],
        compiler_params=_cparams("parallel", "parallel"),
        name="gdn",
    )(x, x, x, x, gates, conv_w, conv_w, conv_w, par, norm_w.reshape(1, HEAD), s0)


def _ssd_kernel(x_ref, bm_ref, cm_ref, z_ref, dt_ref, cwx_ref, cwb_ref, cwc_ref, cbx_ref, cbb_ref, cbc_ref,
                par_ref, dsk_ref, h0_ref, y_ref, hfin_ref, xs, bs, cs):
    seq_len = x_ref.shape[1]
    nchunks = seq_len // CHUNK
    grp = pl.program_id(1)
    width = SSD_HPG * SSD_P

    def prep(c, _):
        r0 = pl.multiple_of(c * CHUNK, CHUNK)
        for src, cw, cb, dst in ((x_ref, cwx_ref, cbx_ref, xs), (bm_ref, cwb_ref, cbb_ref, bs),
                                 (cm_ref, cwc_ref, cbc_ref, cs)):
            dst[pl.ds(r0, CHUNK), :] = _silu(_conv3(src, r0, c, nchunks, seq_len, cw, lead=(0,)) + cb[...])
        return 0

    lax.fori_loop(0, nchunks, prep, 0)

    ri = _iota((CHUNK, CHUNK), 0)
    ci = _iota((CHUNK, CHUNK), 1)
    head_of_lane = jnp.right_shift(_iota((CHUNK, width), 1), 6)
    neg_a = -jnp.exp(par_ref[0:1, :])
    dt_bias = par_ref[1:2, :]

    def expand(cols):
        out = jnp.broadcast_to(cols[0], (CHUNK, width))
        for e in range(1, SSD_HPG):
            out = jnp.where(head_of_lane == e, cols[e], out)
        return out

    def chunk(c, d, hstate):
        r0 = pl.multiple_of(c * CHUNK, CHUNK)
        x, bm, cm = xs[pl.ds(r0, CHUNK), :], bs[pl.ds(r0, CHUNK), :], cs[pl.ds(r0, CHUNK), :]
        dt_t = _softplus(dt_ref[0, pl.ds(r0, CHUNK), :] + dt_bias)
        la_t = dt_t * neg_a
        cs_t = _cumsum_rows(la_t)
        tot_row = cs_t[CHUNK - 1:CHUNK, :]
        tot_t = jnp.broadcast_to(tot_row, cs_t.shape)
        if d == 1:
            cs_t = tot_t - cs_t + la_t
        incl = (ri >= ci) if d == 0 else (ri <= ci)
        cb = _dot_nt(cm, bm)
        dts, css, tots, mats = [], [], [], []
        for e in range(SSD_HPG):
            idx = d * SSD_HEADS + grp * SSD_HPG + e
            dts.append(_lane_col(dt_t, idx))
            cse = _lane_col(cs_t, idx)
            css.append(cse)
            tots.append(_lane_col(tot_t, idx))
            mats.append(cb * jnp.exp(jnp.where(incl, cse - _row_form(cse), NEG)))
        xdt = x * expand(dts)
        cs_full = expand(css)
        tot_full = expand(tots)
        y = _dot_nt(cm, hstate) * jnp.exp(cs_full)
        for e in range(SSD_HPG):
            y = y + _dot(mats[e], jnp.where(head_of_lane == e, xdt, 0.0))
        states = _dot(jnp.transpose(xdt * jnp.exp(tot_full - cs_full)), bm)
        tot_p = jnp.broadcast_to(tot_row, (SSD_P, HEAD))
        cd = jnp.concatenate(
            [jnp.broadcast_to(jnp.exp(_lane_col(tot_p, d * SSD_HEADS + grp * SSD_HPG + e)), (SSD_P, HEAD))
             for e in range(SSD_HPG)], axis=0)
        return y, hstate * cd + states

    def fwd(i, hstate):
        y, hstate = chunk(i, 0, hstate)
        y_ref[0, pl.ds(pl.multiple_of(i * CHUNK, CHUNK), CHUNK), :] = y
        return hstate

    hf = lax.fori_loop(0, nchunks, fwd, h0_ref[0, 0].reshape(width, HEAD))
    hfin_ref[0, 0] = hf.reshape(SSD_HPG, SSD_P, HEAD)

    def bwd(i, hstate):
        c = nchunks - 1 - i
        y, hstate = chunk(c, 1, hstate)
        r0 = pl.multiple_of(c * CHUNK, CHUNK)
        tot = y_ref[0, pl.ds(r0, CHUNK), :] + y + xs[pl.ds(r0, CHUNK), :] * dsk_ref[...]
        y_ref[0, pl.ds(r0, CHUNK), :] = tot * _silu(z_ref[0, pl.ds(r0, CHUNK), :])
        return hstate

    hb = lax.fori_loop(0, nchunks, bwd, h0_ref[0, 1].reshape(width, HEAD))
    hfin_ref[0, 1] = hb.reshape(SSD_HPG, SSD_P, HEAD)


def _ssd(x, gates, conv_w, conv_b, par, d_skip, h0):
    bsz, seq_len, _ = x.shape
    width = SSD_HPG * SSD_P
    spec = lambda wd, off: pl.BlockSpec((1, seq_len, wd), lambda b, g: (b, 0, off + g))
    cws = lambda rows, wd, off: pl.BlockSpec((rows, wd), lambda b, g: (0, off + g))
    st = pl.BlockSpec((1, 2, SSD_HPG, SSD_P, HEAD), lambda b, g: (b, 0, g, 0, 0))
    return pl.pallas_call(
        _ssd_kernel,
        grid=(bsz, SSD_GROUPS),
        in_specs=[spec(width, 0), spec(HEAD, 4), spec(HEAD, 6), spec(width, 4),
                  pl.BlockSpec((1, seq_len, HEAD), lambda b, g: (b, 0, 0)),
                  cws(3, width, 0), cws(3, HEAD, 4), cws(3, HEAD, 6),
                  cws(1, width, 0), cws(1, HEAD, 4), cws(1, HEAD, 6),
                  pl.BlockSpec((8, HEAD), lambda b, g: (0, 0)),
                  cws(1, width, 0),
                  st],
        out_specs=[pl.BlockSpec((1, seq_len, width), lambda b, g: (b, 0, g)), st],
        out_shape=[jax.ShapeDtypeStruct((bsz, seq_len, SSD_HEADS * SSD_P), F32),
                   jax.ShapeDtypeStruct((bsz, 2, SSD_HEADS, SSD_P, HEAD), F32)],
        scratch_shapes=[pltpu.VMEM((seq_len, width), F32), pltpu.VMEM((seq_len, HEAD), F32),
                        pltpu.VMEM((seq_len, HEAD), F32)],
        compiler_params=_cparams("parallel", "parallel"),
        name="ssd",
    )(x, x, x, x, gates, conv_w, conv_w, conv_w, conv_b, conv_b, conv_b, par, d_skip, h0)


def _rope_tables(n_tokens):
    half = HEAD // 2
    inv_freq = ROPE_THETA ** (-jnp.arange(0, half, 2, dtype=F32) / half)
    t = jnp.arange(n_tokens)
    ang_r = (t // GRID_W).astype(F32)[:, None] * inv_freq
    ang_c = (t % GRID_W).astype(F32)[:, None] * inv_freq
    cos = jnp.concatenate([jnp.cos(ang_r)] * 2 + [jnp.cos(ang_c)] * 2, axis=-1)
    sin = jnp.concatenate([-jnp.sin(ang_r), jnp.sin(ang_r), -jnp.sin(ang_c), jnp.sin(ang_c)], axis=-1)
    return cos, sin


def _gate_rows(a_log, dt_bias, offset):
    n = a_log.size
    rows = jnp.zeros((8, HEAD), F32)
    rows = rows.at[0, offset:offset + n].set(a_log.reshape(-1).astype(F32))
    return rows.at[1, offset:offset + n].set(dt_bias.reshape(-1).astype(F32))


def _pad_cols(w, width):
    return jnp.pad(w, ((0, 0), (0, width - w.shape[1])))


def kernel(x_prompt, x_sample, state_gdn, cache_gqa_k, cache_gqa_v, state_ssd, cache_na_k, cache_na_v, c, c_ctx, ada_w, ada_b, norm_mix_pre, norm_mix_post, norm_mlp_pre, norm_mlp_post, mlp_w1, mlp_w2, ev_w_in, ev_w_out, gdn_conv, gdn_a_log, gdn_dt_bias, gdn_norm, gqa_q_norm, gqa_k_norm, od_w_in, od_w_out, ssd_conv, ssd_conv_b, ssd_a_log, ssd_dt_bias, ssd_d, ssd_norm, na_rpb):
    nb, ns, _ = x_prompt.shape
    db, dl, _ = x_sample.shape
    past = cache_gqa_k.shape[2]

    c16 = jnp.zeros((16, D_MODEL), F32).at[:db].set(c).at[db].set(c_ctx)
    mods = _ada(c16, ada_w, ada_b).reshape(DEPTH, 16, 6, D_MODEL)

    cos_l, sin_l = _rope_tables(dl)
    cos_c, sin_c = jnp.ones((ns, HEAD), F32), jnp.zeros((ns, HEAD), F32)

    xp = x_prompt.reshape(1, nb * ns, D_MODEL)
    xs = x_sample
    tm_c = min(1024, nb * ns)
    tm_in = min(512, nb * ns, dl)
    ones_half = jnp.ones((D_MODEL // 2,), F32)
    new_gdn, new_gk, new_gv, new_ssd, new_nk, new_nv = [], [], [], [], [], []

    for i in range(DEPTH):
        j = i // 2
        mod_l, mod_c = mods[i, :db], mods[i, db:db + 1]
        if i % 2 == 0:
            wi = ev_w_in[j]
            w_in = jnp.concatenate([wi[:, :2048], wi[:, 2064:], _pad_cols(wi[:, 2048:2064], HEAD)], axis=1).astype(BF16)
            widths = (2048, 1024, HEAD)
            w_out = ev_w_out[j].astype(BF16)
            par = _gate_rows(gdn_a_log[j], gdn_dt_bias[j], 2 * GDN_HEADS)

            gx, ax, gt = _inproj(xp, mod_c, norm_mix_pre[i], w_in, widths, tm_in)
            gx, ax, gt = (t.reshape(nb, ns, -1) for t in (gx, ax, gt))
            o_a, s_a = _gdn(gx, gt, gdn_conv[j], par, gdn_norm[j], jnp.zeros((nb, 2, GDN_HEADS, HEAD, HEAD), F32))
            q, k, v, kf = _qkprep(ax, cos_c, sin_c, gqa_q_norm[j], gqa_k_norm[j], ns)
            o_b = _flash(q, k, v, GQA_KV_HEADS, 2, ns, ns, True)
            yc = _outproj(o_a.reshape(1, nb * ns, -1), o_b.reshape(1, nb * ns, -1), xp, mod_c, ones_half,
                          w_out, norm_mix_post[i], tm_c, False)
            new_gdn.append(s_a)
            new_gk.append(kf.reshape(nb, ns, GQA_KV_HEADS, HEAD))
            new_gv.append(ax[..., 768:].reshape(nb, ns, GQA_KV_HEADS, HEAD))

            gx, ax, gt = _inproj(xs, mod_l, norm_mix_pre[i], w_in, widths, tm_in)
            o_a, _ = _gdn(gx, gt, gdn_conv[j], par, gdn_norm[j], state_gdn[:, j])
            q, k, v, _ = _qkprep(ax, cos_l, sin_l, gqa_q_norm[j], gqa_k_norm[j], 512)
            kcat = jnp.concatenate([k, cache_gqa_k[:, j].reshape(db, past, -1).astype(BF16)], axis=1)
            vcat = jnp.concatenate([v, cache_gqa_v[:, j].reshape(db, past, -1).astype(BF16)], axis=1)
            o_b = _flash(q, kcat, vcat, GQA_KV_HEADS, 2, 256, (dl + past) // 3, True)
            yl = _outproj(o_a, o_b, xs, mod_l, ones_half, w_out, norm_mix_post[i], 512, False)
        else:
            wi = od_w_in[j]
            w_in = jnp.concatenate([wi[:, 512:1536], wi[:, :512], wi[:, 1552:], _pad_cols(wi[:, 1536:1552], HEAD)], axis=1).astype(BF16)
            widths = (1536, 1536, HEAD)
            w_out = od_w_out[j].astype(BF16)
            par = _gate_rows(ssd_a_log[j], ssd_dt_bias[j], 0)
            dsk = jnp.repeat(ssd_d[j].astype(F32), SSD_P).reshape(1, -1)
            cvw, cvb = ssd_conv[j], ssd_conv_b[j].reshape(1, -1)

            sx, ax, gt = _inproj(xp, mod_c, norm_mix_pre[i], w_in, widths, tm_in)
            sx, ax, gt = (t.reshape(nb, ns, -1) for t in (sx, ax, gt))
            y_c, s_c = _ssd(sx, gt, cvw, cvb, par, dsk, jnp.zeros((nb, 2, SSD_HEADS, SSD_P, HEAD), F32))
            o_d = _flash(ax, ax, ax, NA_HEADS, 1, ns, ns, False, (0, NA_HEADS, 2 * NA_HEADS))
            yc = _outproj(y_c.reshape(1, nb * ns, -1), o_d.reshape(1, nb * ns, -1), xp, mod_c, ssd_norm[j],
                          w_out, norm_mix_post[i], tm_c, True)
            new_ssd.append(s_c)
            new_nk.append(ax[..., 512:1024].reshape(nb, ns, NA_HEADS, HEAD))
            new_nv.append(ax[..., 1024:].reshape(nb, ns, NA_HEADS, HEAD))

            sx, ax, gt = _inproj(xs, mod_l, norm_mix_pre[i], w_in, widths, tm_in)
            y_c, _ = _ssd(sx, gt, cvw, cvb, par, dsk, state_ssd[:, j])
            o_d = _natten(ax, cache_na_k[:, j].reshape(db, past, -1), cache_na_v[:, j].reshape(db, past, -1),
                          _natten_bias(na_rpb[j]))
            yl = _outproj(y_c, o_d, xs, mod_l, ssd_norm[j], w_out, norm_mix_post[i], 512, True)
        xp, xs = yc, yl
        w1, w2 = mlp_w1[i].astype(BF16), mlp_w2[i].astype(BF16)
        xp = _mlp(xp, mod_c, norm_mlp_pre[i], w1, w2, norm_mlp_post[i], tm_c, 1024)
        xs = _mlp(xs, mod_l, norm_mlp_pre[i], w1, w2, norm_mlp_post[i], 1024, 1024)

    return (xp.reshape(nb, ns, D_MODEL), xs, jnp.stack(new_gdn, axis=1), jnp.stack(new_gk, axis=1),
            jnp.stack(new_gv, axis=1), jnp.stack(new_ssd, axis=1), jnp.stack(new_nk, axis=1),
            jnp.stack(new_nv, axis=1))
```

```python
import functools

import numpy as np
import jax
import jax.numpy as jnp
from jax import lax
from jax.experimental import pallas as pl
from jax.experimental.pallas import tpu as pltpu

F32 = jnp.float32
BF16 = jnp.bfloat16

D_MODEL = 1024
DEPTH = 4
GRID_W = 64
EPS = 1e-6
ROPE_THETA = 10000.0
D_FF = 4 * D_MODEL
HEAD = 128
GDN_HEADS = 4
GQA_HEADS = 4
GQA_KV_HEADS = 2
SSD_HEADS = 8
SSD_P = 64
SSD_GROUPS = 2
SSD_HPG = SSD_HEADS // SSD_GROUPS
NA_HEADS = 4
NA_WIN_R = 8
NA_WIN_C = 16
CHUNK = 128
GDN_UNROLL = 4
SSD_UNROLL = 4
TRI_PASSES = (1, 1, 2)
NEG = -1e30
QK_SCALE_LOG2 = HEAD ** -0.5 * float(np.log2(np.e))
VMEM_LIMIT_BYTES = 56 * 1024 * 1024


def _cparams(*sem):
    return pltpu.CompilerParams(dimension_semantics=sem, vmem_limit_bytes=VMEM_LIMIT_BYTES)


def _dot(a, b):
    return jnp.dot(a.astype(BF16), b.astype(BF16), preferred_element_type=F32)


def _dot_nt(a, b):
    return lax.dot_general(a.astype(BF16), b.astype(BF16), (((1,), (1,)), ((), ())),
                           preferred_element_type=F32)


def _split_bf16(a):
    hi = a.astype(BF16)
    lo = (a - hi.astype(F32)).astype(BF16)
    return hi, lo


def _dot3(a, b):
    ah, al = _split_bf16(a)
    bh, bl = _split_bf16(b)
    d = lambda x, y: jnp.dot(x, y, preferred_element_type=F32)
    return d(ah, bh) + (d(ah, bl) + d(al, bh))


def _silu(x):
    return x * jax.nn.sigmoid(x)


def _softplus(x):
    return jnp.maximum(x, 0.0) + jnp.log(1.0 + jnp.exp(-jnp.abs(x)))


def _rms(x, w):
    return x * lax.rsqrt(jnp.mean(x * x, axis=-1, keepdims=True) + EPS) * w


def _iota(shape, axis):
    return lax.broadcasted_iota(jnp.int32, shape, axis)


def _lane_col(tile, idx):
    return jnp.sum(jnp.where(_iota(tile.shape, 1) == idx, tile, 0.0), axis=-1, keepdims=True)


def _cumsum_rows(t):
    rows = _iota(t.shape, 0)
    s = 1
    while s < t.shape[0]:
        t = t + jnp.where(rows >= s, pltpu.roll(t, s, 0), 0.0)
        s *= 2
    return t


def _row_form(col):
    n = col.shape[0]
    return jnp.transpose(jnp.broadcast_to(col, (n, n)))


def _dot3_many(xs, ys):
    xs = [_split_bf16(x) for x in xs]
    ys = [_split_bf16(y) for y in ys]
    d = lambda x, y: jnp.dot(x, y, preferred_element_type=F32)
    hh = [d(x[0], y[0]) for x, y in zip(xs, ys)]
    hl = [d(x[0], y[1]) for x, y in zip(xs, ys)]
    lh = [d(x[1], y[0]) for x, y in zip(xs, ys)]
    return [a + (b + c) for a, b, c in zip(hh, hl, lh)]


def _dots_many(xs, ys, passes):
    if passes == 3:
        return _dot3_many(xs, ys)
    if passes == 2:
        d = lambda x, y: jnp.dot(x, y, preferred_element_type=F32)
        xs = [x.astype(BF16) for x in xs]
        ys = [_split_bf16(y) for y in ys]
        hi = [d(x, y[0]) for x, y in zip(xs, ys)]
        lo = [d(x, y[1]) for x, y in zip(xs, ys)]
        return [a + b for a, b in zip(hi, lo)]
    return [_dot(x, y) for x, y in zip(xs, ys)]


def _unit_tri_inverse_many(a_list, ri, ci):
    same = lambda sh: jnp.right_shift(ri, sh) == jnp.right_shift(ci, sh)
    eye = jnp.where(ri == ci, 1.0, 0.0)
    p = [jnp.where(same(4), a, 0.0) for a in a_list]
    m = [eye - x for x in p]
    for _ in range(3):
        p = _dots_many(p, p, TRI_PASSES[0])
        m = [x + y for x, y in zip(m, _dots_many(m, p, TRI_PASSES[0]))]
    for sh in (4, 5, 6):
        off = [jnp.where(same(sh + 1), jnp.where(same(sh), 0.0, a), 0.0) for a in a_list]
        m = [x - y for x, y in zip(m, _dots_many(m, _dots_many(off, m, TRI_PASSES[1]), TRI_PASSES[1]))]
    return m


def _conv3(ref, r0, c, nchunks, seq_len, w, lead=()):
    x = ref[lead + (pl.ds(r0, CHUNK), slice(None))]
    prev = ref[lead + (pl.ds(jnp.maximum(r0 - 1, 0), 1), slice(None))]
    nxt = ref[lead + (pl.ds(jnp.minimum(r0 + CHUNK, seq_len - 1), 1), slice(None))]
    prev = jnp.where(c == 0, 0.0, prev)
    nxt = jnp.where(c == nchunks - 1, 0.0, nxt)
    rows = _iota(x.shape, 0)
    xm = jnp.where(rows == 0, prev, pltpu.roll(x, 1, 0))
    xp = jnp.where(rows == CHUNK - 1, nxt, pltpu.roll(x, CHUNK - 1, 0))
    return xm * w[0:1, :] + x * w[1:2, :] + xp * w[2:3, :]


def _ada_kernel(c_ref, w_ref, b_ref, o_ref):
    o_ref[0] = _dot(_silu(c_ref[...]), w_ref[0]) + b_ref[0]


def _ada(c16, ada_w, ada_b):
    tn = 1536
    n = ada_w.shape[-1]
    return pl.pallas_call(
        _ada_kernel,
        grid=(DEPTH, n // tn),
        in_specs=[pl.BlockSpec((16, D_MODEL), lambda i, j: (0, 0)),
                  pl.BlockSpec((1, D_MODEL, tn), lambda i, j: (i, 0, j)),
                  pl.BlockSpec((1, 1, tn), lambda i, j: (i, 0, j))],
        out_specs=pl.BlockSpec((1, 16, tn), lambda i, j: (i, 0, j)),
        out_shape=jax.ShapeDtypeStruct((DEPTH, 16, n), F32),
        compiler_params=_cparams("parallel", "parallel"),
        name="ada",
    )(c16, ada_w, ada_b.reshape(DEPTH, 1, n))


def _mod_spec(mod):
    if mod.shape[0] == 1:
        return pl.BlockSpec((1, 6, D_MODEL), lambda b, t: (0, 0, 0))
    return pl.BlockSpec((1, 6, D_MODEL), lambda b, t: (b, 0, 0))


def _inproj_kernel(x_ref, mod_ref, nw_ref, w_ref, *o_refs, splits):
    h = _rms(x_ref[0], nw_ref[...])
    h = h * (1.0 + mod_ref[0, 1:2, :]) + mod_ref[0, 0:1, :]
    hb = h.astype(BF16)
    for o_ref, (a, b) in zip(o_refs, splits):
        o_ref[0] = jnp.dot(hb, w_ref[:, a:b], preferred_element_type=F32)


def _inproj(x, mod, norm_w, w, widths, tm):
    bsz, seq_len, _ = x.shape
    splits, a = [], 0
    for wd in widths:
        splits.append((a, a + wd))
        a += wd
    return pl.pallas_call(
        functools.partial(_inproj_kernel, splits=tuple(splits)),
        grid=(bsz, seq_len // tm),
        in_specs=[pl.BlockSpec((1, tm, D_MODEL), lambda b, t: (b, t, 0)),
                  _mod_spec(mod),
                  pl.BlockSpec((1, D_MODEL), lambda b, t: (0, 0)),
                  pl.BlockSpec(w.shape, lambda b, t: (0, 0))],
        out_specs=[pl.BlockSpec((1, tm, wd), lambda b, t: (b, t, 0)) for wd in widths],
        out_shape=[jax.ShapeDtypeStruct((bsz, seq_len, wd), F32) for wd in widths],
        compiler_params=_cparams("parallel", "parallel"),
        name="inproj",
    )(x, mod, norm_w.reshape(1, D_MODEL), w)


def _outproj_kernel(a_ref, b_ref, x_ref, mod_ref, w_ref, nw_ref, *rest, norm_a):
    a = a_ref[0]
    if norm_a:
        na_ref, o_ref = rest
        a = _rms(a, na_ref[...])
    else:
        (o_ref,) = rest
    half = a.shape[-1]
    y = (jnp.dot(a.astype(BF16), w_ref[:half, :], preferred_element_type=F32)
         + jnp.dot(b_ref[0].astype(BF16), w_ref[half:, :], preferred_element_type=F32))
    o_ref[0] = x_ref[0] + mod_ref[0, 2:3, :] * _rms(y, nw_ref[...])


def _outproj(a, b, x, mod, w, norm_w, tm, norm_a_w=None):
    bsz, seq_len, _ = x.shape
    half = a.shape[-1]
    in_specs = [pl.BlockSpec((1, tm, half), lambda bb, t: (bb, t, 0)),
                pl.BlockSpec((1, tm, half), lambda bb, t: (bb, t, 0)),
                pl.BlockSpec((1, tm, D_MODEL), lambda bb, t: (bb, t, 0)),
                _mod_spec(mod),
                pl.BlockSpec(w.shape, lambda bb, t: (0, 0)),
                pl.BlockSpec((1, D_MODEL), lambda bb, t: (0, 0))]
    args = [a, b, x, mod, w, norm_w.reshape(1, D_MODEL)]
    if norm_a_w is not None:
        in_specs.append(pl.BlockSpec((1, half), lambda bb, t: (0, 0)))
        args.append(norm_a_w.reshape(1, half))
    return pl.pallas_call(
        functools.partial(_outproj_kernel, norm_a=norm_a_w is not None),
        grid=(bsz, seq_len // tm),
        in_specs=in_specs,
        out_specs=pl.BlockSpec((1, tm, D_MODEL), lambda bb, t: (bb, t, 0)),
        out_shape=jax.ShapeDtypeStruct(x.shape, F32),
        compiler_params=_cparams("parallel", "parallel"),
        name="outproj",
    )(*args)


def _mlp_kernel(x_ref, mod_ref, npre_ref, w1_ref, w2_ref, npost_ref, o_ref, h_ref, acc_ref):
    f = pl.program_id(2)

    @pl.when(f == 0)
    def _():
        h = _rms(x_ref[0], npre_ref[...])
        h = h * (1.0 + mod_ref[0, 4:5, :]) + mod_ref[0, 3:4, :]
        h_ref[...] = h.astype(BF16)
        acc_ref[...] = jnp.zeros_like(acc_ref)

    u = jnp.maximum(jnp.dot(h_ref[...], w1_ref[...], preferred_element_type=F32), 0.0)
    acc_ref[...] += jnp.dot((u * u).astype(BF16), w2_ref[...], preferred_element_type=F32)

    @pl.when(f == pl.num_programs(2) - 1)
    def _():
        o_ref[0] = x_ref[0] + mod_ref[0, 5:6, :] * _rms(acc_ref[...], npost_ref[...])


def _mlp(x, mod, npre, w1, w2, npost, tm, tf):
    bsz, seq_len, _ = x.shape
    mspec = _mod_spec(mod)
    return pl.pallas_call(
        _mlp_kernel,
        grid=(bsz, seq_len // tm, D_FF // tf),
        in_specs=[pl.BlockSpec((1, tm, D_MODEL), lambda b, t, f: (b, t, 0)),
                  pl.BlockSpec((1, 6, D_MODEL), lambda b, t, f: mspec.index_map(b, t)),
                  pl.BlockSpec((1, D_MODEL), lambda b, t, f: (0, 0)),
                  pl.BlockSpec((D_MODEL, tf), lambda b, t, f: (0, f)),
                  pl.BlockSpec((tf, D_MODEL), lambda b, t, f: (f, 0)),
                  pl.BlockSpec((1, D_MODEL), lambda b, t, f: (0, 0))],
        out_specs=pl.BlockSpec((1, tm, D_MODEL), lambda b, t, f: (b, t, 0)),
        out_shape=jax.ShapeDtypeStruct(x.shape, F32),
        scratch_shapes=[pltpu.VMEM((tm, D_MODEL), BF16), pltpu.VMEM((tm, D_MODEL), F32)],
        compiler_params=_cparams("parallel", "parallel", "arbitrary"),
        name="mlp",
    )(x, mod, npre.reshape(1, D_MODEL), w1, w2, npost.reshape(1, D_MODEL))


def _qkprep_kernel(x_ref, cos_ref, sin_ref, qn_ref, kn_ref, q_ref, k_ref, v_ref, kf_ref=None):
    cos, sin = cos_ref[...], sin_ref[...]
    lane = _iota(cos.shape, 1)
    first = (lane & 32) == 0

    def prep(xh, w):
        y = _rms(xh, w)
        swapped = jnp.where(first, pltpu.roll(y, HEAD - 32, 1), pltpu.roll(y, 32, 1))
        return y, y * cos + swapped * sin

    nq = GQA_HEADS * HEAD
    for hd in range(GQA_HEADS):
        _, r = prep(x_ref[0, :, hd * HEAD:(hd + 1) * HEAD], qn_ref[...])
        q_ref[0, :, hd * HEAD:(hd + 1) * HEAD] = (r * QK_SCALE_LOG2).astype(BF16)
    for hd in range(GQA_KV_HEADS):
        y, r = prep(x_ref[0, :, nq + hd * HEAD:nq + (hd + 1) * HEAD], kn_ref[...])
        k_ref[0, :, hd * HEAD:(hd + 1) * HEAD] = r.astype(BF16)
        if kf_ref is not None:
            kf_ref[0, :, hd * HEAD:(hd + 1) * HEAD] = y
    nk = GQA_KV_HEADS * HEAD
    v_ref[0] = x_ref[0, :, nq + nk:nq + 2 * nk].astype(BF16)


def _qkprep(x, cos, sin, q_norm, k_norm, tm, want_plain_k):
    bsz, seq_len, width = x.shape
    nq, nk = GQA_HEADS * HEAD, GQA_KV_HEADS * HEAD
    row = lambda wd: pl.BlockSpec((1, tm, wd), lambda b, t: (b, t, 0))
    n_out = 4 if want_plain_k else 3
    return pl.pallas_call(
        _qkprep_kernel,
        grid=(bsz, seq_len // tm),
        in_specs=[row(width),
                  pl.BlockSpec((tm, HEAD), lambda b, t: (t, 0)),
                  pl.BlockSpec((tm, HEAD), lambda b, t: (t, 0)),
                  pl.BlockSpec((1, HEAD), lambda b, t: (0, 0)),
                  pl.BlockSpec((1, HEAD), lambda b, t: (0, 0))],
        out_specs=[row(nq), row(nk), row(nk), row(nk)][:n_out],
        out_shape=[jax.ShapeDtypeStruct((bsz, seq_len, nq), BF16),
                   jax.ShapeDtypeStruct((bsz, seq_len, nk), BF16),
                   jax.ShapeDtypeStruct((bsz, seq_len, nk), BF16),
                   jax.ShapeDtypeStruct((bsz, seq_len, nk), F32)][:n_out],
        compiler_params=_cparams("parallel", "parallel"),
        name="qkprep",
    )(x, cos, sin, q_norm.reshape(1, HEAD), k_norm.reshape(1, HEAD))


def _flash_kernel(q_ref, k_ref, v_ref, *rest, rep, tk, prescaled, with_cache):
    if with_cache:
        kc_ref, vc_ref, o_ref = rest
    else:
        (o_ref,) = rest
    tq = q_ref.shape[1]
    if rep == 2:
        qs = [q_ref[0, :, r * HEAD:(r + 1) * HEAD] for r in range(rep)]
    else:
        qs = [q_ref[0, :tq // 2, :], q_ref[0, tq // 2:, :]]
    if not prescaled:
        qs = [(q.astype(F32) * QK_SCALE_LOG2).astype(BF16) for q in qs]
    nk = k_ref.shape[1] // tk
    rows = qs[0].shape[0]

    def lane_chunks(s):
        return [s[:, j * HEAD:(j + 1) * HEAD] for j in range(s.shape[1] // HEAD)]

    def row_max(k, part):
        s = [_dot_nt(q, k) for q in qs]
        return tuple(functools.reduce(jnp.maximum, lane_chunks(sc), m) for sc, m in zip(s, part))

    def accumulate(k, v, mx, carry):
        s = [_dot_nt(q, k) for q in qs]
        ps, ls = [], []
        for sc, m, (l, _) in zip(s, mx, carry):
            pc = [jnp.exp2(c - m) for c in lane_chunks(sc)]
            ls.append(functools.reduce(jnp.add, pc, l))
            ps.append(jnp.concatenate([c.astype(BF16) for c in pc], axis=-1))
        pv = [jnp.dot(p, v.astype(BF16), preferred_element_type=F32) for p in ps]
        return tuple((l, acc + x) for l, (_, acc), x in zip(ls, carry, pv))

    def block(ref, i):
        return ref[0, pl.ds(pl.multiple_of(i * tk, tk), tk), :]

    part = tuple(jnp.full((rows, HEAD), NEG, F32) for _ in qs)
    part = lax.fori_loop(0, nk, lambda i, p: row_max(block(k_ref, i), p), part)
    if with_cache:
        part = row_max(kc_ref[0], part)
    mx = [jnp.broadcast_to(jnp.max(m, axis=-1, keepdims=True), (rows, HEAD)) for m in part]

    stats = tuple((jnp.zeros((rows, HEAD), F32), jnp.zeros((rows, HEAD), F32)) for _ in qs)
    stats = lax.fori_loop(0, nk, lambda i, c: accumulate(block(k_ref, i), block(v_ref, i), mx, c), stats)
    if with_cache:
        stats = accumulate(kc_ref[0], vc_ref[0], mx, stats)
    outs = [acc / jnp.sum(l, axis=-1, keepdims=True) for l, acc in stats]
    if rep == 2:
        for r in range(rep):
            o_ref[0, :, r * HEAD:(r + 1) * HEAD] = outs[r]
    else:
        o_ref[0, :tq // 2, :] = outs[0]
        o_ref[0, tq // 2:, :] = outs[1]


def _flash(q, k, v, kv_heads, rep, tq, tk, prescaled, head_offsets=(0, 0, 0), cache=None):
    bsz, seq_len, _ = q.shape
    assert rep in (1, 2) and head_offsets[0] % rep == 0
    m_len = k.shape[1]
    qo, ko, vo = head_offsets[0] // rep, head_offsets[1], head_offsets[2]
    in_specs = [pl.BlockSpec((1, tq, rep * HEAD), lambda b, g, t: (b, t, qo + g)),
                pl.BlockSpec((1, m_len, HEAD), lambda b, g, t: (b, 0, ko + g)),
                pl.BlockSpec((1, m_len, HEAD), lambda b, g, t: (b, 0, vo + g))]
    args = [q, k, v]
    if cache is not None:
        layer = cache[2]
        spec = pl.BlockSpec((1, None, cache[0].shape[2], HEAD), lambda b, g, t: (b, layer, 0, g))
        in_specs += [spec, spec]
        args += [cache[0], cache[1]]
    return pl.pallas_call(
        functools.partial(_flash_kernel, rep=rep, tk=tk, prescaled=prescaled, with_cache=cache is not None),
        grid=(bsz, kv_heads, seq_len // tq),
        in_specs=in_specs,
        out_specs=pl.BlockSpec((1, tq, rep * HEAD), lambda b, g, t: (b, t, g)),
        out_shape=jax.ShapeDtypeStruct((bsz, seq_len, kv_heads * rep * HEAD), F32),
        compiler_params=_cparams("parallel", "parallel", "parallel"),
        name="flash",
    )(*args)


NA_GROUP = 4
NA_BAND = NA_GROUP + NA_WIN_R
NA_STEP = 2


def _natten_kernel(q_ref, k_ref, v_ref, kc_ref, vc_ref, bias_ref, o_ref):
    rows = q_ref.shape[1] // GRID_W
    scale = HEAD ** -0.5
    gq = NA_GROUP * GRID_W
    kc = kc_ref[0].astype(BF16)
    vc = vc_ref[0].astype(BF16)
    low_half = _iota((GRID_W, HEAD), 1) < GRID_W

    def group_bias(r_first, band_first):
        per_row = []
        for i in range(NA_GROUP):
            r = r_first + i
            r0 = jnp.clip(r - NA_WIN_R // 2, 0, rows - NA_WIN_R)
            blocks = []
            for jj in range(NA_BAND // 2):
                kr = band_first + 2 * jj
                blk = bias_ref[0, jnp.clip(kr - r + NA_WIN_R, 0, 2 * NA_WIN_R - 1)]
                ok_a = jnp.logical_and(kr >= r0, kr < r0 + NA_WIN_R)
                ok_b = jnp.logical_and(kr + 1 >= r0, kr + 1 < r0 + NA_WIN_R)
                blocks.append(jnp.where(low_half, jnp.where(ok_a, blk, NEG), jnp.where(ok_b, blk, NEG)))
            per_row.append(jnp.concatenate(blocks, axis=-1))
        return jnp.concatenate(per_row, axis=0)

    def body(it, _):
        firsts = [(it * NA_STEP + g) * NA_GROUP for g in range(NA_STEP)]
        bands = [jnp.clip(r - NA_WIN_R // 2, 0, rows - NA_BAND) for r in firsts]
        q0 = [pl.multiple_of(r * GRID_W, gq) for r in firsts]
        k0 = [pl.multiple_of(b * GRID_W, GRID_W) for b in bands]
        qs = [q_ref[0, pl.ds(a, gq), :] for a in q0]
        ks = [k_ref[0, pl.ds(a, NA_BAND * GRID_W), :] for a in k0]
        vs = [v_ref[0, pl.ds(a, NA_BAND * GRID_W), :] for a in k0]
        bias = [group_bias(r, b) for r, b in zip(firsts, bands)]
        s_loc = [_dot_nt(q, k) for q, k in zip(qs, ks)]
        s_ctx = [_dot_nt(q, kc) for q in qs]
        s_loc = [s * scale + b for s, b in zip(s_loc, bias)]
        s_ctx = [s * scale for s in s_ctx]
        m = [jnp.maximum(jnp.max(a, axis=-1, keepdims=True), jnp.max(c, axis=-1, keepdims=True))
             for a, c in zip(s_loc, s_ctx)]
        p_loc = [jnp.exp(a - mm) for a, mm in zip(s_loc, m)]
        p_ctx = [jnp.exp(c - mm) for c, mm in zip(s_ctx, m)]
        l = [jnp.sum(a, axis=-1, keepdims=True) + jnp.sum(c, axis=-1, keepdims=True) for a, c in zip(p_loc, p_ctx)]
        o_loc = [_dot(p, v) for p, v in zip(p_loc, vs)]
        o_ctx = [_dot(p, vc) for p in p_ctx]
        for a, ol, oc, ll in zip(q0, o_loc, o_ctx, l):
            o_ref[0, pl.ds(a, gq), :] = (ol + oc) / ll
        return 0

    lax.fori_loop(0, rows // (NA_GROUP * NA_STEP), body, 0)


def _natten(qkv, kc, vc, layer, bias):
    bsz, seq_len, _ = qkv.shape
    m_len = kc.shape[2]
    col = lambda off: pl.BlockSpec((1, seq_len, HEAD), lambda b, h: (b, 0, off + h))
    ctx = pl.BlockSpec((1, None, m_len, HEAD), lambda b, h: (b, layer, 0, h))
    return pl.pallas_call(
        _natten_kernel,
        grid=(bsz, NA_HEADS),
        in_specs=[col(0), col(NA_HEADS), col(2 * NA_HEADS), ctx, ctx,
                  pl.BlockSpec((1,) + bias.shape[1:], lambda b, h: (h, 0, 0, 0))],
        out_specs=pl.BlockSpec((1, seq_len, HEAD), lambda b, h: (b, 0, h)),
        out_shape=jax.ShapeDtypeStruct((bsz, seq_len, NA_HEADS * HEAD), F32),
        compiler_params=_cparams("parallel", "parallel"),
        name="natten",
    )(qkv, qkv, qkv, kc, vc, bias)


def _natten_bias(rpb):
    col = np.arange(GRID_W)
    col_start = np.clip(col - NA_WIN_C // 2, 0, GRID_W - NA_WIN_C)
    kc = np.arange(GRID_W)
    valid = (kc[None, :] >= col_start[:, None]) & (kc[None, :] < col_start[:, None] + NA_WIN_C)
    coff = np.clip(kc[None, :] - col[:, None] + NA_WIN_C - 1, 0, 2 * NA_WIN_C - 2)
    t = jnp.where(valid[None, None], rpb.astype(F32)[:, :, coff], NEG)
    t = jnp.pad(t, ((0, 0), (1, 1), (0, 0), (0, 0)), constant_values=NEG)
    return jnp.concatenate([t[:, :-1], t[:, 1:]], axis=-1)


def _gdn_kernel(q_ref, k_ref, v_ref, z_ref, g_ref, cwq_ref, cwk_ref, cwv_ref, par_ref, nrm_ref, s0_ref,
                o_ref, sfin_ref, qs, ks, vs, oacc):
    seq_len = q_ref.shape[1]
    nchunks = seq_len // CHUNK
    h = pl.program_id(1)

    def prep(c, _):
        r0 = pl.multiple_of(c * CHUNK, CHUNK)
        for src, cw, dst, kind in ((q_ref, cwq_ref, qs, "q"), (k_ref, cwk_ref, ks, "k"), (v_ref, cwv_ref, vs, "v")):
            y = _silu(_conv3(src, r0, c, nchunks, seq_len, cw, lead=(0,)))
            if kind != "v":
                y = y * lax.rsqrt(jnp.sum(y * y, axis=-1, keepdims=True) + EPS)
            if kind == "q":
                y = y * HEAD ** -0.5
            dst[pl.ds(r0, CHUNK), :] = y
        return 0

    lax.fori_loop(0, nchunks, prep, 0)

    ri = _iota((CHUNK, CHUNK), 0)
    ci = _iota((CHUNK, CHUNK), 1)
    neg_a = -jnp.exp(par_ref[0:1, :])
    dt_bias = par_ref[1:2, :]

    def load(c):
        r0 = pl.multiple_of(c * CHUNK, CHUNK)
        return (qs[pl.ds(r0, CHUNK), :], ks[pl.ds(r0, CHUNK), :], vs[pl.ds(r0, CHUNK), :],
                g_ref[0, pl.ds(r0, CHUNK), :])

    def gate_terms(loaded, d):
        q, k, v, gates = loaded
        beta = _lane_col(jax.nn.sigmoid(gates), d * GDN_HEADS + h)
        g_t = neg_a * _softplus(gates + dt_bias)
        gc_t = _cumsum_rows(g_t)
        tot_t = jnp.broadcast_to(gc_t[CHUNK - 1:CHUNK, :], gc_t.shape)
        if d == 1:
            gc_t = tot_t - gc_t + g_t
        gidx = 2 * GDN_HEADS + d * GDN_HEADS + h
        gc = _lane_col(gc_t, gidx)
        tot = _lane_col(tot_t, gidx)
        incl = (ri >= ci) if d == 0 else (ri <= ci)
        decay = jnp.exp(jnp.where(incl, gc - _row_form(gc), NEG))
        egc = jnp.exp(gc)
        kb = k * beta
        return dict(q=q, k=k, kb=kb, incl=incl, strict=(ri > ci) if d == 0 else (ri < ci), decay=decay,
                    x=jnp.concatenate([v * beta, kb * egc], axis=-1), q_dec=q * egc,
                    k_dec=k * jnp.exp(tot - gc), etot=jnp.exp(tot))

    unroll = min(GDN_UNROLL, nchunks)

    def step(i, carry):
        sf, sb = carry
        cf = [i * unroll + j for j in range(unroll)]
        cb = [nchunks - 1 - c for c in cf]
        dirs = [0] * unroll + [1] * unroll
        t = [gate_terms(load(c), d) for c, d in zip(cf + cb, dirs)]
        kk = [_dot_nt(ch["kb"], ch["k"]) for ch in t]
        a = [jnp.where(ch["strict"], m * ch["decay"], 0.0) for ch, m in zip(t, kk)]
        x = _dots_many(_unit_tri_inverse_many(a, ri, ci), [ch["x"] for ch in t], TRI_PASSES[2])
        qk = [_dot_nt(ch["q"], ch["k"]) for ch in t]
        qk = [jnp.where(ch["incl"], m * ch["decay"], 0.0) for ch, m in zip(t, qk)]
        outs = []
        for j in range(unroll):
            idx = (j, unroll + j)
            st = (sf, sb)
            ws = [_dot(x[n][:, HEAD:], s) for n, s in zip(idx, st)]
            v_new = [x[n][:, :HEAD] - w for n, w in zip(idx, ws)]
            o1 = [_dot(t[n]["q_dec"], s) for n, s in zip(idx, st)]
            o2 = [_dot(qk[n], vn) for n, vn in zip(idx, v_new)]
            kv = [_dot(jnp.transpose(t[n]["k_dec"]), vn) for n, vn in zip(idx, v_new)]
            sf, sb = [s * t[n]["etot"] + m for n, s, m in zip(idx, st, kv)]
            outs.append([a1 + a2 for a1, a2 in zip(o1, o2)])
        for j in range(unroll):
            oacc[0, pl.ds(pl.multiple_of(cf[j] * CHUNK, CHUNK), CHUNK), :] = outs[j][0]
            oacc[1, pl.ds(pl.multiple_of(cb[j] * CHUNK, CHUNK), CHUNK), :] = outs[j][1]
        return sf, sb

    sf, sb = lax.fori_loop(0, nchunks // unroll, step, (s0_ref[0, 0, 0], s0_ref[0, 1, 0]))
    sfin_ref[0, 0, 0] = sf
    sfin_ref[0, 1, 0] = sb

    def fin(c, _):
        r0 = pl.multiple_of(c * CHUNK, CHUNK)
        o = oacc[0, pl.ds(r0, CHUNK), :] + oacc[1, pl.ds(r0, CHUNK), :]
        o_ref[0, pl.ds(r0, CHUNK), :] = _rms(o, nrm_ref[...]) * _silu(z_ref[0, pl.ds(r0, CHUNK), :])
        return 0

    lax.fori_loop(0, nchunks, fin, 0)


def _gdn(x, gates, conv_w, par, norm_w, s0, layer):
    bsz, seq_len, _ = x.shape
    col = lambda off: pl.BlockSpec((1, seq_len, HEAD), lambda b, h: (b, 0, off + h))
    cw = lambda off: pl.BlockSpec((3, HEAD), lambda b, h: (0, off + h))
    st = pl.BlockSpec((1, 2, 1, HEAD, HEAD), lambda b, h: (b, 0, h, 0, 0))
    st_in = pl.BlockSpec((1, None, 2, 1, HEAD, HEAD), lambda b, h: (b, layer, 0, h, 0, 0))
    return pl.pallas_call(
        _gdn_kernel,
        grid=(bsz, GDN_HEADS),
        in_specs=[col(0), col(GDN_HEADS), col(2 * GDN_HEADS), col(3 * GDN_HEADS),
                  pl.BlockSpec((1, seq_len, HEAD), lambda b, h: (b, 0, 0)),
                  cw(0), cw(GDN_HEADS), cw(2 * GDN_HEADS),
                  pl.BlockSpec((8, HEAD), lambda b, h: (0, 0)),
                  pl.BlockSpec((1, HEAD), lambda b, h: (0, 0)),
                  st_in],
        out_specs=[pl.BlockSpec((1, seq_len, HEAD), lambda b, h: (b, 0, h)), st],
        out_shape=[jax.ShapeDtypeStruct((bsz, seq_len, GDN_HEADS * HEAD), F32),
                   jax.ShapeDtypeStruct((bsz, 2, GDN_HEADS, HEAD, HEAD), F32)],
        scratch_shapes=[pltpu.VMEM((seq_len, HEAD), F32)] * 3 + [pltpu.VMEM((2, seq_len, HEAD), F32)],
        compiler_params=_cparams("parallel", "parallel"),
        name="gdn",
    )(x, x, x, x, gates, conv_w, conv_w, conv_w, par, norm_w.reshape(1, HEAD), s0)


def _ssd_kernel(x_ref, bm_ref, cm_ref, z_ref, dt_ref, cwx_ref, cwb_ref, cwc_ref, cbx_ref, cbb_ref, cbc_ref,
                par_ref, dsk_ref, h0_ref, y_ref, hfin_ref, xs, bs, cs):
    seq_len = x_ref.shape[1]
    nchunks = seq_len // CHUNK
    width = SSD_HPG * SSD_P

    def prep(c, _):
        r0 = pl.multiple_of(c * CHUNK, CHUNK)
        for src, cw, cb, dst in ((x_ref, cwx_ref, cbx_ref, xs), (bm_ref, cwb_ref, cbb_ref, bs),
                                 (cm_ref, cwc_ref, cbc_ref, cs)):
            dst[pl.ds(r0, CHUNK), :] = _silu(_conv3(src, r0, c, nchunks, seq_len, cw, lead=(0,)) + cb[...])
        return 0

    lax.fori_loop(0, nchunks, prep, 0)

    ri = _iota((CHUNK, CHUNK), 0)
    ci = _iota((CHUNK, CHUNK), 1)
    head_of_lane = jnp.right_shift(_iota((CHUNK, width), 1), 6)
    neg_a = -jnp.exp(par_ref[0:1, :])
    dt_bias = par_ref[1:2, :]

    def expand(cols):
        out = jnp.broadcast_to(cols[0], (CHUNK, width))
        for e in range(1, SSD_HPG):
            out = jnp.where(head_of_lane == e, cols[e], out)
        return out

    def gate_terms(c, d):
        r0 = pl.multiple_of(c * CHUNK, CHUNK)
        x, bm, cm = xs[pl.ds(r0, CHUNK), :], bs[pl.ds(r0, CHUNK), :], cs[pl.ds(r0, CHUNK), :]
        dt_t = _softplus(dt_ref[0, pl.ds(r0, CHUNK), :] + dt_bias)
        la_t = dt_t * neg_a
        cs_t = _cumsum_rows(la_t)
        tot_row = cs_t[CHUNK - 1:CHUNK, :]
        tot_t = jnp.broadcast_to(tot_row, cs_t.shape)
        if d == 1:
            cs_t = tot_t - cs_t + la_t
        incl = (ri >= ci) if d == 0 else (ri <= ci)
        cs_rows = jnp.transpose(cs_t)
        rest_t = tot_t - cs_t
        tot_p = jnp.broadcast_to(tot_row, (SSD_P, HEAD))
        lanes = [d * SSD_HPG + e for e in range(SSD_HPG)]
        col = _lane_col
        lmats = [jnp.exp(jnp.where(incl, col(cs_t, k) - cs_rows[k:k + 1, :], NEG)) for k in lanes]
        xdt = x * expand([col(dt_t, k) for k in lanes])
        cs_full = expand([col(cs_t, k) for k in lanes])
        cd = jnp.concatenate([jnp.broadcast_to(jnp.exp(col(tot_p, k)), (SSD_P, HEAD)) for k in lanes], axis=0)
        return dict(r0=r0, x=x, bm=bm, cm=cm, xdt=xdt, lmats=lmats, ecs=jnp.exp(cs_full), cd=cd,
                    xdec_t=jnp.transpose(xdt * jnp.exp(expand([col(rest_t, k) for k in lanes]))))

    unroll = min(SSD_UNROLL, nchunks)

    def direction(d, h0, finish):
        def step(i, hstate):
            cidx = [i * unroll + j for j in range(unroll)]
            if d == 1:
                cidx = [nchunks - 1 - c for c in cidx]
            t = [gate_terms(c, d) for c in cidx]
            cb = [_dot_nt(ch["cm"], ch["bm"]) for ch in t]
            states = [_dot(ch["xdec_t"], ch["bm"]) for ch in t]
            ys = []
            for ch, m in zip(t, cb):
                y = None
                for e in range(SSD_HPG):
                    part = _dot(m * ch["lmats"][e], jnp.where(head_of_lane == e, ch["xdt"], 0.0))
                    y = part if y is None else y + part
                ys.append(y)
            for j, ch in enumerate(t):
                ys[j] = ys[j] + _dot_nt(ch["cm"], hstate) * ch["ecs"]
                hstate = hstate * ch["cd"] + states[j]
            for ch, y in zip(t, ys):
                finish(ch, y)
            return hstate

        return lax.fori_loop(0, nchunks // unroll, step, h0.reshape(width, HEAD))

    def store_fwd(ch, y):
        y_ref[0, pl.ds(ch["r0"], CHUNK), :] = y

    def store_bwd(ch, y):
        tot = y_ref[0, pl.ds(ch["r0"], CHUNK), :] + y + ch["x"] * dsk_ref[...]
        y_ref[0, pl.ds(ch["r0"], CHUNK), :] = tot * _silu(z_ref[0, pl.ds(ch["r0"], CHUNK), :])

    hfin_ref[0, 0] = direction(0, h0_ref[0, 0], store_fwd).reshape(SSD_HPG, SSD_P, HEAD)
    hfin_ref[0, 1] = direction(1, h0_ref[0, 1], store_bwd).reshape(SSD_HPG, SSD_P, HEAD)


def _ssd(x, gates, conv_w, conv_b, par, d_skip, h0, layer):
    bsz, seq_len, _ = x.shape
    width = SSD_HPG * SSD_P
    spec = lambda wd, off: pl.BlockSpec((1, seq_len, wd), lambda b, g: (b, 0, off + g))
    cws = lambda rows, wd, off: pl.BlockSpec((rows, wd), lambda b, g: (0, off + g))
    st = pl.BlockSpec((1, 2, SSD_HPG, SSD_P, HEAD), lambda b, g: (b, 0, g, 0, 0))
    st_in = pl.BlockSpec((1, None, 2, SSD_HPG, SSD_P, HEAD), lambda b, g: (b, layer, 0, g, 0, 0))
    return pl.pallas_call(
        _ssd_kernel,
        grid=(bsz, SSD_GROUPS),
        in_specs=[spec(width, 0), spec(HEAD, 4), spec(HEAD, 6), spec(width, 4),
                  pl.BlockSpec((1, seq_len, HEAD), lambda b, g: (b, 0, g)),
                  cws(3, width, 0), cws(3, HEAD, 4), cws(3, HEAD, 6),
                  cws(1, width, 0), cws(1, HEAD, 4), cws(1, HEAD, 6),
                  pl.BlockSpec((8, HEAD), lambda b, g: (0, g)),
                  cws(1, width, 0),
                  st_in],
        out_specs=[pl.BlockSpec((1, seq_len, width), lambda b, g: (b, 0, g)), st],
        out_shape=[jax.ShapeDtypeStruct((bsz, seq_len, SSD_HEADS * SSD_P), F32),
                   jax.ShapeDtypeStruct((bsz, 2, SSD_HEADS, SSD_P, HEAD), F32)],
        scratch_shapes=[pltpu.VMEM((seq_len, width), F32), pltpu.VMEM((seq_len, HEAD), F32),
                        pltpu.VMEM((seq_len, HEAD), F32)],
        compiler_params=_cparams("parallel", "parallel"),
        name="ssd",
    )(x, x, x, x, gates, conv_w, conv_w, conv_w, conv_b, conv_b, conv_b, par, d_skip, h0)


def _rope_tables(n_tokens):
    half = HEAD // 2
    inv_freq = ROPE_THETA ** (-jnp.arange(0, half, 2, dtype=F32) / half)
    t = jnp.arange(n_tokens)
    ang_r = (t // GRID_W).astype(F32)[:, None] * inv_freq
    ang_c = (t % GRID_W).astype(F32)[:, None] * inv_freq
    cos = jnp.concatenate([jnp.cos(ang_r)] * 2 + [jnp.cos(ang_c)] * 2, axis=-1)
    sin = jnp.concatenate([-jnp.sin(ang_r), jnp.sin(ang_r), -jnp.sin(ang_c), jnp.sin(ang_c)], axis=-1)
    return cos, sin


def _gate_rows(a_log, dt_bias, offset):
    n = a_log.size
    rows = jnp.zeros((8, HEAD), F32)
    rows = rows.at[0, offset:offset + n].set(a_log.reshape(-1).astype(F32))
    return rows.at[1, offset:offset + n].set(dt_bias.reshape(-1).astype(F32))


def _ssd_group_lanes(t):
    lead = t.shape[:-1]
    t = t.reshape(lead + (2, SSD_GROUPS, SSD_HPG))
    t = jnp.swapaxes(t, -3, -2).reshape(lead + (SSD_GROUPS, 2 * SSD_HPG))
    pad = [(0, 0)] * (t.ndim - 1) + [(0, HEAD - 2 * SSD_HPG)]
    return jnp.pad(t, pad).reshape(lead + (SSD_GROUPS * HEAD,))


def _pad_cols(w, width):
    return jnp.pad(w, ((0, 0), (0, width - w.shape[1])))


def kernel(x_prompt, x_sample, state_gdn, cache_gqa_k, cache_gqa_v, state_ssd, cache_na_k, cache_na_v, c, c_ctx, ada_w, ada_b, norm_mix_pre, norm_mix_post, norm_mlp_pre, norm_mlp_post, mlp_w1, mlp_w2, ev_w_in, ev_w_out, gdn_conv, gdn_a_log, gdn_dt_bias, gdn_norm, gqa_q_norm, gqa_k_norm, od_w_in, od_w_out, ssd_conv, ssd_conv_b, ssd_a_log, ssd_dt_bias, ssd_d, ssd_norm, na_rpb):
    nb, ns, _ = x_prompt.shape
    db, dl, _ = x_sample.shape
    past = cache_gqa_k.shape[2]

    c16 = jnp.zeros((16, D_MODEL), F32).at[:db].set(c).at[db].set(c_ctx)
    mods = _ada(c16, ada_w, ada_b).reshape(DEPTH, 16, 6, D_MODEL)

    cos_l, sin_l = _rope_tables(dl)
    cos_c, sin_c = jnp.ones((ns, HEAD), F32), jnp.zeros((ns, HEAD), F32)

    xp = x_prompt.reshape(1, nb * ns, D_MODEL)
    xs = x_sample
    tm_c = min(1024, nb * ns)
    tm_in = min(512, nb * ns, dl)
    zero_gdn = jnp.zeros((nb, 1, 2, GDN_HEADS, HEAD, HEAD), F32)
    zero_ssd = jnp.zeros((nb, 1, 2, SSD_HEADS, SSD_P, HEAD), F32)
    new_gdn, new_gk, new_gv, new_ssd, new_nk, new_nv = [], [], [], [], [], []

    for i in range(DEPTH):
        j = i // 2
        mod_l, mod_c = mods[i, :db], mods[i, db:db + 1]
        if i % 2 == 0:
            wi = ev_w_in[j]
            w_in = jnp.concatenate([wi[:, :2048], wi[:, 2064:], _pad_cols(wi[:, 2048:2064], HEAD)], axis=1).astype(BF16)
            widths = (2048, 1024, HEAD)
            w_out = ev_w_out[j].astype(BF16)
            par = _gate_rows(gdn_a_log[j], gdn_dt_bias[j], 2 * GDN_HEADS)

            gx, ax, gt = _inproj(xp, mod_c, norm_mix_pre[i], w_in, widths, tm_in)
            gx, ax, gt = (t.reshape(nb, ns, -1) for t in (gx, ax, gt))
            o_a, s_a = _gdn(gx, gt, gdn_conv[j], par, gdn_norm[j], zero_gdn, 0)
            q, k, v, kf = _qkprep(ax, cos_c, sin_c, gqa_q_norm[j], gqa_k_norm[j], ns, True)
            o_b = _flash(q, k, v, GQA_KV_HEADS, 2, ns, ns, True)
            yc = _outproj(o_a.reshape(1, nb * ns, -1), o_b.reshape(1, nb * ns, -1), xp, mod_c,
                          w_out, norm_mix_post[i], tm_c)
            new_gdn.append(s_a)
            new_gk.append(kf.reshape(nb, ns, GQA_KV_HEADS, HEAD))
            new_gv.append(ax[..., 768:].reshape(nb, ns, GQA_KV_HEADS, HEAD))

            gx, ax, gt = _inproj(xs, mod_l, norm_mix_pre[i], w_in, widths, tm_in)
            o_a, _ = _gdn(gx, gt, gdn_conv[j], par, gdn_norm[j], state_gdn, j)
            q, k, v = _qkprep(ax, cos_l, sin_l, gqa_q_norm[j], gqa_k_norm[j], 512, False)
            o_b = _flash(q, k, v, GQA_KV_HEADS, 2, 256, min(2048, dl), True,
                         cache=(cache_gqa_k.reshape(db, -1, past, GQA_KV_HEADS * HEAD),
                                cache_gqa_v.reshape(db, -1, past, GQA_KV_HEADS * HEAD), j))
            yl = _outproj(o_a, o_b, xs, mod_l, w_out, norm_mix_post[i], 512)
        else:
            wi = od_w_in[j]
            w_in = jnp.concatenate([wi[:, 512:1536], wi[:, :512], wi[:, 1552:], _ssd_group_lanes(wi[:, 1536:1552])], axis=1).astype(BF16)
            widths = (1536, 1536, SSD_GROUPS * HEAD)
            w_out = od_w_out[j].astype(BF16)
            par = jnp.zeros((8, SSD_GROUPS * HEAD), F32)
            par = par.at[0].set(_ssd_group_lanes(ssd_a_log[j].reshape(-1).astype(F32)))
            par = par.at[1].set(_ssd_group_lanes(ssd_dt_bias[j].reshape(-1).astype(F32)))
            dsk = jnp.repeat(ssd_d[j].astype(F32), SSD_P).reshape(1, -1)
            cvw, cvb = ssd_conv[j], ssd_conv_b[j].reshape(1, -1)

            sx, ax, gt = _inproj(xp, mod_c, norm_mix_pre[i], w_in, widths, tm_in)
            sx, ax, gt = (t.reshape(nb, ns, -1) for t in (sx, ax, gt))
            y_c, s_c = _ssd(sx, gt, cvw, cvb, par, dsk, zero_ssd, 0)
            o_d = _flash(ax, ax, ax, NA_HEADS, 1, ns, ns, False, (0, NA_HEADS, 2 * NA_HEADS))
            yc = _outproj(y_c.reshape(1, nb * ns, -1), o_d.reshape(1, nb * ns, -1), xp, mod_c,
                          w_out, norm_mix_post[i], tm_c, ssd_norm[j])
            new_ssd.append(s_c)
            new_nk.append(ax[..., 512:1024].reshape(nb, ns, NA_HEADS, HEAD))
            new_nv.append(ax[..., 1024:].reshape(nb, ns, NA_HEADS, HEAD))

            sx, ax, gt = _inproj(xs, mod_l, norm_mix_pre[i], w_in, widths, tm_in)
            y_c, _ = _ssd(sx, gt, cvw, cvb, par, dsk, state_ssd, j)
            o_d = _natten(ax, cache_na_k.reshape(db, -1, past, NA_HEADS * HEAD),
                          cache_na_v.reshape(db, -1, past, NA_HEADS * HEAD), j, _natten_bias(na_rpb[j]))
            yl = _outproj(y_c, o_d, xs, mod_l, w_out, norm_mix_post[i], 512, ssd_norm[j])
        xp, xs = yc, yl
        w1, w2 = mlp_w1[i].astype(BF16), mlp_w2[i].astype(BF16)
        xp = _mlp(xp, mod_c, norm_mlp_pre[i], w1, w2, norm_mlp_post[i], tm_c, 1024)
        xs = _mlp(xs, mod_l, norm_mlp_pre[i], w1, w2, norm_mlp_post[i], 1024, 1024)

    return (xp.reshape(nb, ns, D_MODEL), xs, jnp.stack(new_gdn, axis=1), jnp.stack(new_gk, axis=1),
            jnp.stack(new_gv, axis=1), jnp.stack(new_ssd, axis=1), jnp.stack(new_nk, axis=1),
            jnp.stack(new_nv, axis=1))
```

```python
import functools

import numpy as np
import jax
import jax.numpy as jnp
from jax import lax
from jax.experimental import pallas as pl
from jax.experimental.pallas import tpu as pltpu

F32 = jnp.float32
BF16 = jnp.bfloat16

D_MODEL = 1024
DEPTH = 4
GRID_W = 64
EPS = 1e-6
ROPE_THETA = 10000.0
D_FF = 4 * D_MODEL
HEAD = 128
GDN_HEADS = 4
GQA_HEADS = 4
GQA_KV_HEADS = 2
SSD_HEADS = 8
SSD_P = 64
SSD_GROUPS = 2
SSD_HPG = SSD_HEADS // SSD_GROUPS
NA_HEADS = 4
NA_WIN_R = 8
NA_WIN_C = 16
CHUNK = 128
GDN_UNROLL = 4
SSD_UNROLL = 4
TRI_PASSES = (1, 1, 2)
NEG = -1e30
QK_SCALE_LOG2 = HEAD ** -0.5 * float(np.log2(np.e))
FLASH_SAFE_BOUND = 60.0
FLASH_SHIFT = 64.0
VMEM_LIMIT_BYTES = 56 * 1024 * 1024


def _cparams(*sem):
    return pltpu.CompilerParams(dimension_semantics=sem, vmem_limit_bytes=VMEM_LIMIT_BYTES)


def _dot(a, b):
    return jnp.dot(a.astype(BF16), b.astype(BF16), preferred_element_type=F32)


def _dot_nt(a, b):
    return lax.dot_general(a.astype(BF16), b.astype(BF16), (((1,), (1,)), ((), ())),
                           preferred_element_type=F32)


def _split_bf16(a):
    hi = a.astype(BF16)
    lo = (a - hi.astype(F32)).astype(BF16)
    return hi, lo


def _dot3(a, b):
    ah, al = _split_bf16(a)
    bh, bl = _split_bf16(b)
    d = lambda x, y: jnp.dot(x, y, preferred_element_type=F32)
    return d(ah, bh) + (d(ah, bl) + d(al, bh))


def _silu(x):
    return x * jax.nn.sigmoid(x)


def _softplus(x):
    return jnp.maximum(x, 0.0) + jnp.log(1.0 + jnp.exp(-jnp.abs(x)))


def _rms(x, w):
    return x * lax.rsqrt(jnp.mean(x * x, axis=-1, keepdims=True) + EPS) * w


def _iota(shape, axis):
    return lax.broadcasted_iota(jnp.int32, shape, axis)


def _lane_col(tile, idx):
    return jnp.sum(jnp.where(_iota(tile.shape, 1) == idx, tile, 0.0), axis=-1, keepdims=True)


def _cumsum_rows(t):
    rows = _iota(t.shape, 0)
    s = 1
    while s < t.shape[0]:
        t = t + jnp.where(rows >= s, pltpu.roll(t, s, 0), 0.0)
        s *= 2
    return t


def _row_form(col):
    n = col.shape[0]
    return jnp.transpose(jnp.broadcast_to(col, (n, n)))


def _dot3_many(xs, ys):
    xs = [_split_bf16(x) for x in xs]
    ys = [_split_bf16(y) for y in ys]
    d = lambda x, y: jnp.dot(x, y, preferred_element_type=F32)
    hh = [d(x[0], y[0]) for x, y in zip(xs, ys)]
    hl = [d(x[0], y[1]) for x, y in zip(xs, ys)]
    lh = [d(x[1], y[0]) for x, y in zip(xs, ys)]
    return [a + (b + c) for a, b, c in zip(hh, hl, lh)]


def _dots_many(xs, ys, passes):
    if passes == 3:
        return _dot3_many(xs, ys)
    if passes == 2:
        d = lambda x, y: jnp.dot(x, y, preferred_element_type=F32)
        xs = [x.astype(BF16) for x in xs]
        ys = [_split_bf16(y) for y in ys]
        hi = [d(x, y[0]) for x, y in zip(xs, ys)]
        lo = [d(x, y[1]) for x, y in zip(xs, ys)]
        return [a + b for a, b in zip(hi, lo)]
    return [_dot(x, y) for x, y in zip(xs, ys)]


def _unit_tri_inverse_stages(a_list, ri, ci):
    same = lambda sh: jnp.right_shift(ri, sh) == jnp.right_shift(ci, sh)
    eye = jnp.where(ri == ci, 1.0, 0.0)
    p = [jnp.where(same(4), a, 0.0) for a in a_list]
    m = [eye - x for x in p]
    p = _dots_many(p, p, TRI_PASSES[0])
    yield
    for last in (False, False, True):
        mp = _dots_many(m, p, TRI_PASSES[0])
        if not last:
            p = _dots_many(p, p, TRI_PASSES[0])
        yield
        m = [x + y for x, y in zip(m, mp)]
    for sh in (4, 5, 6):
        off = [jnp.where(same(sh + 1), jnp.where(same(sh), 0.0, a), 0.0) for a in a_list]
        t = _dots_many(off, m, TRI_PASSES[1])
        yield
        mt = _dots_many(m, t, TRI_PASSES[1])
        yield
        m = [x - y for x, y in zip(m, mt)]
    return m


def _run_interleaved(gen_a, gen_b):
    results, live = [None, None], [gen_a, gen_b]
    while any(g is not None for g in live):
        for n, g in enumerate(live):
            if g is None:
                continue
            try:
                next(g)
            except StopIteration as stop:
                results[n], live[n] = stop.value, None
    return results


def _conv3(ref, r0, c, nchunks, seq_len, w, lead=()):
    x = ref[lead + (pl.ds(r0, CHUNK), slice(None))]
    prev = ref[lead + (pl.ds(jnp.maximum(r0 - 1, 0), 1), slice(None))]
    nxt = ref[lead + (pl.ds(jnp.minimum(r0 + CHUNK, seq_len - 1), 1), slice(None))]
    prev = jnp.where(c == 0, 0.0, prev)
    nxt = jnp.where(c == nchunks - 1, 0.0, nxt)
    rows = _iota(x.shape, 0)
    xm = jnp.where(rows == 0, prev, pltpu.roll(x, 1, 0))
    xp = jnp.where(rows == CHUNK - 1, nxt, pltpu.roll(x, CHUNK - 1, 0))
    return xm * w[0:1, :] + x * w[1:2, :] + xp * w[2:3, :]


def _ada_kernel(c_ref, w_ref, b_ref, o_ref):
    o_ref[0] = _dot(_silu(c_ref[...]), w_ref[0]) + b_ref[0]


def _ada(c16, ada_w, ada_b):
    tn = 1536
    n = ada_w.shape[-1]
    return pl.pallas_call(
        _ada_kernel,
        grid=(DEPTH, n // tn),
        in_specs=[pl.BlockSpec((16, D_MODEL), lambda i, j: (0, 0)),
                  pl.BlockSpec((1, D_MODEL, tn), lambda i, j: (i, 0, j)),
                  pl.BlockSpec((1, 1, tn), lambda i, j: (i, 0, j))],
        out_specs=pl.BlockSpec((1, 16, tn), lambda i, j: (i, 0, j)),
        out_shape=jax.ShapeDtypeStruct((DEPTH, 16, n), F32),
        compiler_params=_cparams("parallel", "parallel"),
        name="ada",
    )(c16, ada_w, ada_b.reshape(DEPTH, 1, n))


def _mod_spec(mod):
    if mod.shape[0] == 1:
        return pl.BlockSpec((1, 6, D_MODEL), lambda b, t: (0, 0, 0))
    return pl.BlockSpec((1, 6, D_MODEL), lambda b, t: (b, 0, 0))


def _inproj_kernel(x_ref, mod_ref, nw_ref, w_ref, *o_refs, splits):
    h = _rms(x_ref[0], nw_ref[...])
    h = h * (1.0 + mod_ref[0, 1:2, :]) + mod_ref[0, 0:1, :]
    hb = h.astype(BF16)
    for o_ref, (a, b) in zip(o_refs, splits):
        o_ref[0] = jnp.dot(hb, w_ref[:, a:b], preferred_element_type=F32)


def _inproj(x, mod, norm_w, w, widths, tm):
    bsz, seq_len, _ = x.shape
    splits, a = [], 0
    for wd in widths:
        splits.append((a, a + wd))
        a += wd
    return pl.pallas_call(
        functools.partial(_inproj_kernel, splits=tuple(splits)),
        grid=(bsz, seq_len // tm),
        in_specs=[pl.BlockSpec((1, tm, D_MODEL), lambda b, t: (b, t, 0)),
                  _mod_spec(mod),
                  pl.BlockSpec((1, D_MODEL), lambda b, t: (0, 0)),
                  pl.BlockSpec(w.shape, lambda b, t: (0, 0))],
        out_specs=[pl.BlockSpec((1, tm, wd), lambda b, t: (b, t, 0)) for wd in widths],
        out_shape=[jax.ShapeDtypeStruct((bsz, seq_len, wd), F32) for wd in widths],
        compiler_params=_cparams("parallel", "parallel"),
        name="inproj",
    )(x, mod, norm_w.reshape(1, D_MODEL), w)


def _outproj_kernel(a_ref, b_ref, x_ref, mod_ref, w_ref, nw_ref, *rest, norm_a):
    a = a_ref[0]
    if norm_a:
        na_ref, o_ref = rest
        a = _rms(a, na_ref[...])
    else:
        (o_ref,) = rest
    half = a.shape[-1]
    y = (jnp.dot(a.astype(BF16), w_ref[:half, :], preferred_element_type=F32)
         + jnp.dot(b_ref[0].astype(BF16), w_ref[half:, :], preferred_element_type=F32))
    o_ref[0] = x_ref[0] + mod_ref[0, 2:3, :] * _rms(y, nw_ref[...])


def _outproj(a, b, x, mod, w, norm_w, tm, norm_a_w=None):
    bsz, seq_len, _ = x.shape
    half = a.shape[-1]
    in_specs = [pl.BlockSpec((1, tm, half), lambda bb, t: (bb, t, 0)),
                pl.BlockSpec((1, tm, half), lambda bb, t: (bb, t, 0)),
                pl.BlockSpec((1, tm, D_MODEL), lambda bb, t: (bb, t, 0)),
                _mod_spec(mod),
                pl.BlockSpec(w.shape, lambda bb, t: (0, 0)),
                pl.BlockSpec((1, D_MODEL), lambda bb, t: (0, 0))]
    args = [a, b, x, mod, w, norm_w.reshape(1, D_MODEL)]
    if norm_a_w is not None:
        in_specs.append(pl.BlockSpec((1, half), lambda bb, t: (0, 0)))
        args.append(norm_a_w.reshape(1, half))
    return pl.pallas_call(
        functools.partial(_outproj_kernel, norm_a=norm_a_w is not None),
        grid=(bsz, seq_len // tm),
        in_specs=in_specs,
        out_specs=pl.BlockSpec((1, tm, D_MODEL), lambda bb, t: (bb, t, 0)),
        out_shape=jax.ShapeDtypeStruct(x.shape, F32),
        compiler_params=_cparams("parallel", "parallel"),
        name="outproj",
    )(*args)


def _mlp_kernel(x_ref, mod_ref, npre_ref, w1_ref, w2_ref, npost_ref, o_ref, h_ref, acc_ref):
    f = pl.program_id(2)

    @pl.when(f == 0)
    def _():
        h = _rms(x_ref[0], npre_ref[...])
        h = h * (1.0 + mod_ref[0, 4:5, :]) + mod_ref[0, 3:4, :]
        h_ref[...] = h.astype(BF16)
        acc_ref[...] = jnp.zeros_like(acc_ref)

    u = jnp.maximum(jnp.dot(h_ref[...], w1_ref[...], preferred_element_type=F32), 0.0)
    acc_ref[...] += jnp.dot((u * u).astype(BF16), w2_ref[...], preferred_element_type=F32)

    @pl.when(f == pl.num_programs(2) - 1)
    def _():
        o_ref[0] = x_ref[0] + mod_ref[0, 5:6, :] * _rms(acc_ref[...], npost_ref[...])


def _mlp(x, mod, npre, w1, w2, npost, tm, tf):
    bsz, seq_len, _ = x.shape
    mspec = _mod_spec(mod)
    return pl.pallas_call(
        _mlp_kernel,
        grid=(bsz, seq_len // tm, D_FF // tf),
        in_specs=[pl.BlockSpec((1, tm, D_MODEL), lambda b, t, f: (b, t, 0)),
                  pl.BlockSpec((1, 6, D_MODEL), lambda b, t, f: mspec.index_map(b, t)),
                  pl.BlockSpec((1, D_MODEL), lambda b, t, f: (0, 0)),
                  pl.BlockSpec((D_MODEL, tf), lambda b, t, f: (0, f)),
                  pl.BlockSpec((tf, D_MODEL), lambda b, t, f: (f, 0)),
                  pl.BlockSpec((1, D_MODEL), lambda b, t, f: (0, 0))],
        out_specs=pl.BlockSpec((1, tm, D_MODEL), lambda b, t, f: (b, t, 0)),
        out_shape=jax.ShapeDtypeStruct(x.shape, F32),
        scratch_shapes=[pltpu.VMEM((tm, D_MODEL), BF16), pltpu.VMEM((tm, D_MODEL), F32)],
        compiler_params=_cparams("parallel", "parallel", "arbitrary"),
        name="mlp",
    )(x, mod, npre.reshape(1, D_MODEL), w1, w2, npost.reshape(1, D_MODEL))


def _qkprep_kernel(x_ref, cos_ref, sin_ref, qn_ref, kn_ref, q_ref, k_ref, v_ref, kf_ref=None):
    cos, sin = cos_ref[...], sin_ref[...]
    lane = _iota(cos.shape, 1)
    first = (lane & 32) == 0

    def prep(xh, w):
        y = _rms(xh, w)
        swapped = jnp.where(first, pltpu.roll(y, HEAD - 32, 1), pltpu.roll(y, 32, 1))
        return y, y * cos + swapped * sin

    nq = GQA_HEADS * HEAD
    for hd in range(GQA_HEADS):
        _, r = prep(x_ref[0, :, hd * HEAD:(hd + 1) * HEAD], qn_ref[...])
        q_ref[0, :, hd * HEAD:(hd + 1) * HEAD] = (r * QK_SCALE_LOG2).astype(BF16)
    for hd in range(GQA_KV_HEADS):
        y, r = prep(x_ref[0, :, nq + hd * HEAD:nq + (hd + 1) * HEAD], kn_ref[...])
        k_ref[0, :, hd * HEAD:(hd + 1) * HEAD] = r.astype(BF16)
        if kf_ref is not None:
            kf_ref[0, :, hd * HEAD:(hd + 1) * HEAD] = y
    nk = GQA_KV_HEADS * HEAD
    v_ref[0] = x_ref[0, :, nq + nk:nq + 2 * nk].astype(BF16)


def _qkprep(x, cos, sin, q_norm, k_norm, tm, want_plain_k):
    bsz, seq_len, width = x.shape
    nq, nk = GQA_HEADS * HEAD, GQA_KV_HEADS * HEAD
    row = lambda wd: pl.BlockSpec((1, tm, wd), lambda b, t: (b, t, 0))
    n_out = 4 if want_plain_k else 3
    return pl.pallas_call(
        _qkprep_kernel,
        grid=(bsz, seq_len // tm),
        in_specs=[row(width),
                  pl.BlockSpec((tm, HEAD), lambda b, t: (t, 0)),
                  pl.BlockSpec((tm, HEAD), lambda b, t: (t, 0)),
                  pl.BlockSpec((1, HEAD), lambda b, t: (0, 0)),
                  pl.BlockSpec((1, HEAD), lambda b, t: (0, 0))],
        out_specs=[row(nq), row(nk), row(nk), row(nk)][:n_out],
        out_shape=[jax.ShapeDtypeStruct((bsz, seq_len, nq), BF16),
                   jax.ShapeDtypeStruct((bsz, seq_len, nk), BF16),
                   jax.ShapeDtypeStruct((bsz, seq_len, nk), BF16),
                   jax.ShapeDtypeStruct((bsz, seq_len, nk), F32)][:n_out],
        compiler_params=_cparams("parallel", "parallel"),
        name="qkprep",
    )(x, cos, sin, q_norm.reshape(1, HEAD), k_norm.reshape(1, HEAD))


def _flash_kernel(q_ref, k_ref, v_ref, *rest, rep, tk, prescaled, with_cache):
    if with_cache:
        kc_ref, vc_ref, o_ref, knorm_ref = rest
    else:
        o_ref, knorm_ref = rest
    tq = q_ref.shape[1]
    if rep == 2:
        qs = [q_ref[0, :, r * HEAD:(r + 1) * HEAD] for r in range(rep)]
    else:
        qs = [q_ref[0, :tq // 2, :], q_ref[0, tq // 2:, :]]
    if not prescaled:
        qs = [(q.astype(F32) * QK_SCALE_LOG2).astype(BF16) for q in qs]
    nk = k_ref.shape[1] // tk
    rows = qs[0].shape[0]

    def lane_chunks(s):
        return [s[:, j * HEAD:(j + 1) * HEAD] for j in range(s.shape[1] // HEAD)]

    def row_max(k, part):
        s = [_dot_nt(q, k) for q in qs]
        return tuple(functools.reduce(jnp.maximum, lane_chunks(sc), m) for sc, m in zip(s, part))

    def accumulate(k, v, mx, carry):
        s = [_dot_nt(q, k) for q in qs]
        ps, ls = [], []
        for sc, m, (l, _) in zip(s, mx, carry):
            pc = [jnp.exp2(c - m) for c in lane_chunks(sc)]
            ls.append(functools.reduce(jnp.add, pc, l))
            ps.append(jnp.concatenate([c.astype(BF16) for c in pc], axis=-1))
        pv = [jnp.dot(p, v.astype(BF16), preferred_element_type=F32) for p in ps]
        return tuple((l, acc + x) for l, (_, acc), x in zip(ls, carry, pv))

    def block(ref, i):
        return ref[0, pl.ds(pl.multiple_of(i * tk, tk), tk), :]

    def max_sq_norm(k, best):
        kf = k.astype(BF16).astype(F32)
        sq = jnp.broadcast_to(jnp.sum(kf * kf, axis=-1, keepdims=True), kf.shape)
        return jnp.maximum(best, jnp.max(sq, axis=0, keepdims=True))

    @pl.when(pl.program_id(2) == 0)
    def _():
        best = lax.fori_loop(0, nk, lambda i, b: max_sq_norm(block(k_ref, i), b), jnp.zeros((1, HEAD), F32))
        if with_cache:
            best = max_sq_norm(kc_ref[0], best)
        knorm_ref[...] = jnp.sqrt(best)

    def exact_row_max():
        part = tuple(jnp.full((rows, HEAD), NEG, F32) for _ in qs)
        part = lax.fori_loop(0, nk, lambda i, p: row_max(block(k_ref, i), p), part)
        if with_cache:
            part = row_max(kc_ref[0], part)
        return [jnp.broadcast_to(jnp.max(m, axis=-1, keepdims=True), (rows, HEAD)) for m in part]

    bounds = []
    for q in qs:
        qf = q.astype(F32)
        qn = jnp.sqrt(jnp.sum(qf * qf, axis=-1, keepdims=True))
        bounds.append(jnp.broadcast_to(qn, (rows, HEAD)) * knorm_ref[...])
    safe = jnp.max(functools.reduce(jnp.maximum, bounds)) <= FLASH_SAFE_BOUND
    mx = lax.cond(safe, lambda: [b - FLASH_SHIFT for b in bounds], exact_row_max)

    stats = tuple((jnp.zeros((rows, HEAD), F32), jnp.zeros((rows, HEAD), F32)) for _ in qs)
    stats = lax.fori_loop(0, nk, lambda i, c: accumulate(block(k_ref, i), block(v_ref, i), mx, c), stats)
    if with_cache:
        stats = accumulate(kc_ref[0], vc_ref[0], mx, stats)
    outs = [acc / jnp.sum(l, axis=-1, keepdims=True) for l, acc in stats]
    if rep == 2:
        for r in range(rep):
            o_ref[0, :, r * HEAD:(r + 1) * HEAD] = outs[r]
    else:
        o_ref[0, :tq // 2, :] = outs[0]
        o_ref[0, tq // 2:, :] = outs[1]


def _flash(q, k, v, kv_heads, rep, tq, tk, prescaled, head_offsets=(0, 0, 0), cache=None):
    bsz, seq_len, _ = q.shape
    assert rep in (1, 2) and head_offsets[0] % rep == 0
    m_len = k.shape[1]
    qo, ko, vo = head_offsets[0] // rep, head_offsets[1], head_offsets[2]
    in_specs = [pl.BlockSpec((1, tq, rep * HEAD), lambda b, g, t: (b, t, qo + g)),
                pl.BlockSpec((1, m_len, HEAD), lambda b, g, t: (b, 0, ko + g)),
                pl.BlockSpec((1, m_len, HEAD), lambda b, g, t: (b, 0, vo + g))]
    args = [q, k, v]
    if cache is not None:
        layer = cache[2]
        spec = pl.BlockSpec((1, None, cache[0].shape[2], HEAD), lambda b, g, t: (b, layer, 0, g))
        in_specs += [spec, spec]
        args += [cache[0], cache[1]]
    return pl.pallas_call(
        functools.partial(_flash_kernel, rep=rep, tk=tk, prescaled=prescaled, with_cache=cache is not None),
        grid=(bsz, kv_heads, seq_len // tq),
        in_specs=in_specs,
        out_specs=pl.BlockSpec((1, tq, rep * HEAD), lambda b, g, t: (b, t, g)),
        out_shape=jax.ShapeDtypeStruct((bsz, seq_len, kv_heads * rep * HEAD), F32),
        scratch_shapes=[pltpu.VMEM((1, HEAD), F32)],
        compiler_params=_cparams("parallel", "parallel", "arbitrary"),
        name="flash",
    )(*args)


NA_GROUP = 4
NA_BAND = NA_GROUP + NA_WIN_R
NA_STEP = 2


def _natten_kernel(q_ref, k_ref, v_ref, kc_ref, vc_ref, bias_ref, o_ref):
    rows = q_ref.shape[1] // GRID_W
    scale = HEAD ** -0.5
    gq = NA_GROUP * GRID_W
    kc = kc_ref[0].astype(BF16)
    vc = vc_ref[0].astype(BF16)
    low_half = _iota((GRID_W, HEAD), 1) < GRID_W

    def group_bias(r_first, band_first):
        per_row = []
        for i in range(NA_GROUP):
            r = r_first + i
            r0 = jnp.clip(r - NA_WIN_R // 2, 0, rows - NA_WIN_R)
            blocks = []
            for jj in range(NA_BAND // 2):
                kr = band_first + 2 * jj
                blk = bias_ref[0, jnp.clip(kr - r + NA_WIN_R, 0, 2 * NA_WIN_R - 1)]
                ok_a = jnp.logical_and(kr >= r0, kr < r0 + NA_WIN_R)
                ok_b = jnp.logical_and(kr + 1 >= r0, kr + 1 < r0 + NA_WIN_R)
                blocks.append(jnp.where(low_half, jnp.where(ok_a, blk, NEG), jnp.where(ok_b, blk, NEG)))
            per_row.append(jnp.concatenate(blocks, axis=-1))
        return jnp.concatenate(per_row, axis=0)

    def body(it, _):
        firsts = [(it * NA_STEP + g) * NA_GROUP for g in range(NA_STEP)]
        bands = [jnp.clip(r - NA_WIN_R // 2, 0, rows - NA_BAND) for r in firsts]
        q0 = [pl.multiple_of(r * GRID_W, gq) for r in firsts]
        k0 = [pl.multiple_of(b * GRID_W, GRID_W) for b in bands]
        qs = [q_ref[0, pl.ds(a, gq), :] for a in q0]
        ks = [k_ref[0, pl.ds(a, NA_BAND * GRID_W), :] for a in k0]
        vs = [v_ref[0, pl.ds(a, NA_BAND * GRID_W), :] for a in k0]
        bias = [group_bias(r, b) for r, b in zip(firsts, bands)]
        s_loc = [_dot_nt(q, k) for q, k in zip(qs, ks)]
        s_ctx = [_dot_nt(q, kc) for q in qs]
        s_loc = [s * scale + b for s, b in zip(s_loc, bias)]
        s_ctx = [s * scale for s in s_ctx]
        m = [jnp.maximum(jnp.max(a, axis=-1, keepdims=True), jnp.max(c, axis=-1, keepdims=True))
             for a, c in zip(s_loc, s_ctx)]
        p_loc = [jnp.exp(a - mm) for a, mm in zip(s_loc, m)]
        p_ctx = [jnp.exp(c - mm) for c, mm in zip(s_ctx, m)]
        l = [jnp.sum(a, axis=-1, keepdims=True) + jnp.sum(c, axis=-1, keepdims=True) for a, c in zip(p_loc, p_ctx)]
        o_loc = [_dot(p, v) for p, v in zip(p_loc, vs)]
        o_ctx = [_dot(p, vc) for p in p_ctx]
        for a, ol, oc, ll in zip(q0, o_loc, o_ctx, l):
            o_ref[0, pl.ds(a, gq), :] = (ol + oc) / ll
        return 0

    lax.fori_loop(0, rows // (NA_GROUP * NA_STEP), body, 0)


def _natten(qkv, kc, vc, layer, bias):
    bsz, seq_len, _ = qkv.shape
    m_len = kc.shape[2]
    col = lambda off: pl.BlockSpec((1, seq_len, HEAD), lambda b, h: (b, 0, off + h))
    ctx = pl.BlockSpec((1, None, m_len, HEAD), lambda b, h: (b, layer, 0, h))
    return pl.pallas_call(
        _natten_kernel,
        grid=(bsz, NA_HEADS),
        in_specs=[col(0), col(NA_HEADS), col(2 * NA_HEADS), ctx, ctx,
                  pl.BlockSpec((1,) + bias.shape[1:], lambda b, h: (h, 0, 0, 0))],
        out_specs=pl.BlockSpec((1, seq_len, HEAD), lambda b, h: (b, 0, h)),
        out_shape=jax.ShapeDtypeStruct((bsz, seq_len, NA_HEADS * HEAD), F32),
        compiler_params=_cparams("parallel", "parallel"),
        name="natten",
    )(qkv, qkv, qkv, kc, vc, bias)


def _natten_bias(rpb):
    col = np.arange(GRID_W)
    col_start = np.clip(col - NA_WIN_C // 2, 0, GRID_W - NA_WIN_C)
    kc = np.arange(GRID_W)
    valid = (kc[None, :] >= col_start[:, None]) & (kc[None, :] < col_start[:, None] + NA_WIN_C)
    coff = np.clip(kc[None, :] - col[:, None] + NA_WIN_C - 1, 0, 2 * NA_WIN_C - 2)
    t = jnp.where(valid[None, None], rpb.astype(F32)[:, :, coff], NEG)
    t = jnp.pad(t, ((0, 0), (1, 1), (0, 0), (0, 0)), constant_values=NEG)
    return jnp.concatenate([t[:, :-1], t[:, 1:]], axis=-1)


def _gdn_kernel(q_ref, k_ref, v_ref, z_ref, g_ref, cwq_ref, cwk_ref, cwv_ref, par_ref, nrm_ref, s0_ref,
                o_ref, sfin_ref, qs, ks, vs, oacc):
    seq_len = q_ref.shape[1]
    nchunks = seq_len // CHUNK
    h = pl.program_id(1)

    def prep(c, _):
        r0 = pl.multiple_of(c * CHUNK, CHUNK)
        for src, cw, dst, kind in ((q_ref, cwq_ref, qs, "q"), (k_ref, cwk_ref, ks, "k"), (v_ref, cwv_ref, vs, "v")):
            y = _silu(_conv3(src, r0, c, nchunks, seq_len, cw, lead=(0,)))
            if kind != "v":
                y = y * lax.rsqrt(jnp.sum(y * y, axis=-1, keepdims=True) + EPS)
            if kind == "q":
                y = y * HEAD ** -0.5
            dst[pl.ds(r0, CHUNK), :] = y
        return 0

    lax.fori_loop(0, nchunks, prep, 0, unroll=2)

    ri = _iota((CHUNK, CHUNK), 0)
    ci = _iota((CHUNK, CHUNK), 1)
    neg_a = -jnp.exp(par_ref[0:1, :])
    dt_bias = par_ref[1:2, :]

    def load(c):
        r0 = pl.multiple_of(c * CHUNK, CHUNK)
        return (qs[pl.ds(r0, CHUNK), :], ks[pl.ds(r0, CHUNK), :], vs[pl.ds(r0, CHUNK), :],
                g_ref[0, pl.ds(r0, CHUNK), :])

    def gate_terms(loaded, d):
        q, k, v, gates = loaded
        beta = _lane_col(jax.nn.sigmoid(gates), d * GDN_HEADS + h)
        g_t = neg_a * _softplus(gates + dt_bias)
        gc_t = _cumsum_rows(g_t)
        tot_t = jnp.broadcast_to(gc_t[CHUNK - 1:CHUNK, :], gc_t.shape)
        if d == 1:
            gc_t = tot_t - gc_t + g_t
        gidx = 2 * GDN_HEADS + d * GDN_HEADS + h
        gc = _lane_col(gc_t, gidx)
        tot = _lane_col(tot_t, gidx)
        incl = (ri >= ci) if d == 0 else (ri <= ci)
        decay = jnp.exp(jnp.where(incl, gc - _row_form(gc), NEG))
        egc = jnp.exp(gc)
        kb = k * beta
        return dict(q=q, k=k, kb=kb, incl=incl, strict=(ri > ci) if d == 0 else (ri < ci), decay=decay,
                    x=jnp.concatenate([v * beta, kb * egc], axis=-1), q_dec=q * egc,
                    k_dec=k * jnp.exp(tot - gc), etot=jnp.exp(tot))

    unroll = min(GDN_UNROLL, nchunks)
    nsteps = nchunks // unroll

    def chunk_ids(i):
        cf = [i * unroll + j for j in range(unroll)]
        return cf, [nchunks - 1 - c for c in cf]

    def solve(i):
        cf, cb = chunk_ids(i)
        t = [gate_terms(load(c), d) for c, d in zip(cf + cb, [0] * unroll + [1] * unroll)]
        kk = [_dot_nt(ch["kb"], ch["k"]) for ch in t]
        qk = [_dot_nt(ch["q"], ch["k"]) for ch in t]
        yield
        a = [jnp.where(ch["strict"], m * ch["decay"], 0.0) for ch, m in zip(t, kk)]
        qk = [jnp.where(ch["incl"], m * ch["decay"], 0.0).astype(BF16) for ch, m in zip(t, qk)]
        inv = yield from _unit_tri_inverse_stages(a, ri, ci)
        x = _dots_many(inv, [ch["x"] for ch in t], TRI_PASSES[2])
        yield
        return tuple((x[n][:, :HEAD], x[n][:, HEAD:].astype(BF16), qk[n], t[n]["q_dec"].astype(BF16),
                      jnp.transpose(t[n]["k_dec"]).astype(BF16), jnp.broadcast_to(t[n]["etot"], (CHUNK, HEAD)))
                     for n in range(2 * unroll))

    def scan(i, pre, sf, sb):
        cf, cb = chunk_ids(i)
        outs = []
        for j in range(unroll):
            chains = (pre[j], pre[unroll + j])
            sbf = [s.astype(BF16) for s in (sf, sb)]
            ws = [jnp.dot(ch[1], s, preferred_element_type=F32) for ch, s in zip(chains, sbf)]
            o1 = [jnp.dot(ch[3], s, preferred_element_type=F32) for ch, s in zip(chains, sbf)]
            yield
            v_new = [(ch[0] - w).astype(BF16) for ch, w in zip(chains, ws)]
            o2 = [jnp.dot(ch[2], vn, preferred_element_type=F32) for ch, vn in zip(chains, v_new)]
            kv = [jnp.dot(ch[4], vn, preferred_element_type=F32) for ch, vn in zip(chains, v_new)]
            yield
            sf, sb = [s * ch[5] + m for ch, s, m in zip(chains, (sf, sb), kv)]
            outs.append([a1 + a2 for a1, a2 in zip(o1, o2)])
        for j in range(unroll):
            oacc[0, pl.ds(pl.multiple_of(cf[j] * CHUNK, CHUNK), CHUNK), :] = outs[j][0]
            oacc[1, pl.ds(pl.multiple_of(cb[j] * CHUNK, CHUNK), CHUNK), :] = outs[j][1]
        return sf, sb

    def step(i, carry):
        pre, sf, sb = carry
        nxt, (sf, sb) = _run_interleaved(solve(i + 1), scan(i, pre, sf, sb))
        return nxt, sf, sb

    pre, _ = _run_interleaved(solve(0), iter(()))
    pre, sf, sb = lax.fori_loop(0, nsteps - 1, step, (pre, s0_ref[0, 0, 0], s0_ref[0, 1, 0]))
    _, (sf, sb) = _run_interleaved(iter(()), scan(nsteps - 1, pre, sf, sb))
    sfin_ref[0, 0, 0] = sf
    sfin_ref[0, 1, 0] = sb

    def fin(c, _):
        r0 = pl.multiple_of(c * CHUNK, CHUNK)
        o = oacc[0, pl.ds(r0, CHUNK), :] + oacc[1, pl.ds(r0, CHUNK), :]
        o_ref[0, pl.ds(r0, CHUNK), :] = _rms(o, nrm_ref[...]) * _silu(z_ref[0, pl.ds(r0, CHUNK), :])
        return 0

    lax.fori_loop(0, nchunks, fin, 0, unroll=2)


def _gdn(x, gates, conv_w, par, norm_w, s0, layer):
    bsz, seq_len, _ = x.shape
    col = lambda off: pl.BlockSpec((1, seq_len, HEAD), lambda b, h: (b, 0, off + h))
    cw = lambda off: pl.BlockSpec((3, HEAD), lambda b, h: (0, off + h))
    st = pl.BlockSpec((1, 2, 1, HEAD, HEAD), lambda b, h: (b, 0, h, 0, 0))
    st_in = pl.BlockSpec((1, None, 2, 1, HEAD, HEAD), lambda b, h: (b, layer, 0, h, 0, 0))
    return pl.pallas_call(
        _gdn_kernel,
        grid=(bsz, GDN_HEADS),
        in_specs=[col(0), col(GDN_HEADS), col(2 * GDN_HEADS), col(3 * GDN_HEADS),
                  pl.BlockSpec((1, seq_len, HEAD), lambda b, h: (b, 0, 0)),
                  cw(0), cw(GDN_HEADS), cw(2 * GDN_HEADS),
                  pl.BlockSpec((8, HEAD), lambda b, h: (0, 0)),
                  pl.BlockSpec((1, HEAD), lambda b, h: (0, 0)),
                  st_in],
        out_specs=[pl.BlockSpec((1, seq_len, HEAD), lambda b, h: (b, 0, h)), st],
        out_shape=[jax.ShapeDtypeStruct((bsz, seq_len, GDN_HEADS * HEAD), F32),
                   jax.ShapeDtypeStruct((bsz, 2, GDN_HEADS, HEAD, HEAD), F32)],
        scratch_shapes=[pltpu.VMEM((seq_len, HEAD), F32)] * 3 + [pltpu.VMEM((2, seq_len, HEAD), F32)],
        compiler_params=_cparams("parallel", "parallel"),
        name="gdn",
    )(x, x, x, x, gates, conv_w, conv_w, conv_w, par, norm_w.reshape(1, HEAD), s0)


def _ssd_kernel(x_ref, bm_ref, cm_ref, z_ref, dt_ref, cwx_ref, cwb_ref, cwc_ref, cbx_ref, cbb_ref, cbc_ref,
                par_ref, dsk_ref, h0_ref, y_ref, hfin_ref, xs, bs, cs):
    seq_len = x_ref.shape[1]
    nchunks = seq_len // CHUNK
    width = SSD_HPG * SSD_P

    def prep(c, _):
        r0 = pl.multiple_of(c * CHUNK, CHUNK)
        for src, cw, cb, dst in ((x_ref, cwx_ref, cbx_ref, xs), (bm_ref, cwb_ref, cbb_ref, bs),
                                 (cm_ref, cwc_ref, cbc_ref, cs)):
            dst[pl.ds(r0, CHUNK), :] = _silu(_conv3(src, r0, c, nchunks, seq_len, cw, lead=(0,)) + cb[...])
        return 0

    lax.fori_loop(0, nchunks, prep, 0)

    ri = _iota((CHUNK, CHUNK), 0)
    ci = _iota((CHUNK, CHUNK), 1)
    head_of_lane = jnp.right_shift(_iota((CHUNK, width), 1), 6)
    neg_a = -jnp.exp(par_ref[0:1, :])
    dt_bias = par_ref[1:2, :]

    def expand(cols):
        out = jnp.broadcast_to(cols[0], (CHUNK, width))
        for e in range(1, SSD_HPG):
            out = jnp.where(head_of_lane == e, cols[e], out)
        return out

    def gate_terms(c, d):
        r0 = pl.multiple_of(c * CHUNK, CHUNK)
        x, bm, cm = xs[pl.ds(r0, CHUNK), :], bs[pl.ds(r0, CHUNK), :], cs[pl.ds(r0, CHUNK), :]
        dt_t = _softplus(dt_ref[0, pl.ds(r0, CHUNK), :] + dt_bias)
        la_t = dt_t * neg_a
        cs_t = _cumsum_rows(la_t)
        tot_row = cs_t[CHUNK - 1:CHUNK, :]
        tot_t = jnp.broadcast_to(tot_row, cs_t.shape)
        if d == 1:
            cs_t = tot_t - cs_t + la_t
        incl = (ri >= ci) if d == 0 else (ri <= ci)
        cs_rows = jnp.transpose(cs_t)
        rest_t = tot_t - cs_t
        tot_p = jnp.broadcast_to(tot_row, (SSD_P, HEAD))
        lanes = [d * SSD_HPG + e for e in range(SSD_HPG)]
        col = _lane_col
        lmats = [jnp.exp(jnp.where(incl, col(cs_t, k) - cs_rows[k:k + 1, :], NEG)) for k in lanes]
        xdt = x * expand([col(dt_t, k) for k in lanes])
        cs_full = expand([col(cs_t, k) for k in lanes])
        cd = jnp.concatenate([jnp.broadcast_to(jnp.exp(col(tot_p, k)), (SSD_P, HEAD)) for k in lanes], axis=0)
        return dict(r0=r0, x=x, bm=bm, cm=cm, xdt=xdt, lmats=lmats, ecs=jnp.exp(cs_full), cd=cd,
                    xdec_t=jnp.transpose(xdt * jnp.exp(expand([col(rest_t, k) for k in lanes]))))

    unroll = min(SSD_UNROLL, nchunks)

    def direction(d, h0, finish):
        def step(i, hstate):
            cidx = [i * unroll + j for j in range(unroll)]
            if d == 1:
                cidx = [nchunks - 1 - c for c in cidx]
            t = [gate_terms(c, d) for c in cidx]
            cb = [_dot_nt(ch["cm"], ch["bm"]) for ch in t]
            states = [_dot(ch["xdec_t"], ch["bm"]) for ch in t]
            ys = []
            for ch, m in zip(t, cb):
                y = None
                for e in range(SSD_HPG):
                    part = _dot(m * ch["lmats"][e], jnp.where(head_of_lane == e, ch["xdt"], 0.0))
                    y = part if y is None else y + part
                ys.append(y)
            for j, ch in enumerate(t):
                ys[j] = ys[j] + _dot_nt(ch["cm"], hstate) * ch["ecs"]
                hstate = hstate * ch["cd"] + states[j]
            for ch, y in zip(t, ys):
                finish(ch, y)
            return hstate

        return lax.fori_loop(0, nchunks // unroll, step, h0.reshape(width, HEAD))

    def store_fwd(ch, y):
        y_ref[0, pl.ds(ch["r0"], CHUNK), :] = y

    def store_bwd(ch, y):
        tot = y_ref[0, pl.ds(ch["r0"], CHUNK), :] + y + ch["x"] * dsk_ref[...]
        y_ref[0, pl.ds(ch["r0"], CHUNK), :] = tot * _silu(z_ref[0, pl.ds(ch["r0"], CHUNK), :])

    hfin_ref[0, 0] = direction(0, h0_ref[0, 0], store_fwd).reshape(SSD_HPG, SSD_P, HEAD)
    hfin_ref[0, 1] = direction(1, h0_ref[0, 1], store_bwd).reshape(SSD_HPG, SSD_P, HEAD)


def _ssd(x, gates, conv_w, conv_b, par, d_skip, h0, layer):
    bsz, seq_len, _ = x.shape
    width = SSD_HPG * SSD_P
    spec = lambda wd, off: pl.BlockSpec((1, seq_len, wd), lambda b, g: (b, 0, off + g))
    cws = lambda rows, wd, off: pl.BlockSpec((rows, wd), lambda b, g: (0, off + g))
    st = pl.BlockSpec((1, 2, SSD_HPG, SSD_P, HEAD), lambda b, g: (b, 0, g, 0, 0))
    st_in = pl.BlockSpec((1, None, 2, SSD_HPG, SSD_P, HEAD), lambda b, g: (b, layer, 0, g, 0, 0))
    return pl.pallas_call(
        _ssd_kernel,
        grid=(bsz, SSD_GROUPS),
        in_specs=[spec(width, 0), spec(HEAD, 4), spec(HEAD, 6), spec(width, 4),
                  pl.BlockSpec((1, seq_len, HEAD), lambda b, g: (b, 0, g)),
                  cws(3, width, 0), cws(3, HEAD, 4), cws(3, HEAD, 6),
                  cws(1, width, 0), cws(1, HEAD, 4), cws(1, HEAD, 6),
                  pl.BlockSpec((8, HEAD), lambda b, g: (0, g)),
                  cws(1, width, 0),
                  st_in],
        out_specs=[pl.BlockSpec((1, seq_len, width), lambda b, g: (b, 0, g)), st],
        out_shape=[jax.ShapeDtypeStruct((bsz, seq_len, SSD_HEADS * SSD_P), F32),
                   jax.ShapeDtypeStruct((bsz, 2, SSD_HEADS, SSD_P, HEAD), F32)],
        scratch_shapes=[pltpu.VMEM((seq_len, width), F32), pltpu.VMEM((seq_len, HEAD), F32),
                        pltpu.VMEM((seq_len, HEAD), F32)],
        compiler_params=_cparams("parallel", "parallel"),
        name="ssd",
    )(x, x, x, x, gates, conv_w, conv_w, conv_w, conv_b, conv_b, conv_b, par, d_skip, h0)


def _rope_tables(n_tokens):
    half = HEAD // 2
    inv_freq = ROPE_THETA ** (-jnp.arange(0, half, 2, dtype=F32) / half)
    t = jnp.arange(n_tokens)
    ang_r = (t // GRID_W).astype(F32)[:, None] * inv_freq
    ang_c = (t % GRID_W).astype(F32)[:, None] * inv_freq
    cos = jnp.concatenate([jnp.cos(ang_r)] * 2 + [jnp.cos(ang_c)] * 2, axis=-1)
    sin = jnp.concatenate([-jnp.sin(ang_r), jnp.sin(ang_r), -jnp.sin(ang_c), jnp.sin(ang_c)], axis=-1)
    return cos, sin


def _gate_rows(a_log, dt_bias, offset):
    n = a_log.size
    rows = jnp.zeros((8, HEAD), F32)
    rows = rows.at[0, offset:offset + n].set(a_log.reshape(-1).astype(F32))
    return rows.at[1, offset:offset + n].set(dt_bias.reshape(-1).astype(F32))


def _ssd_group_lanes(t):
    lead = t.shape[:-1]
    t = t.reshape(lead + (2, SSD_GROUPS, SSD_HPG))
    t = jnp.swapaxes(t, -3, -2).reshape(lead + (SSD_GROUPS, 2 * SSD_HPG))
    pad = [(0, 0)] * (t.ndim - 1) + [(0, HEAD - 2 * SSD_HPG)]
    return jnp.pad(t, pad).reshape(lead + (SSD_GROUPS * HEAD,))


def _pad_cols(w, width):
    return jnp.pad(w, ((0, 0), (0, width - w.shape[1])))


def kernel(x_prompt, x_sample, state_gdn, cache_gqa_k, cache_gqa_v, state_ssd, cache_na_k, cache_na_v, c, c_ctx, ada_w, ada_b, norm_mix_pre, norm_mix_post, norm_mlp_pre, norm_mlp_post, mlp_w1, mlp_w2, ev_w_in, ev_w_out, gdn_conv, gdn_a_log, gdn_dt_bias, gdn_norm, gqa_q_norm, gqa_k_norm, od_w_in, od_w_out, ssd_conv, ssd_conv_b, ssd_a_log, ssd_dt_bias, ssd_d, ssd_norm, na_rpb):
    nb, ns, _ = x_prompt.shape
    db, dl, _ = x_sample.shape
    past = cache_gqa_k.shape[2]

    c16 = jnp.zeros((16, D_MODEL), F32).at[:db].set(c).at[db].set(c_ctx)
    mods = _ada(c16, ada_w, ada_b).reshape(DEPTH, 16, 6, D_MODEL)

    cos_l, sin_l = _rope_tables(dl)
    cos_c, sin_c = jnp.ones((ns, HEAD), F32), jnp.zeros((ns, HEAD), F32)

    xp = x_prompt.reshape(1, nb * ns, D_MODEL)
    xs = x_sample
    tm_c = min(1024, nb * ns)
    tm_in = min(512, nb * ns, dl)
    zero_gdn = jnp.zeros((nb, 1, 2, GDN_HEADS, HEAD, HEAD), F32)
    zero_ssd = jnp.zeros((nb, 1, 2, SSD_HEADS, SSD_P, HEAD), F32)
    new_gdn, new_gk, new_gv, new_ssd, new_nk, new_nv = [], [], [], [], [], []

    for i in range(DEPTH):
        j = i // 2
        mod_l, mod_c = mods[i, :db], mods[i, db:db + 1]
        if i % 2 == 0:
            wi = ev_w_in[j]
            w_in = jnp.concatenate([wi[:, :2048], wi[:, 2064:], _pad_cols(wi[:, 2048:2064], HEAD)], axis=1).astype(BF16)
            widths = (2048, 1024, HEAD)
            w_out = ev_w_out[j].astype(BF16)
            par = _gate_rows(gdn_a_log[j], gdn_dt_bias[j], 2 * GDN_HEADS)

            gx, ax, gt = _inproj(xp, mod_c, norm_mix_pre[i], w_in, widths, tm_in)
            gx, ax, gt = (t.reshape(nb, ns, -1) for t in (gx, ax, gt))
            o_a, s_a = _gdn(gx, gt, gdn_conv[j], par, gdn_norm[j], zero_gdn, 0)
            q, k, v, kf = _qkprep(ax, cos_c, sin_c, gqa_q_norm[j], gqa_k_norm[j], ns, True)
            o_b = _flash(q, k, v, GQA_KV_HEADS, 2, ns, ns, True)
            yc = _outproj(o_a.reshape(1, nb * ns, -1), o_b.reshape(1, nb * ns, -1), xp, mod_c,
                          w_out, norm_mix_post[i], tm_c)
            new_gdn.append(s_a)
            new_gk.append(kf.reshape(nb, ns, GQA_KV_HEADS, HEAD))
            new_gv.append(ax[..., 768:].reshape(nb, ns, GQA_KV_HEADS, HEAD))

            gx, ax, gt = _inproj(xs, mod_l, norm_mix_pre[i], w_in, widths, tm_in)
            o_a, _ = _gdn(gx, gt, gdn_conv[j], par, gdn_norm[j], state_gdn, j)
            q, k, v = _qkprep(ax, cos_l, sin_l, gqa_q_norm[j], gqa_k_norm[j], 512, False)
            o_b = _flash(q, k, v, GQA_KV_HEADS, 2, 256, min(2048, dl), True,
                         cache=(cache_gqa_k.reshape(db, -1, past, GQA_KV_HEADS * HEAD),
                                cache_gqa_v.reshape(db, -1, past, GQA_KV_HEADS * HEAD), j))
            yl = _outproj(o_a, o_b, xs, mod_l, w_out, norm_mix_post[i], 512)
        else:
            wi = od_w_in[j]
            w_in = jnp.concatenate([wi[:, 512:1536], wi[:, :512], wi[:, 1552:], _ssd_group_lanes(wi[:, 1536:1552])], axis=1).astype(BF16)
            widths = (1536, 1536, SSD_GROUPS * HEAD)
            w_out = od_w_out[j].astype(BF16)
            par = jnp.zeros((8, SSD_GROUPS * HEAD), F32)
            par = par.at[0].set(_ssd_group_lanes(ssd_a_log[j].reshape(-1).astype(F32)))
            par = par.at[1].set(_ssd_group_lanes(ssd_dt_bias[j].reshape(-1).astype(F32)))
            dsk = jnp.repeat(ssd_d[j].astype(F32), SSD_P).reshape(1, -1)
            cvw, cvb = ssd_conv[j], ssd_conv_b[j].reshape(1, -1)

            sx, ax, gt = _inproj(xp, mod_c, norm_mix_pre[i], w_in, widths, tm_in)
            sx, ax, gt = (t.reshape(nb, ns, -1) for t in (sx, ax, gt))
            y_c, s_c = _ssd(sx, gt, cvw, cvb, par, dsk, zero_ssd, 0)
            o_d = _flash(ax, ax, ax, NA_HEADS, 1, ns, ns, False, (0, NA_HEADS, 2 * NA_HEADS))
            yc = _outproj(y_c.reshape(1, nb * ns, -1), o_d.reshape(1, nb * ns, -1), xp, mod_c,
                          w_out, norm_mix_post[i], tm_c, ssd_norm[j])
            new_ssd.append(s_c)
            new_nk.append(ax[..., 512:1024].reshape(nb, ns, NA_HEADS, HEAD))
            new_nv.append(ax[..., 1024:].reshape(nb, ns, NA_HEADS, HEAD))

            sx, ax, gt = _inproj(xs, mod_l, norm_mix_pre[i], w_in, widths, tm_in)
            y_c, _ = _ssd(sx, gt, cvw, cvb, par, dsk, state_ssd, j)
            o_d = _natten(ax, cache_na_k.reshape(db, -1, past, NA_HEADS * HEAD),
                          cache_na_v.reshape(db, -1, past, NA_HEADS * HEAD), j, _natten_bias(na_rpb[j]))
            yl = _outproj(y_c, o_d, xs, mod_l, w_out, norm_mix_post[i], 512, ssd_norm[j])
        xp, xs = yc, yl
        w1, w2 = mlp_w1[i].astype(BF16), mlp_w2[i].astype(BF16)
        xp = _mlp(xp, mod_c, norm_mlp_pre[i], w1, w2, norm_mlp_post[i], tm_c, 1024)
        xs = _mlp(xs, mod_l, norm_mlp_pre[i], w1, w2, norm_mlp_post[i], 1024, 1024)

    return (xp.reshape(nb, ns, D_MODEL), xs, jnp.stack(new_gdn, axis=1), jnp.stack(new_gk, axis=1),
            jnp.stack(new_gv, axis=1), jnp.stack(new_ssd, axis=1), jnp.stack(new_nk, axis=1),
            jnp.stack(new_nv, axis=1))
```

```python
import functools

import numpy as np
import jax
import jax.numpy as jnp
from jax import lax
from jax.experimental import pallas as pl
from jax.experimental.pallas import tpu as pltpu

F32 = jnp.float32
BF16 = jnp.bfloat16

D_MODEL = 1024
DEPTH = 4
GRID_W = 64
EPS = 1e-6
ROPE_THETA = 10000.0
D_FF = 4 * D_MODEL
HEAD = 128
GDN_HEADS = 4
GQA_HEADS = 4
GQA_KV_HEADS = 2
SSD_HEADS = 8
SSD_P = 64
SSD_GROUPS = 2
SSD_HPG = SSD_HEADS // SSD_GROUPS
NA_HEADS = 4
NA_WIN_R = 8
NA_WIN_C = 16
CHUNK = 128
GDN_UNROLL = 4
SSD_UNROLL = 4
TRI_PASSES = (1, 1, 2)
NEG = -1e30
QK_SCALE_LOG2 = HEAD ** -0.5 * float(np.log2(np.e))
FLASH_SAFE_BOUND = 60.0
FLASH_SHIFT = 64.0
VMEM_LIMIT_BYTES = 56 * 1024 * 1024


def _cparams(*sem):
    return pltpu.CompilerParams(dimension_semantics=sem, vmem_limit_bytes=VMEM_LIMIT_BYTES)


def _dot(a, b):
    return jnp.dot(a.astype(BF16), b.astype(BF16), preferred_element_type=F32)


def _dot_nt(a, b):
    return lax.dot_general(a.astype(BF16), b.astype(BF16), (((1,), (1,)), ((), ())),
                           preferred_element_type=F32)


def _split_bf16(a):
    hi = a.astype(BF16)
    lo = (a - hi.astype(F32)).astype(BF16)
    return hi, lo


def _dot3(a, b):
    ah, al = _split_bf16(a)
    bh, bl = _split_bf16(b)
    d = lambda x, y: jnp.dot(x, y, preferred_element_type=F32)
    return d(ah, bh) + (d(ah, bl) + d(al, bh))


def _silu(x):
    return x * jax.nn.sigmoid(x)


def _softplus(x):
    return jnp.maximum(x, 0.0) + jnp.log(1.0 + jnp.exp(-jnp.abs(x)))


def _rms(x, w):
    return x * lax.rsqrt(jnp.mean(x * x, axis=-1, keepdims=True) + EPS) * w


def _iota(shape, axis):
    return lax.broadcasted_iota(jnp.int32, shape, axis)


def _lane_col(tile, idx):
    return jnp.sum(jnp.where(_iota(tile.shape, 1) == idx, tile, 0.0), axis=-1, keepdims=True)


def _cumsum_rows(t):
    rows = _iota(t.shape, 0)
    s = 1
    while s < t.shape[0]:
        t = t + jnp.where(rows >= s, pltpu.roll(t, s, 0), 0.0)
        s *= 2
    return t


def _row_form(col):
    n = col.shape[0]
    return jnp.transpose(jnp.broadcast_to(col, (n, n)))


def _dot3_many(xs, ys):
    xs = [_split_bf16(x) for x in xs]
    ys = [_split_bf16(y) for y in ys]
    d = lambda x, y: jnp.dot(x, y, preferred_element_type=F32)
    hh = [d(x[0], y[0]) for x, y in zip(xs, ys)]
    hl = [d(x[0], y[1]) for x, y in zip(xs, ys)]
    lh = [d(x[1], y[0]) for x, y in zip(xs, ys)]
    return [a + (b + c) for a, b, c in zip(hh, hl, lh)]


def _dots_many(xs, ys, passes):
    if passes == 3:
        return _dot3_many(xs, ys)
    if passes == 2:
        d = lambda x, y: jnp.dot(x, y, preferred_element_type=F32)
        xs = [x.astype(BF16) for x in xs]
        ys = [_split_bf16(y) for y in ys]
        hi = [d(x, y[0]) for x, y in zip(xs, ys)]
        lo = [d(x, y[1]) for x, y in zip(xs, ys)]
        return [a + b for a, b in zip(hi, lo)]
    return [_dot(x, y) for x, y in zip(xs, ys)]


def _unit_tri_inverse_stages(a_list, ri, ci):
    same = lambda sh: jnp.right_shift(ri, sh) == jnp.right_shift(ci, sh)
    eye = jnp.where(ri == ci, 1.0, 0.0)
    p = [jnp.where(same(4), a, 0.0) for a in a_list]
    m = [eye - x for x in p]
    p = _dots_many(p, p, TRI_PASSES[0])
    yield
    for last in (False, False, True):
        mp = _dots_many(m, p, TRI_PASSES[0])
        if not last:
            p = _dots_many(p, p, TRI_PASSES[0])
        yield
        m = [x + y for x, y in zip(m, mp)]
    for sh in (4, 5, 6):
        off = [jnp.where(same(sh + 1), jnp.where(same(sh), 0.0, a), 0.0) for a in a_list]
        t = _dots_many(off, m, TRI_PASSES[1])
        yield
        mt = _dots_many(m, t, TRI_PASSES[1])
        yield
        m = [x - y for x, y in zip(m, mt)]
    return m


def _run_interleaved(gen_a, gen_b):
    results, live = [None, None], [gen_a, gen_b]
    while any(g is not None for g in live):
        for n, g in enumerate(live):
            if g is None:
                continue
            try:
                next(g)
            except StopIteration as stop:
                results[n], live[n] = stop.value, None
    return results


def _conv3(ref, r0, c, nchunks, seq_len, w, lead=()):
    x = ref[lead + (pl.ds(r0, CHUNK), slice(None))]
    prev = ref[lead + (pl.ds(jnp.maximum(r0 - 1, 0), 1), slice(None))]
    nxt = ref[lead + (pl.ds(jnp.minimum(r0 + CHUNK, seq_len - 1), 1), slice(None))]
    prev = jnp.where(c == 0, 0.0, prev)
    nxt = jnp.where(c == nchunks - 1, 0.0, nxt)
    rows = _iota(x.shape, 0)
    xm = jnp.where(rows == 0, prev, pltpu.roll(x, 1, 0))
    xp = jnp.where(rows == CHUNK - 1, nxt, pltpu.roll(x, CHUNK - 1, 0))
    return xm * w[0:1, :] + x * w[1:2, :] + xp * w[2:3, :]


def _ada_kernel(c_ref, w_ref, b_ref, o_ref):
    o_ref[0] = _dot(_silu(c_ref[...]), w_ref[0]) + b_ref[0]


def _ada(c16, ada_w, ada_b):
    tn = 1536
    n = ada_w.shape[-1]
    return pl.pallas_call(
        _ada_kernel,
        grid=(DEPTH, n // tn),
        in_specs=[pl.BlockSpec((16, D_MODEL), lambda i, j: (0, 0)),
                  pl.BlockSpec((1, D_MODEL, tn), lambda i, j: (i, 0, j)),
                  pl.BlockSpec((1, 1, tn), lambda i, j: (i, 0, j))],
        out_specs=pl.BlockSpec((1, 16, tn), lambda i, j: (i, 0, j)),
        out_shape=jax.ShapeDtypeStruct((DEPTH, 16, n), F32),
        compiler_params=_cparams("parallel", "parallel"),
        name="ada",
    )(c16, ada_w, ada_b.reshape(DEPTH, 1, n))


def _mod_spec(mod):
    if mod.shape[0] == 1:
        return pl.BlockSpec((1, 6, D_MODEL), lambda b, t: (0, 0, 0))
    return pl.BlockSpec((1, 6, D_MODEL), lambda b, t: (b, 0, 0))


def _inproj_kernel(x_ref, mod_ref, nw_ref, w_ref, *o_refs, splits):
    h = _rms(x_ref[0], nw_ref[...])
    h = h * (1.0 + mod_ref[0, 1:2, :]) + mod_ref[0, 0:1, :]
    hb = h.astype(BF16)
    for o_ref, (a, b) in zip(o_refs, splits):
        o_ref[0] = jnp.dot(hb, w_ref[:, a:b], preferred_element_type=F32)


def _inproj(x, mod, norm_w, w, widths, tm):
    bsz, seq_len, _ = x.shape
    splits, a = [], 0
    for wd in widths:
        splits.append((a, a + wd))
        a += wd
    return pl.pallas_call(
        functools.partial(_inproj_kernel, splits=tuple(splits)),
        grid=(bsz, seq_len // tm),
        in_specs=[pl.BlockSpec((1, tm, D_MODEL), lambda b, t: (b, t, 0)),
                  _mod_spec(mod),
                  pl.BlockSpec((1, D_MODEL), lambda b, t: (0, 0)),
                  pl.BlockSpec(w.shape, lambda b, t: (0, 0))],
        out_specs=[pl.BlockSpec((1, tm, wd), lambda b, t: (b, t, 0)) for wd in widths],
        out_shape=[jax.ShapeDtypeStruct((bsz, seq_len, wd), F32) for wd in widths],
        compiler_params=_cparams("parallel", "parallel"),
        name="inproj",
    )(x, mod, norm_w.reshape(1, D_MODEL), w)


def _outproj_kernel(a_ref, b_ref, x_ref, mod_ref, w_ref, nw_ref, *rest, norm_a):
    a = a_ref[0]
    if norm_a:
        na_ref, o_ref = rest
        a = _rms(a, na_ref[...])
    else:
        (o_ref,) = rest
    half = a.shape[-1]
    y = (jnp.dot(a.astype(BF16), w_ref[:half, :], preferred_element_type=F32)
         + jnp.dot(b_ref[0].astype(BF16), w_ref[half:, :], preferred_element_type=F32))
    o_ref[0] = x_ref[0] + mod_ref[0, 2:3, :] * _rms(y, nw_ref[...])


def _outproj(a, b, x, mod, w, norm_w, tm, norm_a_w=None):
    bsz, seq_len, _ = x.shape
    half = a.shape[-1]
    in_specs = [pl.BlockSpec((1, tm, half), lambda bb, t: (bb, t, 0)),
                pl.BlockSpec((1, tm, half), lambda bb, t: (bb, t, 0)),
                pl.BlockSpec((1, tm, D_MODEL), lambda bb, t: (bb, t, 0)),
                _mod_spec(mod),
                pl.BlockSpec(w.shape, lambda bb, t: (0, 0)),
                pl.BlockSpec((1, D_MODEL), lambda bb, t: (0, 0))]
    args = [a, b, x, mod, w, norm_w.reshape(1, D_MODEL)]
    if norm_a_w is not None:
        in_specs.append(pl.BlockSpec((1, half), lambda bb, t: (0, 0)))
        args.append(norm_a_w.reshape(1, half))
    return pl.pallas_call(
        functools.partial(_outproj_kernel, norm_a=norm_a_w is not None),
        grid=(bsz, seq_len // tm),
        in_specs=in_specs,
        out_specs=pl.BlockSpec((1, tm, D_MODEL), lambda bb, t: (bb, t, 0)),
        out_shape=jax.ShapeDtypeStruct(x.shape, F32),
        compiler_params=_cparams("parallel", "parallel"),
        name="outproj",
    )(*args)


def _mlp_kernel(x_ref, mod_ref, npre_ref, w1_ref, w2_ref, npost_ref, o_ref, h_ref, acc_ref):
    f = pl.program_id(2)

    @pl.when(f == 0)
    def _():
        h = _rms(x_ref[0], npre_ref[...])
        h = h * (1.0 + mod_ref[0, 4:5, :]) + mod_ref[0, 3:4, :]
        h_ref[...] = h.astype(BF16)
        acc_ref[...] = jnp.zeros_like(acc_ref)

    u = jnp.maximum(jnp.dot(h_ref[...], w1_ref[...], preferred_element_type=F32), 0.0)
    acc_ref[...] += jnp.dot((u * u).astype(BF16), w2_ref[...], preferred_element_type=F32)

    @pl.when(f == pl.num_programs(2) - 1)
    def _():
        o_ref[0] = x_ref[0] + mod_ref[0, 5:6, :] * _rms(acc_ref[...], npost_ref[...])


def _mlp(x, mod, npre, w1, w2, npost, tm, tf):
    bsz, seq_len, _ = x.shape
    mspec = _mod_spec(mod)
    return pl.pallas_call(
        _mlp_kernel,
        grid=(bsz, seq_len // tm, D_FF // tf),
        in_specs=[pl.BlockSpec((1, tm, D_MODEL), lambda b, t, f: (b, t, 0)),
                  pl.BlockSpec((1, 6, D_MODEL), lambda b, t, f: mspec.index_map(b, t)),
                  pl.BlockSpec((1, D_MODEL), lambda b, t, f: (0, 0)),
                  pl.BlockSpec((D_MODEL, tf), lambda b, t, f: (0, f)),
                  pl.BlockSpec((tf, D_MODEL), lambda b, t, f: (f, 0)),
                  pl.BlockSpec((1, D_MODEL), lambda b, t, f: (0, 0))],
        out_specs=pl.BlockSpec((1, tm, D_MODEL), lambda b, t, f: (b, t, 0)),
        out_shape=jax.ShapeDtypeStruct(x.shape, F32),
        scratch_shapes=[pltpu.VMEM((tm, D_MODEL), BF16), pltpu.VMEM((tm, D_MODEL), F32)],
        compiler_params=_cparams("parallel", "parallel", "arbitrary"),
        name="mlp",
    )(x, mod, npre.reshape(1, D_MODEL), w1, w2, npost.reshape(1, D_MODEL))


def _qkprep_kernel(x_ref, cos_ref, sin_ref, qn_ref, kn_ref, q_ref, k_ref, v_ref, kf_ref=None):
    cos, sin = cos_ref[...], sin_ref[...]
    lane = _iota(cos.shape, 1)
    first = (lane & 32) == 0

    def prep(xh, w):
        y = _rms(xh, w)
        swapped = jnp.where(first, pltpu.roll(y, HEAD - 32, 1), pltpu.roll(y, 32, 1))
        return y, y * cos + swapped * sin

    nq = GQA_HEADS * HEAD
    for hd in range(GQA_HEADS):
        _, r = prep(x_ref[0, :, hd * HEAD:(hd + 1) * HEAD], qn_ref[...])
        q_ref[0, :, hd * HEAD:(hd + 1) * HEAD] = (r * QK_SCALE_LOG2).astype(BF16)
    for hd in range(GQA_KV_HEADS):
        y, r = prep(x_ref[0, :, nq + hd * HEAD:nq + (hd + 1) * HEAD], kn_ref[...])
        k_ref[0, :, hd * HEAD:(hd + 1) * HEAD] = r.astype(BF16)
        if kf_ref is not None:
            kf_ref[0, :, hd * HEAD:(hd + 1) * HEAD] = y
    nk = GQA_KV_HEADS * HEAD
    v_ref[0] = x_ref[0, :, nq + nk:nq + 2 * nk].astype(BF16)


def _qkprep(x, cos, sin, q_norm, k_norm, tm, want_plain_k):
    bsz, seq_len, width = x.shape
    nq, nk = GQA_HEADS * HEAD, GQA_KV_HEADS * HEAD
    row = lambda wd: pl.BlockSpec((1, tm, wd), lambda b, t: (b, t, 0))
    n_out = 4 if want_plain_k else 3
    return pl.pallas_call(
        _qkprep_kernel,
        grid=(bsz, seq_len // tm),
        in_specs=[row(width),
                  pl.BlockSpec((tm, HEAD), lambda b, t: (t, 0)),
                  pl.BlockSpec((tm, HEAD), lambda b, t: (t, 0)),
                  pl.BlockSpec((1, HEAD), lambda b, t: (0, 0)),
                  pl.BlockSpec((1, HEAD), lambda b, t: (0, 0))],
        out_specs=[row(nq), row(nk), row(nk), row(nk)][:n_out],
        out_shape=[jax.ShapeDtypeStruct((bsz, seq_len, nq), BF16),
                   jax.ShapeDtypeStruct((bsz, seq_len, nk), BF16),
                   jax.ShapeDtypeStruct((bsz, seq_len, nk), BF16),
                   jax.ShapeDtypeStruct((bsz, seq_len, nk), F32)][:n_out],
        compiler_params=_cparams("parallel", "parallel"),
        name="qkprep",
    )(x, cos, sin, q_norm.reshape(1, HEAD), k_norm.reshape(1, HEAD))


def _flash_kernel(q_ref, k_ref, v_ref, *rest, rep, tk, prescaled, with_cache):
    if with_cache:
        kc_ref, vc_ref, o_ref, knorm_ref = rest
    else:
        o_ref, knorm_ref = rest
    tq = q_ref.shape[1]
    if rep == 2:
        qs = [q_ref[0, :, r * HEAD:(r + 1) * HEAD] for r in range(rep)]
    else:
        qs = [q_ref[0, :tq // 2, :], q_ref[0, tq // 2:, :]]
    if not prescaled:
        qs = [(q.astype(F32) * QK_SCALE_LOG2).astype(BF16) for q in qs]
    nk = k_ref.shape[1] // tk
    rows = qs[0].shape[0]

    def lane_chunks(s):
        return [s[:, j * HEAD:(j + 1) * HEAD] for j in range(s.shape[1] // HEAD)]

    def row_max(k, part):
        s = [_dot_nt(q, k) for q in qs]
        return tuple(functools.reduce(jnp.maximum, lane_chunks(sc), m) for sc, m in zip(s, part))

    def accumulate(k, v, mx, carry):
        s = [_dot_nt(q, k) for q in qs]
        ps, ls = [], []
        for sc, m, (l, _) in zip(s, mx, carry):
            pc = [jnp.exp2(c - m) for c in lane_chunks(sc)]
            ls.append(functools.reduce(jnp.add, pc, l))
            ps.append(jnp.concatenate([c.astype(BF16) for c in pc], axis=-1))
        pv = [jnp.dot(p, v.astype(BF16), preferred_element_type=F32) for p in ps]
        return tuple((l, acc + x) for l, (_, acc), x in zip(ls, carry, pv))

    def block(ref, i):
        return ref[0, pl.ds(pl.multiple_of(i * tk, tk), tk), :]

    def max_sq_norm(k, best):
        kf = k.astype(BF16).astype(F32)
        sq = jnp.broadcast_to(jnp.sum(kf * kf, axis=-1, keepdims=True), kf.shape)
        return jnp.maximum(best, jnp.max(sq, axis=0, keepdims=True))

    @pl.when(pl.program_id(2) == 0)
    def _():
        best = lax.fori_loop(0, nk, lambda i, b: max_sq_norm(block(k_ref, i), b), jnp.zeros((1, HEAD), F32))
        if with_cache:
            best = max_sq_norm(kc_ref[0], best)
        knorm_ref[...] = jnp.sqrt(best)

    def exact_row_max():
        part = tuple(jnp.full((rows, HEAD), NEG, F32) for _ in qs)
        part = lax.fori_loop(0, nk, lambda i, p: row_max(block(k_ref, i), p), part)
        if with_cache:
            part = row_max(kc_ref[0], part)
        return [jnp.broadcast_to(jnp.max(m, axis=-1, keepdims=True), (rows, HEAD)) for m in part]

    bounds = []
    for q in qs:
        qf = q.astype(F32)
        qn = jnp.sqrt(jnp.sum(qf * qf, axis=-1, keepdims=True))
        bounds.append(jnp.broadcast_to(qn, (rows, HEAD)) * knorm_ref[...])
    safe = jnp.max(functools.reduce(jnp.maximum, bounds)) <= FLASH_SAFE_BOUND
    mx = lax.cond(safe, lambda: [b - FLASH_SHIFT for b in bounds], exact_row_max)

    stats = tuple((jnp.zeros((rows, HEAD), F32), jnp.zeros((rows, HEAD), F32)) for _ in qs)
    stats = lax.fori_loop(0, nk, lambda i, c: accumulate(block(k_ref, i), block(v_ref, i), mx, c), stats)
    if with_cache:
        stats = accumulate(kc_ref[0], vc_ref[0], mx, stats)
    outs = [(acc / jnp.sum(l, axis=-1, keepdims=True)).astype(o_ref.dtype) for l, acc in stats]
    if rep == 2:
        for r in range(rep):
            o_ref[0, :, r * HEAD:(r + 1) * HEAD] = outs[r]
    else:
        o_ref[0, :tq // 2, :] = outs[0]
        o_ref[0, tq // 2:, :] = outs[1]


def _flash(q, k, v, kv_heads, rep, tq, tk, prescaled, head_offsets=(0, 0, 0), cache=None):
    bsz, seq_len, _ = q.shape
    assert rep in (1, 2) and head_offsets[0] % rep == 0
    m_len = k.shape[1]
    qo, ko, vo = head_offsets[0] // rep, head_offsets[1], head_offsets[2]
    in_specs = [pl.BlockSpec((1, tq, rep * HEAD), lambda b, g, t: (b, t, qo + g)),
                pl.BlockSpec((1, m_len, HEAD), lambda b, g, t: (b, 0, ko + g)),
                pl.BlockSpec((1, m_len, HEAD), lambda b, g, t: (b, 0, vo + g))]
    args = [q, k, v]
    if cache is not None:
        layer = cache[2]
        spec = pl.BlockSpec((1, None, cache[0].shape[2], HEAD), lambda b, g, t: (b, layer, 0, g))
        in_specs += [spec, spec]
        args += [cache[0], cache[1]]
    return pl.pallas_call(
        functools.partial(_flash_kernel, rep=rep, tk=tk, prescaled=prescaled, with_cache=cache is not None),
        grid=(bsz, kv_heads, seq_len // tq),
        in_specs=in_specs,
        out_specs=pl.BlockSpec((1, tq, rep * HEAD), lambda b, g, t: (b, t, g)),
        out_shape=jax.ShapeDtypeStruct((bsz, seq_len, kv_heads * rep * HEAD), BF16),
        scratch_shapes=[pltpu.VMEM((1, HEAD), F32)],
        compiler_params=_cparams("parallel", "parallel", "arbitrary"),
        name="flash",
    )(*args)


NA_GROUP = 4
NA_BAND = NA_GROUP + NA_WIN_R
NA_STEP = 2


def _natten_kernel(q_ref, k_ref, v_ref, kc_ref, vc_ref, bias_ref, o_ref):
    rows = q_ref.shape[1] // GRID_W
    gq = NA_GROUP * GRID_W
    kc = kc_ref[0].astype(BF16)
    vc = vc_ref[0].astype(BF16)
    low_half = _iota((GRID_W, HEAD), 1) < GRID_W

    def group_bias(r_first, band_first):
        per_row = []
        for i in range(NA_GROUP):
            r = r_first + i
            r0 = jnp.clip(r - NA_WIN_R // 2, 0, rows - NA_WIN_R)
            blocks = []
            for jj in range(NA_BAND // 2):
                kr = band_first + 2 * jj
                blk = bias_ref[0, jnp.clip(kr - r + NA_WIN_R, 0, 2 * NA_WIN_R - 1)]
                ok_a = jnp.logical_and(kr >= r0, kr < r0 + NA_WIN_R)
                ok_b = jnp.logical_and(kr + 1 >= r0, kr + 1 < r0 + NA_WIN_R)
                blocks.append(jnp.where(low_half, jnp.where(ok_a, blk, NEG), jnp.where(ok_b, blk, NEG)))
            per_row.append(jnp.concatenate(blocks, axis=-1))
        return jnp.concatenate(per_row, axis=0)

    def body(it, _):
        firsts = [(it * NA_STEP + g) * NA_GROUP for g in range(NA_STEP)]
        bands = [jnp.clip(r - NA_WIN_R // 2, 0, rows - NA_BAND) for r in firsts]
        q0 = [pl.multiple_of(r * GRID_W, gq) for r in firsts]
        k0 = [pl.multiple_of(b * GRID_W, GRID_W) for b in bands]
        qs = [(q_ref[0, pl.ds(a, gq), :] * QK_SCALE_LOG2).astype(BF16) for a in q0]
        ks = [k_ref[0, pl.ds(a, NA_BAND * GRID_W), :] for a in k0]
        vs = [v_ref[0, pl.ds(a, NA_BAND * GRID_W), :] for a in k0]
        bias = [group_bias(r, b) for r, b in zip(firsts, bands)]
        s_loc = [_dot_nt(q, k) for q, k in zip(qs, ks)]
        s_ctx = [_dot_nt(q, kc) for q in qs]
        s_loc = [s + b for s, b in zip(s_loc, bias)]
        m = [jnp.maximum(jnp.max(a, axis=-1, keepdims=True), jnp.max(c, axis=-1, keepdims=True))
             for a, c in zip(s_loc, s_ctx)]
        p_loc = [jnp.exp2(a - mm) for a, mm in zip(s_loc, m)]
        p_ctx = [jnp.exp2(c - mm) for c, mm in zip(s_ctx, m)]
        l = [jnp.sum(a, axis=-1, keepdims=True) + jnp.sum(c, axis=-1, keepdims=True) for a, c in zip(p_loc, p_ctx)]
        o_loc = [_dot(p, v) for p, v in zip(p_loc, vs)]
        o_ctx = [_dot(p, vc) for p in p_ctx]
        for a, ol, oc, ll in zip(q0, o_loc, o_ctx, l):
            o_ref[0, pl.ds(a, gq), :] = ((ol + oc) / ll).astype(o_ref.dtype)
        return 0

    lax.fori_loop(0, rows // (NA_GROUP * NA_STEP), body, 0)


def _natten(qkv, kc, vc, layer, bias):
    bsz, seq_len, _ = qkv.shape
    m_len = kc.shape[2]
    col = lambda off: pl.BlockSpec((1, seq_len, HEAD), lambda b, h: (b, 0, off + h))
    ctx = pl.BlockSpec((1, None, m_len, HEAD), lambda b, h: (b, layer, 0, h))
    return pl.pallas_call(
        _natten_kernel,
        grid=(bsz, NA_HEADS),
        in_specs=[col(0), col(NA_HEADS), col(2 * NA_HEADS), ctx, ctx,
                  pl.BlockSpec((1,) + bias.shape[1:], lambda b, h: (h, 0, 0, 0))],
        out_specs=pl.BlockSpec((1, seq_len, HEAD), lambda b, h: (b, 0, h)),
        out_shape=jax.ShapeDtypeStruct((bsz, seq_len, NA_HEADS * HEAD), BF16),
        compiler_params=_cparams("parallel", "parallel"),
        name="natten",
    )(qkv, qkv, qkv, kc, vc, bias)


def _natten_bias(rpb):
    col = np.arange(GRID_W)
    col_start = np.clip(col - NA_WIN_C // 2, 0, GRID_W - NA_WIN_C)
    kc = np.arange(GRID_W)
    valid = (kc[None, :] >= col_start[:, None]) & (kc[None, :] < col_start[:, None] + NA_WIN_C)
    coff = np.clip(kc[None, :] - col[:, None] + NA_WIN_C - 1, 0, 2 * NA_WIN_C - 2)
    log2e = float(np.log2(np.e))
    t = jnp.where(valid[None, None], rpb.astype(F32)[:, :, coff] * log2e, NEG)
    t = jnp.pad(t, ((0, 0), (1, 1), (0, 0), (0, 0)), constant_values=NEG)
    return jnp.concatenate([t[:, :-1], t[:, 1:]], axis=-1)


def _gdn_kernel(q_ref, k_ref, v_ref, z_ref, g_ref, cwq_ref, cwk_ref, cwv_ref, par_ref, nrm_ref, s0_ref,
                o_ref, sfin_ref, qs, ks, vs, oacc):
    seq_len = q_ref.shape[1]
    nchunks = seq_len // CHUNK
    h = pl.program_id(1)

    def prep(c, _):
        r0 = pl.multiple_of(c * CHUNK, CHUNK)
        for src, cw, dst, kind in ((q_ref, cwq_ref, qs, "q"), (k_ref, cwk_ref, ks, "k"), (v_ref, cwv_ref, vs, "v")):
            y = _silu(_conv3(src, r0, c, nchunks, seq_len, cw, lead=(0,)))
            if kind != "v":
                y = y * lax.rsqrt(jnp.sum(y * y, axis=-1, keepdims=True) + EPS)
            if kind == "q":
                y = y * HEAD ** -0.5
            dst[pl.ds(r0, CHUNK), :] = y
        return 0

    lax.fori_loop(0, nchunks, prep, 0, unroll=2)

    ri = _iota((CHUNK, CHUNK), 0)
    ci = _iota((CHUNK, CHUNK), 1)
    neg_a = -jnp.exp(par_ref[0:1, :])
    dt_bias = par_ref[1:2, :]

    def load(c):
        r0 = pl.multiple_of(c * CHUNK, CHUNK)
        return (qs[pl.ds(r0, CHUNK), :], ks[pl.ds(r0, CHUNK), :], vs[pl.ds(r0, CHUNK), :],
                g_ref[0, pl.ds(r0, CHUNK), :])

    def gate_terms(loaded, d):
        q, k, v, gates = loaded
        beta = _lane_col(jax.nn.sigmoid(gates), d * GDN_HEADS + h)
        g_t = neg_a * _softplus(gates + dt_bias)
        gc_t = _cumsum_rows(g_t)
        tot_t = jnp.broadcast_to(gc_t[CHUNK - 1:CHUNK, :], gc_t.shape)
        if d == 1:
            gc_t = tot_t - gc_t + g_t
        gidx = 2 * GDN_HEADS + d * GDN_HEADS + h
        gc = _lane_col(gc_t, gidx)
        tot = _lane_col(tot_t, gidx)
        incl = (ri >= ci) if d == 0 else (ri <= ci)
        decay = jnp.exp(jnp.where(incl, gc - _row_form(gc), NEG))
        egc = jnp.exp(gc)
        kb = k * beta
        return dict(q=q, k=k, kb=kb, incl=incl, strict=(ri > ci) if d == 0 else (ri < ci), decay=decay,
                    x=jnp.concatenate([v * beta, kb * egc], axis=-1), q_dec=q * egc,
                    k_dec=k * jnp.exp(tot - gc), etot=jnp.exp(tot))

    unroll = min(GDN_UNROLL, nchunks)
    nsteps = nchunks // unroll

    def chunk_ids(i):
        cf = [i * unroll + j for j in range(unroll)]
        return cf, [nchunks - 1 - c for c in cf]

    def solve(i):
        cf, cb = chunk_ids(i)
        t = [gate_terms(load(c), d) for c, d in zip(cf + cb, [0] * unroll + [1] * unroll)]
        kk = [_dot_nt(ch["kb"], ch["k"]) for ch in t]
        qk = [_dot_nt(ch["q"], ch["k"]) for ch in t]
        yield
        a = [jnp.where(ch["strict"], m * ch["decay"], 0.0) for ch, m in zip(t, kk)]
        qk = [jnp.where(ch["incl"], m * ch["decay"], 0.0).astype(BF16) for ch, m in zip(t, qk)]
        inv = yield from _unit_tri_inverse_stages(a, ri, ci)
        x = _dots_many(inv, [ch["x"] for ch in t], TRI_PASSES[2])
        yield
        return tuple((x[n][:, :HEAD], x[n][:, HEAD:].astype(BF16), qk[n], t[n]["q_dec"].astype(BF16),
                      jnp.transpose(t[n]["k_dec"]).astype(BF16), jnp.broadcast_to(t[n]["etot"], (CHUNK, HEAD)))
                     for n in range(2 * unroll))

    def scan(i, pre, sf, sb):
        cf, cb = chunk_ids(i)
        outs = []
        for j in range(unroll):
            chains = (pre[j], pre[unroll + j])
            sbf = [s.astype(BF16) for s in (sf, sb)]
            ws = [jnp.dot(ch[1], s, preferred_element_type=F32) for ch, s in zip(chains, sbf)]
            o1 = [jnp.dot(ch[3], s, preferred_element_type=F32) for ch, s in zip(chains, sbf)]
            yield
            v_new = [(ch[0] - w).astype(BF16) for ch, w in zip(chains, ws)]
            o2 = [jnp.dot(ch[2], vn, preferred_element_type=F32) for ch, vn in zip(chains, v_new)]
            kv = [jnp.dot(ch[4], vn, preferred_element_type=F32) for ch, vn in zip(chains, v_new)]
            yield
            sf, sb = [s * ch[5] + m for ch, s, m in zip(chains, (sf, sb), kv)]
            outs.append([a1 + a2 for a1, a2 in zip(o1, o2)])
        for j in range(unroll):
            oacc[0, pl.ds(pl.multiple_of(cf[j] * CHUNK, CHUNK), CHUNK), :] = outs[j][0]
            oacc[1, pl.ds(pl.multiple_of(cb[j] * CHUNK, CHUNK), CHUNK), :] = outs[j][1]
        return sf, sb

    def step(i, carry):
        pre, sf, sb = carry
        nxt, (sf, sb) = _run_interleaved(solve(i + 1), scan(i, pre, sf, sb))
        return nxt, sf, sb

    pre, _ = _run_interleaved(solve(0), iter(()))
    pre, sf, sb = lax.fori_loop(0, nsteps - 1, step, (pre, s0_ref[0, 0, 0], s0_ref[0, 1, 0]))
    _, (sf, sb) = _run_interleaved(iter(()), scan(nsteps - 1, pre, sf, sb))
    sfin_ref[0, 0, 0] = sf
    sfin_ref[0, 1, 0] = sb

    def fin(c, _):
        r0 = pl.multiple_of(c * CHUNK, CHUNK)
        o = oacc[0, pl.ds(r0, CHUNK), :] + oacc[1, pl.ds(r0, CHUNK), :]
        y = _rms(o, nrm_ref[...]) * _silu(z_ref[0, pl.ds(r0, CHUNK), :])
        o_ref[0, pl.ds(r0, CHUNK), :] = y.astype(o_ref.dtype)
        return 0

    lax.fori_loop(0, nchunks, fin, 0, unroll=2)


def _gdn(x, gates, conv_w, par, norm_w, s0, layer):
    bsz, seq_len, _ = x.shape
    col = lambda off: pl.BlockSpec((1, seq_len, HEAD), lambda b, h: (b, 0, off + h))
    cw = lambda off: pl.BlockSpec((3, HEAD), lambda b, h: (0, off + h))
    st = pl.BlockSpec((1, 2, 1, HEAD, HEAD), lambda b, h: (b, 0, h, 0, 0))
    st_in = pl.BlockSpec((1, None, 2, 1, HEAD, HEAD), lambda b, h: (b, layer, 0, h, 0, 0))
    return pl.pallas_call(
        _gdn_kernel,
        grid=(bsz, GDN_HEADS),
        in_specs=[col(0), col(GDN_HEADS), col(2 * GDN_HEADS), col(3 * GDN_HEADS),
                  pl.BlockSpec((1, seq_len, HEAD), lambda b, h: (b, 0, 0)),
                  cw(0), cw(GDN_HEADS), cw(2 * GDN_HEADS),
                  pl.BlockSpec((8, HEAD), lambda b, h: (0, 0)),
                  pl.BlockSpec((1, HEAD), lambda b, h: (0, 0)),
                  st_in],
        out_specs=[pl.BlockSpec((1, seq_len, HEAD), lambda b, h: (b, 0, h)), st],
        out_shape=[jax.ShapeDtypeStruct((bsz, seq_len, GDN_HEADS * HEAD), BF16),
                   jax.ShapeDtypeStruct((bsz, 2, GDN_HEADS, HEAD, HEAD), F32)],
        scratch_shapes=[pltpu.VMEM((seq_len, HEAD), F32)] * 3 + [pltpu.VMEM((2, seq_len, HEAD), F32)],
        compiler_params=_cparams("parallel", "parallel"),
        name="gdn",
    )(x, x, x, x, gates, conv_w, conv_w, conv_w, par, norm_w.reshape(1, HEAD), s0)


def _ssd_kernel(x_ref, bm_ref, cm_ref, z_ref, dt_ref, cwx_ref, cwb_ref, cwc_ref, cbx_ref, cbb_ref, cbc_ref,
                par_ref, dsk_ref, h0_ref, y_ref, hfin_ref, xs, bs, cs):
    seq_len = x_ref.shape[1]
    nchunks = seq_len // CHUNK
    width = SSD_HPG * SSD_P

    def prep(c, _):
        r0 = pl.multiple_of(c * CHUNK, CHUNK)
        for src, cw, cb, dst in ((x_ref, cwx_ref, cbx_ref, xs), (bm_ref, cwb_ref, cbb_ref, bs),
                                 (cm_ref, cwc_ref, cbc_ref, cs)):
            dst[pl.ds(r0, CHUNK), :] = _silu(_conv3(src, r0, c, nchunks, seq_len, cw, lead=(0,)) + cb[...])
        return 0

    lax.fori_loop(0, nchunks, prep, 0)

    ri = _iota((CHUNK, CHUNK), 0)
    ci = _iota((CHUNK, CHUNK), 1)
    head_of_lane = jnp.right_shift(_iota((CHUNK, width), 1), 6)
    neg_a = -jnp.exp(par_ref[0:1, :])
    dt_bias = par_ref[1:2, :]

    def expand(cols):
        out = jnp.broadcast_to(cols[0], (CHUNK, width))
        for e in range(1, SSD_HPG):
            out = jnp.where(head_of_lane == e, cols[e], out)
        return out

    def gate_terms(c, d):
        r0 = pl.multiple_of(c * CHUNK, CHUNK)
        x, bm, cm = xs[pl.ds(r0, CHUNK), :], bs[pl.ds(r0, CHUNK), :], cs[pl.ds(r0, CHUNK), :]
        dt_t = _softplus(dt_ref[0, pl.ds(r0, CHUNK), :] + dt_bias)
        la_t = dt_t * neg_a
        cs_t = _cumsum_rows(la_t)
        tot_row = cs_t[CHUNK - 1:CHUNK, :]
        tot_t = jnp.broadcast_to(tot_row, cs_t.shape)
        if d == 1:
            cs_t = tot_t - cs_t + la_t
        incl = (ri >= ci) if d == 0 else (ri <= ci)
        cs_rows = jnp.transpose(cs_t)
        rest_t = tot_t - cs_t
        tot_p = jnp.broadcast_to(tot_row, (SSD_P, HEAD))
        lanes = [d * SSD_HPG + e for e in range(SSD_HPG)]
        col = _lane_col
        lmats = [jnp.exp(jnp.where(incl, col(cs_t, k) - cs_rows[k:k + 1, :], NEG)) for k in lanes]
        xdt = x * expand([col(dt_t, k) for k in lanes])
        cs_full = expand([col(cs_t, k) for k in lanes])
        cd = jnp.concatenate([jnp.broadcast_to(jnp.exp(col(tot_p, k)), (SSD_P, HEAD)) for k in lanes], axis=0)
        return dict(r0=r0, x=x, bm=bm, cm=cm, xdt=xdt, lmats=lmats, ecs=jnp.exp(cs_full), cd=cd,
                    xdec_t=jnp.transpose(xdt * jnp.exp(expand([col(rest_t, k) for k in lanes]))))

    unroll = min(SSD_UNROLL, nchunks)

    def direction(d, h0, finish):
        def step(i, hstate):
            cidx = [i * unroll + j for j in range(unroll)]
            if d == 1:
                cidx = [nchunks - 1 - c for c in cidx]
            t = [gate_terms(c, d) for c in cidx]
            cb = [_dot_nt(ch["cm"], ch["bm"]) for ch in t]
            states = [_dot(ch["xdec_t"], ch["bm"]) for ch in t]
            ys = []
            for ch, m in zip(t, cb):
                y = None
                for e in range(SSD_HPG):
                    part = _dot(m * ch["lmats"][e], jnp.where(head_of_lane == e, ch["xdt"], 0.0))
                    y = part if y is None else y + part
                ys.append(y)
            for j, ch in enumerate(t):
                ys[j] = ys[j] + _dot_nt(ch["cm"], hstate) * ch["ecs"]
                hstate = hstate * ch["cd"] + states[j]
            for ch, y in zip(t, ys):
                finish(ch, y)
            return hstate

        return lax.fori_loop(0, nchunks // unroll, step, h0.reshape(width, HEAD))

    def store_fwd(ch, y):
        y_ref[0, pl.ds(ch["r0"], CHUNK), :] = y

    def store_bwd(ch, y):
        tot = y_ref[0, pl.ds(ch["r0"], CHUNK), :] + y + ch["x"] * dsk_ref[...]
        y_ref[0, pl.ds(ch["r0"], CHUNK), :] = tot * _silu(z_ref[0, pl.ds(ch["r0"], CHUNK), :])

    hfin_ref[0, 0] = direction(0, h0_ref[0, 0], store_fwd).reshape(SSD_HPG, SSD_P, HEAD)
    hfin_ref[0, 1] = direction(1, h0_ref[0, 1], store_bwd).reshape(SSD_HPG, SSD_P, HEAD)


def _ssd(x, gates, conv_w, conv_b, par, d_skip, h0, layer):
    bsz, seq_len, _ = x.shape
    width = SSD_HPG * SSD_P
    spec = lambda wd, off: pl.BlockSpec((1, seq_len, wd), lambda b, g: (b, 0, off + g))
    cws = lambda rows, wd, off: pl.BlockSpec((rows, wd), lambda b, g: (0, off + g))
    st = pl.BlockSpec((1, 2, SSD_HPG, SSD_P, HEAD), lambda b, g: (b, 0, g, 0, 0))
    st_in = pl.BlockSpec((1, None, 2, SSD_HPG, SSD_P, HEAD), lambda b, g: (b, layer, 0, g, 0, 0))
    return pl.pallas_call(
        _ssd_kernel,
        grid=(bsz, SSD_GROUPS),
        in_specs=[spec(width, 0), spec(HEAD, 4), spec(HEAD, 6), spec(width, 4),
                  pl.BlockSpec((1, seq_len, HEAD), lambda b, g: (b, 0, g)),
                  cws(3, width, 0), cws(3, HEAD, 4), cws(3, HEAD, 6),
                  cws(1, width, 0), cws(1, HEAD, 4), cws(1, HEAD, 6),
                  pl.BlockSpec((8, HEAD), lambda b, g: (0, g)),
                  cws(1, width, 0),
                  st_in],
        out_specs=[pl.BlockSpec((1, seq_len, width), lambda b, g: (b, 0, g)), st],
        out_shape=[jax.ShapeDtypeStruct((bsz, seq_len, SSD_HEADS * SSD_P), F32),
                   jax.ShapeDtypeStruct((bsz, 2, SSD_HEADS, SSD_P, HEAD), F32)],
        scratch_shapes=[pltpu.VMEM((seq_len, width), F32), pltpu.VMEM((seq_len, HEAD), F32),
                        pltpu.VMEM((seq_len, HEAD), F32)],
        compiler_params=_cparams("parallel", "parallel"),
        name="ssd",
    )(x, x, x, x, gates, conv_w, conv_w, conv_w, conv_b, conv_b, conv_b, par, d_skip, h0)


def _rope_tables(n_tokens):
    half = HEAD // 2
    inv_freq = ROPE_THETA ** (-jnp.arange(0, half, 2, dtype=F32) / half)
    t = jnp.arange(n_tokens)
    ang_r = (t // GRID_W).astype(F32)[:, None] * inv_freq
    ang_c = (t % GRID_W).astype(F32)[:, None] * inv_freq
    cos = jnp.concatenate([jnp.cos(ang_r)] * 2 + [jnp.cos(ang_c)] * 2, axis=-1)
    sin = jnp.concatenate([-jnp.sin(ang_r), jnp.sin(ang_r), -jnp.sin(ang_c), jnp.sin(ang_c)], axis=-1)
    return cos, sin


def _gate_rows(a_log, dt_bias, offset):
    n = a_log.size
    rows = jnp.zeros((8, HEAD), F32)
    rows = rows.at[0, offset:offset + n].set(a_log.reshape(-1).astype(F32))
    return rows.at[1, offset:offset + n].set(dt_bias.reshape(-1).astype(F32))


def _ssd_group_lanes(t):
    lead = t.shape[:-1]
    t = t.reshape(lead + (2, SSD_GROUPS, SSD_HPG))
    t = jnp.swapaxes(t, -3, -2).reshape(lead + (SSD_GROUPS, 2 * SSD_HPG))
    pad = [(0, 0)] * (t.ndim - 1) + [(0, HEAD - 2 * SSD_HPG)]
    return jnp.pad(t, pad).reshape(lead + (SSD_GROUPS * HEAD,))


def _pad_cols(w, width):
    return jnp.pad(w, ((0, 0), (0, width - w.shape[1])))


def kernel(x_prompt, x_sample, state_gdn, cache_gqa_k, cache_gqa_v, state_ssd, cache_na_k, cache_na_v, c, c_ctx, ada_w, ada_b, norm_mix_pre, norm_mix_post, norm_mlp_pre, norm_mlp_post, mlp_w1, mlp_w2, ev_w_in, ev_w_out, gdn_conv, gdn_a_log, gdn_dt_bias, gdn_norm, gqa_q_norm, gqa_k_norm, od_w_in, od_w_out, ssd_conv, ssd_conv_b, ssd_a_log, ssd_dt_bias, ssd_d, ssd_norm, na_rpb):
    nb, ns, _ = x_prompt.shape
    db, dl, _ = x_sample.shape
    past = cache_gqa_k.shape[2]

    c16 = jnp.zeros((16, D_MODEL), F32).at[:db].set(c).at[db].set(c_ctx)
    mods = _ada(c16, ada_w, ada_b).reshape(DEPTH, 16, 6, D_MODEL)

    cos_l, sin_l = _rope_tables(dl)
    cos_c, sin_c = jnp.ones((ns, HEAD), F32), jnp.zeros((ns, HEAD), F32)

    xp = x_prompt.reshape(1, nb * ns, D_MODEL)
    xs = x_sample
    tm_c = min(1024, nb * ns)
    tm_in = min(512, nb * ns, dl)
    zero_gdn = jnp.zeros((nb, 1, 2, GDN_HEADS, HEAD, HEAD), F32)
    zero_ssd = jnp.zeros((nb, 1, 2, SSD_HEADS, SSD_P, HEAD), F32)
    new_gdn, new_gk, new_gv, new_ssd, new_nk, new_nv = [], [], [], [], [], []

    for i in range(DEPTH):
        j = i // 2
        mod_l, mod_c = mods[i, :db], mods[i, db:db + 1]
        if i % 2 == 0:
            wi = ev_w_in[j]
            w_in = jnp.concatenate([wi[:, :2048], wi[:, 2064:], _pad_cols(wi[:, 2048:2064], HEAD)], axis=1).astype(BF16)
            widths = (2048, 1024, HEAD)
            w_out = ev_w_out[j].astype(BF16)
            par = _gate_rows(gdn_a_log[j], gdn_dt_bias[j], 2 * GDN_HEADS)

            gx, ax, gt = _inproj(xp, mod_c, norm_mix_pre[i], w_in, widths, tm_in)
            gx, ax, gt = (t.reshape(nb, ns, -1) for t in (gx, ax, gt))
            o_a, s_a = _gdn(gx, gt, gdn_conv[j], par, gdn_norm[j], zero_gdn, 0)
            q, k, v, kf = _qkprep(ax, cos_c, sin_c, gqa_q_norm[j], gqa_k_norm[j], ns, True)
            o_b = _flash(q, k, v, GQA_KV_HEADS, 2, ns, ns, True)
            yc = _outproj(o_a.reshape(1, nb * ns, -1), o_b.reshape(1, nb * ns, -1), xp, mod_c,
                          w_out, norm_mix_post[i], tm_c)
            new_gdn.append(s_a)
            new_gk.append(kf.reshape(nb, ns, GQA_KV_HEADS, HEAD))
            new_gv.append(ax[..., 768:].reshape(nb, ns, GQA_KV_HEADS, HEAD))

            gx, ax, gt = _inproj(xs, mod_l, norm_mix_pre[i], w_in, widths, tm_in)
            o_a, _ = _gdn(gx, gt, gdn_conv[j], par, gdn_norm[j], state_gdn, j)
            q, k, v = _qkprep(ax, cos_l, sin_l, gqa_q_norm[j], gqa_k_norm[j], 512, False)
            o_b = _flash(q, k, v, GQA_KV_HEADS, 2, 512, min(2048, dl), True,
                         cache=(cache_gqa_k.reshape(db, -1, past, GQA_KV_HEADS * HEAD),
                                cache_gqa_v.reshape(db, -1, past, GQA_KV_HEADS * HEAD), j))
            yl = _outproj(o_a, o_b, xs, mod_l, w_out, norm_mix_post[i], 512)
        else:
            wi = od_w_in[j]
            w_in = jnp.concatenate([wi[:, 512:1536], wi[:, :512], wi[:, 1552:], _ssd_group_lanes(wi[:, 1536:1552])], axis=1).astype(BF16)
            widths = (1536, 1536, SSD_GROUPS * HEAD)
            w_out = od_w_out[j].astype(BF16)
            par = jnp.zeros((8, SSD_GROUPS * HEAD), F32)
            par = par.at[0].set(_ssd_group_lanes(ssd_a_log[j].reshape(-1).astype(F32)))
            par = par.at[1].set(_ssd_group_lanes(ssd_dt_bias[j].reshape(-1).astype(F32)))
            dsk = jnp.repeat(ssd_d[j].astype(F32), SSD_P).reshape(1, -1)
            cvw, cvb = ssd_conv[j], ssd_conv_b[j].reshape(1, -1)

            sx, ax, gt = _inproj(xp, mod_c, norm_mix_pre[i], w_in, widths, tm_in)
            sx, ax, gt = (t.reshape(nb, ns, -1) for t in (sx, ax, gt))
            y_c, s_c = _ssd(sx, gt, cvw, cvb, par, dsk, zero_ssd, 0)
            o_d = _flash(ax, ax, ax, NA_HEADS, 1, ns, ns, False, (0, NA_HEADS, 2 * NA_HEADS))
            yc = _outproj(y_c.reshape(1, nb * ns, -1), o_d.reshape(1, nb * ns, -1), xp, mod_c,
                          w_out, norm_mix_post[i], tm_c, ssd_norm[j])
            new_ssd.append(s_c)
            new_nk.append(ax[..., 512:1024].reshape(nb, ns, NA_HEADS, HEAD))
            new_nv.append(ax[..., 1024:].reshape(nb, ns, NA_HEADS, HEAD))

            sx, ax, gt = _inproj(xs, mod_l, norm_mix_pre[i], w_in, widths, tm_in)
            y_c, _ = _ssd(sx, gt, cvw, cvb, par, dsk, state_ssd, j)
            o_d = _natten(ax, cache_na_k.reshape(db, -1, past, NA_HEADS * HEAD),
                          cache_na_v.reshape(db, -1, past, NA_HEADS * HEAD), j, _natten_bias(na_rpb[j]))
            yl = _outproj(y_c, o_d, xs, mod_l, w_out, norm_mix_post[i], 512, ssd_norm[j])
        xp, xs = yc, yl
        w1, w2 = mlp_w1[i].astype(BF16), mlp_w2[i].astype(BF16)
        xp = _mlp(xp, mod_c, norm_mlp_pre[i], w1, w2, norm_mlp_post[i], tm_c, 1024)
        xs = _mlp(xs, mod_l, norm_mlp_pre[i], w1, w2, norm_mlp_post[i], 1024, 1024)

    return (xp.reshape(nb, ns, D_MODEL), xs, jnp.stack(new_gdn, axis=1), jnp.stack(new_gk, axis=1),
            jnp.stack(new_gv, axis=1), jnp.stack(new_ssd, axis=1), jnp.stack(new_nk, axis=1),
            jnp.stack(new_nv, axis=1))
```

```python
import functools

import numpy as np
import jax
import jax.numpy as jnp
from jax import lax
from jax.experimental import pallas as pl
from jax.experimental.pallas import tpu as pltpu

F32 = jnp.float32
BF16 = jnp.bfloat16

D_MODEL = 1024
DEPTH = 4
GRID_W = 64
EPS = 1e-6
ROPE_THETA = 10000.0
D_FF = 4 * D_MODEL
HEAD = 128
GDN_HEADS = 4
GQA_HEADS = 4
GQA_KV_HEADS = 2
SSD_HEADS = 8
SSD_P = 64
SSD_GROUPS = 2
SSD_HPG = SSD_HEADS // SSD_GROUPS
NA_HEADS = 4
NA_WIN_R = 8
NA_WIN_C = 16
CHUNK = 128
GDN_UNROLL = 4
SSD_UNROLL = 4
TRI_PASSES = (1, 1, 2)
NEG = -1e30
QK_SCALE_LOG2 = HEAD ** -0.5 * float(np.log2(np.e))
FLASH_SAFE_BOUND = 60.0
FLASH_SHIFT = 64.0
VMEM_LIMIT_BYTES = 56 * 1024 * 1024


def _cparams(*sem):
    return pltpu.CompilerParams(dimension_semantics=sem, vmem_limit_bytes=VMEM_LIMIT_BYTES)


def _dot(a, b):
    return jnp.dot(a.astype(BF16), b.astype(BF16), preferred_element_type=F32)


def _dot_nt(a, b):
    return lax.dot_general(a.astype(BF16), b.astype(BF16), (((1,), (1,)), ((), ())),
                           preferred_element_type=F32)


def _split_bf16(a):
    hi = a.astype(BF16)
    lo = (a - hi.astype(F32)).astype(BF16)
    return hi, lo


def _dot3(a, b):
    ah, al = _split_bf16(a)
    bh, bl = _split_bf16(b)
    d = lambda x, y: jnp.dot(x, y, preferred_element_type=F32)
    return d(ah, bh) + (d(ah, bl) + d(al, bh))


def _silu(x):
    return x * jax.nn.sigmoid(x)


def _softplus(x):
    return jnp.maximum(x, 0.0) + jnp.log(1.0 + jnp.exp(-jnp.abs(x)))


def _rms(x, w):
    return x * lax.rsqrt(jnp.mean(x * x, axis=-1, keepdims=True) + EPS) * w


def _iota(shape, axis):
    return lax.broadcasted_iota(jnp.int32, shape, axis)


def _lane_col(tile, idx):
    return jnp.sum(jnp.where(_iota(tile.shape, 1) == idx, tile, 0.0), axis=-1, keepdims=True)


def _cumsum_rows(t):
    rows = _iota(t.shape, 0)
    s = 1
    while s < t.shape[0]:
        t = t + jnp.where(rows >= s, pltpu.roll(t, s, 0), 0.0)
        s *= 2
    return t


def _row_form(col):
    n = col.shape[0]
    return jnp.transpose(jnp.broadcast_to(col, (n, n)))


def _dot3_many(xs, ys):
    xs = [_split_bf16(x) for x in xs]
    ys = [_split_bf16(y) for y in ys]
    d = lambda x, y: jnp.dot(x, y, preferred_element_type=F32)
    hh = [d(x[0], y[0]) for x, y in zip(xs, ys)]
    hl = [d(x[0], y[1]) for x, y in zip(xs, ys)]
    lh = [d(x[1], y[0]) for x, y in zip(xs, ys)]
    return [a + (b + c) for a, b, c in zip(hh, hl, lh)]


def _dots_many(xs, ys, passes):
    if passes == 3:
        return _dot3_many(xs, ys)
    if passes == 2:
        d = lambda x, y: jnp.dot(x, y, preferred_element_type=F32)
        xs = [x.astype(BF16) for x in xs]
        ys = [_split_bf16(y) for y in ys]
        hi = [d(x, y[0]) for x, y in zip(xs, ys)]
        lo = [d(x, y[1]) for x, y in zip(xs, ys)]
        return [a + b for a, b in zip(hi, lo)]
    return [_dot(x, y) for x, y in zip(xs, ys)]


def _unit_tri_inverse_stages(a_list, ri, ci):
    same = lambda sh: jnp.right_shift(ri, sh) == jnp.right_shift(ci, sh)
    eye = jnp.where(ri == ci, 1.0, 0.0)
    p = [jnp.where(same(4), a, 0.0) for a in a_list]
    m = [eye - x for x in p]
    p = _dots_many(p, p, TRI_PASSES[0])
    yield
    for last in (False, False, True):
        mp = _dots_many(m, p, TRI_PASSES[0])
        if not last:
            p = _dots_many(p, p, TRI_PASSES[0])
        yield
        m = [x + y for x, y in zip(m, mp)]
    for sh in (4, 5, 6):
        off = [jnp.where(same(sh + 1), jnp.where(same(sh), 0.0, a), 0.0) for a in a_list]
        t = _dots_many(off, m, TRI_PASSES[1])
        yield
        mt = _dots_many(m, t, TRI_PASSES[1])
        yield
        m = [x - y for x, y in zip(m, mt)]
    return m


def _run_interleaved(gen_a, gen_b):
    results, live = [None, None], [gen_a, gen_b]
    while any(g is not None for g in live):
        for n, g in enumerate(live):
            if g is None:
                continue
            try:
                next(g)
            except StopIteration as stop:
                results[n], live[n] = stop.value, None
    return results


def _conv3(ref, r0, c, nchunks, seq_len, w, lead=()):
    x = ref[lead + (pl.ds(r0, CHUNK), slice(None))]
    prev = ref[lead + (pl.ds(jnp.maximum(r0 - 1, 0), 1), slice(None))]
    nxt = ref[lead + (pl.ds(jnp.minimum(r0 + CHUNK, seq_len - 1), 1), slice(None))]
    prev = jnp.where(c == 0, 0.0, prev)
    nxt = jnp.where(c == nchunks - 1, 0.0, nxt)
    rows = _iota(x.shape, 0)
    xm = jnp.where(rows == 0, prev, pltpu.roll(x, 1, 0))
    xp = jnp.where(rows == CHUNK - 1, nxt, pltpu.roll(x, CHUNK - 1, 0))
    return xm * w[0:1, :] + x * w[1:2, :] + xp * w[2:3, :]


def _ada_kernel(c_ref, w_ref, b_ref, o_ref):
    o_ref[0] = _dot(_silu(c_ref[...]), w_ref[0]) + b_ref[0]


def _ada(c16, ada_w, ada_b):
    tn = 1536
    n = ada_w.shape[-1]
    return pl.pallas_call(
        _ada_kernel,
        grid=(DEPTH, n // tn),
        in_specs=[pl.BlockSpec((16, D_MODEL), lambda i, j: (0, 0)),
                  pl.BlockSpec((1, D_MODEL, tn), lambda i, j: (i, 0, j)),
                  pl.BlockSpec((1, 1, tn), lambda i, j: (i, 0, j))],
        out_specs=pl.BlockSpec((1, 16, tn), lambda i, j: (i, 0, j)),
        out_shape=jax.ShapeDtypeStruct((DEPTH, 16, n), F32),
        compiler_params=_cparams("parallel", "parallel"),
        name="ada",
    )(c16, ada_w, ada_b.reshape(DEPTH, 1, n))


def _mod_spec(mod):
    if mod.shape[0] == 1:
        return pl.BlockSpec((1, 6, D_MODEL), lambda b, t: (0, 0, 0))
    return pl.BlockSpec((1, 6, D_MODEL), lambda b, t: (b, 0, 0))


def _inproj_kernel(x_ref, mod_ref, nw_ref, w_ref, *o_refs, splits):
    h = _rms(x_ref[0], nw_ref[...])
    h = h * (1.0 + mod_ref[0, 1:2, :]) + mod_ref[0, 0:1, :]
    hb = h.astype(BF16)
    for o_ref, (a, b) in zip(o_refs, splits):
        o_ref[0] = jnp.dot(hb, w_ref[:, a:b], preferred_element_type=F32)


def _inproj(x, mod, norm_w, w, widths, tm):
    bsz, seq_len, _ = x.shape
    splits, a = [], 0
    for wd in widths:
        splits.append((a, a + wd))
        a += wd
    return pl.pallas_call(
        functools.partial(_inproj_kernel, splits=tuple(splits)),
        grid=(bsz, seq_len // tm),
        in_specs=[pl.BlockSpec((1, tm, D_MODEL), lambda b, t: (b, t, 0)),
                  _mod_spec(mod),
                  pl.BlockSpec((1, D_MODEL), lambda b, t: (0, 0)),
                  pl.BlockSpec(w.shape, lambda b, t: (0, 0))],
        out_specs=[pl.BlockSpec((1, tm, wd), lambda b, t: (b, t, 0)) for wd in widths],
        out_shape=[jax.ShapeDtypeStruct((bsz, seq_len, wd), F32) for wd in widths],
        compiler_params=_cparams("parallel", "parallel"),
        name="inproj",
    )(x, mod, norm_w.reshape(1, D_MODEL), w)


def _outproj_kernel(a_ref, b_ref, x_ref, mod_ref, w_ref, nw_ref, *rest, norm_a):
    a = a_ref[0]
    if norm_a:
        na_ref, o_ref = rest
        a = _rms(a, na_ref[...])
    else:
        (o_ref,) = rest
    half = a.shape[-1]
    y = (jnp.dot(a.astype(BF16), w_ref[:half, :], preferred_element_type=F32)
         + jnp.dot(b_ref[0].astype(BF16), w_ref[half:, :], preferred_element_type=F32))
    o_ref[0] = x_ref[0] + mod_ref[0, 2:3, :] * _rms(y, nw_ref[...])


def _outproj(a, b, x, mod, w, norm_w, tm, norm_a_w=None):
    bsz, seq_len, _ = x.shape
    half = a.shape[-1]
    in_specs = [pl.BlockSpec((1, tm, half), lambda bb, t: (bb, t, 0)),
                pl.BlockSpec((1, tm, half), lambda bb, t: (bb, t, 0)),
                pl.BlockSpec((1, tm, D_MODEL), lambda bb, t: (bb, t, 0)),
                _mod_spec(mod),
                pl.BlockSpec(w.shape, lambda bb, t: (0, 0)),
                pl.BlockSpec((1, D_MODEL), lambda bb, t: (0, 0))]
    args = [a, b, x, mod, w, norm_w.reshape(1, D_MODEL)]
    if norm_a_w is not None:
        in_specs.append(pl.BlockSpec((1, half), lambda bb, t: (0, 0)))
        args.append(norm_a_w.reshape(1, half))
    return pl.pallas_call(
        functools.partial(_outproj_kernel, norm_a=norm_a_w is not None),
        grid=(bsz, seq_len // tm),
        in_specs=in_specs,
        out_specs=pl.BlockSpec((1, tm, D_MODEL), lambda bb, t: (bb, t, 0)),
        out_shape=jax.ShapeDtypeStruct(x.shape, F32),
        compiler_params=_cparams("parallel", "parallel"),
        name="outproj",
    )(*args)


def _mlp_kernel(x_ref, mod_ref, npre_ref, w1_ref, w2_ref, npost_ref, o_ref, h_ref, acc_ref):
    f = pl.program_id(2)

    @pl.when(f == 0)
    def _():
        h = _rms(x_ref[0], npre_ref[...])
        h = h * (1.0 + mod_ref[0, 4:5, :]) + mod_ref[0, 3:4, :]
        h_ref[...] = h.astype(BF16)
        acc_ref[...] = jnp.zeros_like(acc_ref)

    u = jnp.maximum(jnp.dot(h_ref[...], w1_ref[...], preferred_element_type=F32), 0.0)
    acc_ref[...] += jnp.dot((u * u).astype(BF16), w2_ref[...], preferred_element_type=F32)

    @pl.when(f == pl.num_programs(2) - 1)
    def _():
        o_ref[0] = x_ref[0] + mod_ref[0, 5:6, :] * _rms(acc_ref[...], npost_ref[...])


def _mlp(x, mod, npre, w1, w2, npost, tm, tf):
    bsz, seq_len, _ = x.shape
    mspec = _mod_spec(mod)
    return pl.pallas_call(
        _mlp_kernel,
        grid=(bsz, seq_len // tm, D_FF // tf),
        in_specs=[pl.BlockSpec((1, tm, D_MODEL), lambda b, t, f: (b, t, 0)),
                  pl.BlockSpec((1, 6, D_MODEL), lambda b, t, f: mspec.index_map(b, t)),
                  pl.BlockSpec((1, D_MODEL), lambda b, t, f: (0, 0)),
                  pl.BlockSpec((D_MODEL, tf), lambda b, t, f: (0, f)),
                  pl.BlockSpec((tf, D_MODEL), lambda b, t, f: (f, 0)),
                  pl.BlockSpec((1, D_MODEL), lambda b, t, f: (0, 0))],
        out_specs=pl.BlockSpec((1, tm, D_MODEL), lambda b, t, f: (b, t, 0)),
        out_shape=jax.ShapeDtypeStruct(x.shape, F32),
        scratch_shapes=[pltpu.VMEM((tm, D_MODEL), BF16), pltpu.VMEM((tm, D_MODEL), F32)],
        compiler_params=_cparams("parallel", "parallel", "arbitrary"),
        name="mlp",
    )(x, mod, npre.reshape(1, D_MODEL), w1, w2, npost.reshape(1, D_MODEL))


def _qkprep_kernel(x_ref, cos_ref, sin_ref, qn_ref, kn_ref, q_ref, k_ref, v_ref, kf_ref=None):
    cos, sin = cos_ref[...], sin_ref[...]
    lane = _iota(cos.shape, 1)
    first = (lane & 32) == 0

    def prep(xh, w):
        y = _rms(xh, w)
        swapped = jnp.where(first, pltpu.roll(y, HEAD - 32, 1), pltpu.roll(y, 32, 1))
        return y, y * cos + swapped * sin

    nq = GQA_HEADS * HEAD
    for hd in range(GQA_HEADS):
        _, r = prep(x_ref[0, :, hd * HEAD:(hd + 1) * HEAD], qn_ref[...])
        q_ref[0, :, hd * HEAD:(hd + 1) * HEAD] = (r * QK_SCALE_LOG2).astype(BF16)
    for hd in range(GQA_KV_HEADS):
        y, r = prep(x_ref[0, :, nq + hd * HEAD:nq + (hd + 1) * HEAD], kn_ref[...])
        k_ref[0, :, hd * HEAD:(hd + 1) * HEAD] = r.astype(BF16)
        if kf_ref is not None:
            kf_ref[0, :, hd * HEAD:(hd + 1) * HEAD] = y
    nk = GQA_KV_HEADS * HEAD
    v_ref[0] = x_ref[0, :, nq + nk:nq + 2 * nk].astype(BF16)


def _qkprep(x, cos, sin, q_norm, k_norm, tm, want_plain_k):
    bsz, seq_len, width = x.shape
    nq, nk = GQA_HEADS * HEAD, GQA_KV_HEADS * HEAD
    row = lambda wd: pl.BlockSpec((1, tm, wd), lambda b, t: (b, t, 0))
    n_out = 4 if want_plain_k else 3
    return pl.pallas_call(
        _qkprep_kernel,
        grid=(bsz, seq_len // tm),
        in_specs=[row(width),
                  pl.BlockSpec((tm, HEAD), lambda b, t: (t, 0)),
                  pl.BlockSpec((tm, HEAD), lambda b, t: (t, 0)),
                  pl.BlockSpec((1, HEAD), lambda b, t: (0, 0)),
                  pl.BlockSpec((1, HEAD), lambda b, t: (0, 0))],
        out_specs=[row(nq), row(nk), row(nk), row(nk)][:n_out],
        out_shape=[jax.ShapeDtypeStruct((bsz, seq_len, nq), BF16),
                   jax.ShapeDtypeStruct((bsz, seq_len, nk), BF16),
                   jax.ShapeDtypeStruct((bsz, seq_len, nk), BF16),
                   jax.ShapeDtypeStruct((bsz, seq_len, nk), F32)][:n_out],
        compiler_params=_cparams("parallel", "parallel"),
        name="qkprep",
    )(x, cos, sin, q_norm.reshape(1, HEAD), k_norm.reshape(1, HEAD))


def _flash_kernel(q_ref, k_ref, v_ref, *rest, rep, tk, prescaled, with_cache):
    if with_cache:
        kc_ref, vc_ref, o_ref, knorm_ref = rest
    else:
        o_ref, knorm_ref = rest
    tq = q_ref.shape[1]
    if rep == 2:
        qs = [q_ref[0, :, r * HEAD:(r + 1) * HEAD] for r in range(rep)]
    else:
        qs = [q_ref[0, :tq // 2, :], q_ref[0, tq // 2:, :]]
    if not prescaled:
        qs = [(q.astype(F32) * QK_SCALE_LOG2).astype(BF16) for q in qs]
    nk = k_ref.shape[1] // tk
    rows = qs[0].shape[0]

    def lane_chunks(s):
        return [s[:, j * HEAD:(j + 1) * HEAD] for j in range(s.shape[1] // HEAD)]

    def row_max(k, part):
        s = [_dot_nt(q, k) for q in qs]
        return tuple(functools.reduce(jnp.maximum, lane_chunks(sc), m) for sc, m in zip(s, part))

    def accumulate(k, v, mx, carry):
        s = [_dot_nt(q, k) for q in qs]
        ps, ls = [], []
        for sc, m, (l, _) in zip(s, mx, carry):
            pc = [jnp.exp2(c - m) for c in lane_chunks(sc)]
            ls.append(functools.reduce(jnp.add, pc, l))
            ps.append(jnp.concatenate([c.astype(BF16) for c in pc], axis=-1))
        pv = [jnp.dot(p, v.astype(BF16), preferred_element_type=F32) for p in ps]
        return tuple((l, acc + x) for l, (_, acc), x in zip(ls, carry, pv))

    def block(ref, i):
        return ref[0, pl.ds(pl.multiple_of(i * tk, tk), tk), :]

    def max_sq_norm(k, best):
        kf = k.astype(BF16).astype(F32)
        sq = jnp.broadcast_to(jnp.sum(kf * kf, axis=-1, keepdims=True), kf.shape)
        return jnp.maximum(best, jnp.max(sq, axis=0, keepdims=True))

    @pl.when(pl.program_id(2) == 0)
    def _():
        best = lax.fori_loop(0, nk, lambda i, b: max_sq_norm(block(k_ref, i), b), jnp.zeros((1, HEAD), F32))
        if with_cache:
            best = max_sq_norm(kc_ref[0], best)
        knorm_ref[...] = jnp.sqrt(best)

    def exact_row_max():
        part = tuple(jnp.full((rows, HEAD), NEG, F32) for _ in qs)
        part = lax.fori_loop(0, nk, lambda i, p: row_max(block(k_ref, i), p), part)
        if with_cache:
            part = row_max(kc_ref[0], part)
        return [jnp.broadcast_to(jnp.max(m, axis=-1, keepdims=True), (rows, HEAD)) for m in part]

    bounds = []
    for q in qs:
        qf = q.astype(F32)
        qn = jnp.sqrt(jnp.sum(qf * qf, axis=-1, keepdims=True))
        bounds.append(jnp.broadcast_to(qn, (rows, HEAD)) * knorm_ref[...])
    safe = jnp.max(functools.reduce(jnp.maximum, bounds)) <= FLASH_SAFE_BOUND
    mx = lax.cond(safe, lambda: [b - FLASH_SHIFT for b in bounds], exact_row_max)

    stats = tuple((jnp.zeros((rows, HEAD), F32), jnp.zeros((rows, HEAD), F32)) for _ in qs)
    stats = lax.fori_loop(0, nk, lambda i, c: accumulate(block(k_ref, i), block(v_ref, i), mx, c), stats)
    if with_cache:
        stats = accumulate(kc_ref[0], vc_ref[0], mx, stats)
    outs = [(acc / jnp.sum(l, axis=-1, keepdims=True)).astype(o_ref.dtype) for l, acc in stats]
    if rep == 2:
        for r in range(rep):
            o_ref[0, :, r * HEAD:(r + 1) * HEAD] = outs[r]
    else:
        o_ref[0, :tq // 2, :] = outs[0]
        o_ref[0, tq // 2:, :] = outs[1]


def _flash(q, k, v, kv_heads, rep, tq, tk, prescaled, head_offsets=(0, 0, 0), cache=None):
    bsz, seq_len, _ = q.shape
    assert rep in (1, 2) and head_offsets[0] % rep == 0
    m_len = k.shape[1]
    qo, ko, vo = head_offsets[0] // rep, head_offsets[1], head_offsets[2]
    in_specs = [pl.BlockSpec((1, tq, rep * HEAD), lambda b, g, t: (b, t, qo + g)),
                pl.BlockSpec((1, m_len, HEAD), lambda b, g, t: (b, 0, ko + g)),
                pl.BlockSpec((1, m_len, HEAD), lambda b, g, t: (b, 0, vo + g))]
    args = [q, k, v]
    if cache is not None:
        layer = cache[2]
        spec = pl.BlockSpec((1, None, cache[0].shape[2], HEAD), lambda b, g, t: (b, layer, 0, g))
        in_specs += [spec, spec]
        args += [cache[0], cache[1]]
    return pl.pallas_call(
        functools.partial(_flash_kernel, rep=rep, tk=tk, prescaled=prescaled, with_cache=cache is not None),
        grid=(bsz, kv_heads, seq_len // tq),
        in_specs=in_specs,
        out_specs=pl.BlockSpec((1, tq, rep * HEAD), lambda b, g, t: (b, t, g)),
        out_shape=jax.ShapeDtypeStruct((bsz, seq_len, kv_heads * rep * HEAD), BF16),
        scratch_shapes=[pltpu.VMEM((1, HEAD), F32)],
        compiler_params=_cparams("parallel", "parallel", "arbitrary"),
        name="flash",
    )(*args)


NA_GROUP = 4
NA_BAND = NA_GROUP + NA_WIN_R
NA_STEP = 2


def _natten_kernel(q_ref, k_ref, v_ref, kc_ref, vc_ref, bias_ref, o_ref):
    rows = q_ref.shape[1] // GRID_W
    gq = NA_GROUP * GRID_W
    kc = kc_ref[0].astype(BF16)
    vc = vc_ref[0].astype(BF16)
    low_half = _iota((GRID_W, HEAD), 1) < GRID_W

    def group_bias(r_first, band_first):
        per_row = []
        for i in range(NA_GROUP):
            r = r_first + i
            r0 = jnp.clip(r - NA_WIN_R // 2, 0, rows - NA_WIN_R)
            blocks = []
            for jj in range(NA_BAND // 2):
                kr = band_first + 2 * jj
                blk = bias_ref[0, jnp.clip(kr - r + NA_WIN_R, 0, 2 * NA_WIN_R - 1)]
                ok_a = jnp.logical_and(kr >= r0, kr < r0 + NA_WIN_R)
                ok_b = jnp.logical_and(kr + 1 >= r0, kr + 1 < r0 + NA_WIN_R)
                blocks.append(jnp.where(low_half, jnp.where(ok_a, blk, NEG), jnp.where(ok_b, blk, NEG)))
            per_row.append(jnp.concatenate(blocks, axis=-1))
        return jnp.concatenate(per_row, axis=0)

    def body(it, _):
        firsts = [(it * NA_STEP + g) * NA_GROUP for g in range(NA_STEP)]
        bands = [jnp.clip(r - NA_WIN_R // 2, 0, rows - NA_BAND) for r in firsts]
        q0 = [pl.multiple_of(r * GRID_W, gq) for r in firsts]
        k0 = [pl.multiple_of(b * GRID_W, GRID_W) for b in bands]
        qs = [(q_ref[0, pl.ds(a, gq), :] * QK_SCALE_LOG2).astype(BF16) for a in q0]
        ks = [k_ref[0, pl.ds(a, NA_BAND * GRID_W), :] for a in k0]
        vs = [v_ref[0, pl.ds(a, NA_BAND * GRID_W), :] for a in k0]
        bias = [group_bias(r, b) for r, b in zip(firsts, bands)]
        s_loc = [_dot_nt(q, k) for q, k in zip(qs, ks)]
        s_ctx = [_dot_nt(q, kc) for q in qs]
        s_loc = [s + b for s, b in zip(s_loc, bias)]
        m = [jnp.maximum(jnp.max(a, axis=-1, keepdims=True), jnp.max(c, axis=-1, keepdims=True))
             for a, c in zip(s_loc, s_ctx)]
        p_loc = [jnp.exp2(a - mm) for a, mm in zip(s_loc, m)]
        p_ctx = [jnp.exp2(c - mm) for c, mm in zip(s_ctx, m)]
        l = [jnp.sum(a, axis=-1, keepdims=True) + jnp.sum(c, axis=-1, keepdims=True) for a, c in zip(p_loc, p_ctx)]
        o_loc = [_dot(p, v) for p, v in zip(p_loc, vs)]
        o_ctx = [_dot(p, vc) for p in p_ctx]
        for a, ol, oc, ll in zip(q0, o_loc, o_ctx, l):
            o_ref[0, pl.ds(a, gq), :] = ((ol + oc) / ll).astype(o_ref.dtype)
        return 0

    lax.fori_loop(0, rows // (NA_GROUP * NA_STEP), body, 0)


def _natten(qkv, kc, vc, layer, bias):
    bsz, seq_len, _ = qkv.shape
    m_len = kc.shape[2]
    col = lambda off: pl.BlockSpec((1, seq_len, HEAD), lambda b, h: (b, 0, off + h))
    ctx = pl.BlockSpec((1, None, m_len, HEAD), lambda b, h: (b, layer, 0, h))
    return pl.pallas_call(
        _natten_kernel,
        grid=(bsz, NA_HEADS),
        in_specs=[col(0), col(NA_HEADS), col(2 * NA_HEADS), ctx, ctx,
                  pl.BlockSpec((1,) + bias.shape[1:], lambda b, h: (h, 0, 0, 0))],
        out_specs=pl.BlockSpec((1, seq_len, HEAD), lambda b, h: (b, 0, h)),
        out_shape=jax.ShapeDtypeStruct((bsz, seq_len, NA_HEADS * HEAD), BF16),
        compiler_params=_cparams("parallel", "parallel"),
        name="natten",
    )(qkv, qkv, qkv, kc, vc, bias)


def _natten_bias(rpb):
    col = np.arange(GRID_W)
    col_start = np.clip(col - NA_WIN_C // 2, 0, GRID_W - NA_WIN_C)
    kc = np.arange(GRID_W)
    valid = (kc[None, :] >= col_start[:, None]) & (kc[None, :] < col_start[:, None] + NA_WIN_C)
    coff = np.clip(kc[None, :] - col[:, None] + NA_WIN_C - 1, 0, 2 * NA_WIN_C - 2)
    log2e = float(np.log2(np.e))
    t = jnp.where(valid[None, None], rpb.astype(F32)[:, :, coff] * log2e, NEG)
    t = jnp.pad(t, ((0, 0), (1, 1), (0, 0), (0, 0)), constant_values=NEG)
    return jnp.concatenate([t[:, :-1], t[:, 1:]], axis=-1)


def _gdn_kernel(q_ref, k_ref, v_ref, z_ref, g_ref, cwq_ref, cwk_ref, cwv_ref, par_ref, nrm_ref, s0_ref,
                o_ref, sfin_ref, qs, ks, vs, oacc):
    bpp, seq_len = q_ref.shape[0], q_ref.shape[1]
    nchunks = seq_len // CHUNK
    h = pl.program_id(1)

    def prep(c, _):
        r0 = pl.multiple_of(c * CHUNK, CHUNK)
        for bb in range(bpp):
            for src, cw, dst, kind in ((q_ref, cwq_ref, qs, "q"), (k_ref, cwk_ref, ks, "k"), (v_ref, cwv_ref, vs, "v")):
                y = _silu(_conv3(src, r0, c, nchunks, seq_len, cw, lead=(bb,)))
                if kind != "v":
                    y = y * lax.rsqrt(jnp.sum(y * y, axis=-1, keepdims=True) + EPS)
                if kind == "q":
                    y = y * HEAD ** -0.5
                dst[bb, pl.ds(r0, CHUNK), :] = y
        return 0

    lax.fori_loop(0, nchunks, prep, 0, unroll=2)

    ri = _iota((CHUNK, CHUNK), 0)
    ci = _iota((CHUNK, CHUNK), 1)
    neg_a = -jnp.exp(par_ref[0:1, :])
    dt_bias = par_ref[1:2, :]

    def load(bb, c):
        r0 = pl.multiple_of(c * CHUNK, CHUNK)
        return (qs[bb, pl.ds(r0, CHUNK), :], ks[bb, pl.ds(r0, CHUNK), :], vs[bb, pl.ds(r0, CHUNK), :],
                g_ref[bb, pl.ds(r0, CHUNK), :])

    def gate_terms(loaded, d):
        q, k, v, gates = loaded
        beta = _lane_col(jax.nn.sigmoid(gates), d * GDN_HEADS + h)
        g_t = neg_a * _softplus(gates + dt_bias)
        gc_t = _cumsum_rows(g_t)
        tot_t = jnp.broadcast_to(gc_t[CHUNK - 1:CHUNK, :], gc_t.shape)
        if d == 1:
            gc_t = tot_t - gc_t + g_t
        gidx = 2 * GDN_HEADS + d * GDN_HEADS + h
        gc = _lane_col(gc_t, gidx)
        tot = _lane_col(tot_t, gidx)
        incl = (ri >= ci) if d == 0 else (ri <= ci)
        decay = jnp.exp(jnp.where(incl, gc - _row_form(gc), NEG))
        egc = jnp.exp(gc)
        kb = k * beta
        return dict(q=q, k=k, kb=kb, incl=incl, strict=(ri > ci) if d == 0 else (ri < ci), decay=decay,
                    x=jnp.concatenate([v * beta, kb * egc], axis=-1), q_dec=q * egc,
                    k_dec=k * jnp.exp(tot - gc), etot=jnp.exp(tot))

    unroll = min(GDN_UNROLL, nchunks)
    nsteps = nchunks // unroll

    def chunk_ids(i):
        cf = [i * unroll + j for j in range(unroll)]
        return cf, [nchunks - 1 - c for c in cf]

    def solve(i):
        cf, cb = chunk_ids(i)
        t = [gate_terms(load(bb, c), d) for bb in range(bpp) for c, d in zip(cf + cb, [0] * unroll + [1] * unroll)]
        kk = [_dot_nt(ch["kb"], ch["k"]) for ch in t]
        qk = [_dot_nt(ch["q"], ch["k"]) for ch in t]
        yield
        a = [jnp.where(ch["strict"], m * ch["decay"], 0.0) for ch, m in zip(t, kk)]
        qk = [jnp.where(ch["incl"], m * ch["decay"], 0.0).astype(BF16) for ch, m in zip(t, qk)]
        inv = yield from _unit_tri_inverse_stages(a, ri, ci)
        x = _dots_many(inv, [ch["x"] for ch in t], TRI_PASSES[2])
        yield
        return tuple((x[n][:, :HEAD], x[n][:, HEAD:].astype(BF16), qk[n], t[n]["q_dec"].astype(BF16),
                      jnp.transpose(t[n]["k_dec"]).astype(BF16), jnp.broadcast_to(t[n]["etot"], (CHUNK, HEAD)))
                     for n in range(2 * unroll * bpp))

    def scan(i, pre, states):
        cf, cb = chunk_ids(i)
        states = list(states)
        outs = []
        for j in range(unroll):
            chains = [pre[bb * 2 * unroll + d * unroll + j] for bb in range(bpp) for d in range(2)]
            sbf = [s.astype(BF16) for s in states]
            ws = [jnp.dot(ch[1], s, preferred_element_type=F32) for ch, s in zip(chains, sbf)]
            o1 = [jnp.dot(ch[3], s, preferred_element_type=F32) for ch, s in zip(chains, sbf)]
            yield
            v_new = [(ch[0] - w).astype(BF16) for ch, w in zip(chains, ws)]
            o2 = [jnp.dot(ch[2], vn, preferred_element_type=F32) for ch, vn in zip(chains, v_new)]
            kv = [jnp.dot(ch[4], vn, preferred_element_type=F32) for ch, vn in zip(chains, v_new)]
            yield
            states = [s * ch[5] + m for ch, s, m in zip(chains, states, kv)]
            outs.append([a1 + a2 for a1, a2 in zip(o1, o2)])
        for j in range(unroll):
            for bb in range(bpp):
                oacc[bb, 0, pl.ds(pl.multiple_of(cf[j] * CHUNK, CHUNK), CHUNK), :] = outs[j][2 * bb]
                oacc[bb, 1, pl.ds(pl.multiple_of(cb[j] * CHUNK, CHUNK), CHUNK), :] = outs[j][2 * bb + 1]
        return tuple(states)

    def step(i, carry):
        pre, states = carry
        nxt, states = _run_interleaved(solve(i + 1), scan(i, pre, states))
        return nxt, states

    pre, _ = _run_interleaved(solve(0), iter(()))
    states = tuple(s0_ref[bb, d, 0] for bb in range(bpp) for d in range(2))
    pre, states = lax.fori_loop(0, nsteps - 1, step, (pre, states))
    _, states = _run_interleaved(iter(()), scan(nsteps - 1, pre, states))
    for bb in range(bpp):
        sfin_ref[bb, 0, 0] = states[2 * bb]
        sfin_ref[bb, 1, 0] = states[2 * bb + 1]

    def fin(c, _):
        r0 = pl.multiple_of(c * CHUNK, CHUNK)
        for bb in range(bpp):
            o = oacc[bb, 0, pl.ds(r0, CHUNK), :] + oacc[bb, 1, pl.ds(r0, CHUNK), :]
            y = _rms(o, nrm_ref[...]) * _silu(z_ref[bb, pl.ds(r0, CHUNK), :])
            o_ref[bb, pl.ds(r0, CHUNK), :] = y.astype(o_ref.dtype)
        return 0

    lax.fori_loop(0, nchunks, fin, 0, unroll=2)


def _gdn(x, gates, conv_w, par, norm_w, s0, layer, bpp):
    bsz, seq_len, _ = x.shape
    col = lambda off: pl.BlockSpec((bpp, seq_len, HEAD), lambda b, h: (b, 0, off + h))
    cw = lambda off: pl.BlockSpec((3, HEAD), lambda b, h: (0, off + h))
    st = pl.BlockSpec((bpp, 2, 1, HEAD, HEAD), lambda b, h: (b, 0, h, 0, 0))
    st_in = pl.BlockSpec((bpp, None, 2, 1, HEAD, HEAD), lambda b, h: (b, layer, 0, h, 0, 0))
    return pl.pallas_call(
        _gdn_kernel,
        grid=(bsz // bpp, GDN_HEADS),
        in_specs=[col(0), col(GDN_HEADS), col(2 * GDN_HEADS), col(3 * GDN_HEADS),
                  pl.BlockSpec((bpp, seq_len, HEAD), lambda b, h: (b, 0, 0)),
                  cw(0), cw(GDN_HEADS), cw(2 * GDN_HEADS),
                  pl.BlockSpec((8, HEAD), lambda b, h: (0, 0)),
                  pl.BlockSpec((1, HEAD), lambda b, h: (0, 0)),
                  st_in],
        out_specs=[pl.BlockSpec((bpp, seq_len, HEAD), lambda b, h: (b, 0, h)), st],
        out_shape=[jax.ShapeDtypeStruct((bsz, seq_len, GDN_HEADS * HEAD), BF16),
                   jax.ShapeDtypeStruct((bsz, 2, GDN_HEADS, HEAD, HEAD), F32)],
        scratch_shapes=[pltpu.VMEM((bpp, seq_len, HEAD), F32)] * 3 + [pltpu.VMEM((bpp, 2, seq_len, HEAD), F32)],
        compiler_params=_cparams("parallel", "parallel"),
        name="gdn",
    )(x, x, x, x, gates, conv_w, conv_w, conv_w, par, norm_w.reshape(1, HEAD), s0)


def _ssd_kernel(x_ref, bm_ref, cm_ref, z_ref, dt_ref, cwx_ref, cwb_ref, cwc_ref, cbx_ref, cbb_ref, cbc_ref,
                par_ref, dsk_ref, h0_ref, y_ref, hfin_ref, xs, bs, cs):
    seq_len = x_ref.shape[1]
    nchunks = seq_len // CHUNK
    width = SSD_HPG * SSD_P

    def prep(c, _):
        r0 = pl.multiple_of(c * CHUNK, CHUNK)
        for src, cw, cb, dst in ((x_ref, cwx_ref, cbx_ref, xs), (bm_ref, cwb_ref, cbb_ref, bs),
                                 (cm_ref, cwc_ref, cbc_ref, cs)):
            dst[pl.ds(r0, CHUNK), :] = _silu(_conv3(src, r0, c, nchunks, seq_len, cw, lead=(0,)) + cb[...])
        return 0

    lax.fori_loop(0, nchunks, prep, 0)

    ri = _iota((CHUNK, CHUNK), 0)
    ci = _iota((CHUNK, CHUNK), 1)
    head_of_lane = jnp.right_shift(_iota((CHUNK, width), 1), 6)
    neg_a = -jnp.exp(par_ref[0:1, :])
    dt_bias = par_ref[1:2, :]

    def expand(cols):
        out = jnp.broadcast_to(cols[0], (CHUNK, width))
        for e in range(1, SSD_HPG):
            out = jnp.where(head_of_lane == e, cols[e], out)
        return out

    def gate_terms(c, d):
        r0 = pl.multiple_of(c * CHUNK, CHUNK)
        x, bm, cm = xs[pl.ds(r0, CHUNK), :], bs[pl.ds(r0, CHUNK), :], cs[pl.ds(r0, CHUNK), :]
        dt_t = _softplus(dt_ref[0, pl.ds(r0, CHUNK), :] + dt_bias)
        la_t = dt_t * neg_a
        cs_t = _cumsum_rows(la_t)
        tot_row = cs_t[CHUNK - 1:CHUNK, :]
        tot_t = jnp.broadcast_to(tot_row, cs_t.shape)
        if d == 1:
            cs_t = tot_t - cs_t + la_t
        incl = (ri >= ci) if d == 0 else (ri <= ci)
        cs_rows = jnp.transpose(cs_t)
        rest_t = tot_t - cs_t
        tot_p = jnp.broadcast_to(tot_row, (SSD_P, HEAD))
        lanes = [d * SSD_HPG + e for e in range(SSD_HPG)]
        col = _lane_col
        lmats = [jnp.exp(jnp.where(incl, col(cs_t, k) - cs_rows[k:k + 1, :], NEG)) for k in lanes]
        xdt = x * expand([col(dt_t, k) for k in lanes])
        cs_full = expand([col(cs_t, k) for k in lanes])
        cd = jnp.concatenate([jnp.broadcast_to(jnp.exp(col(tot_p, k)), (SSD_P, HEAD)) for k in lanes], axis=0)
        return dict(r0=r0, x=x, bm=bm, cm=cm, xdt=xdt, lmats=lmats, ecs=jnp.exp(cs_full), cd=cd,
                    xdec_t=jnp.transpose(xdt * jnp.exp(expand([col(rest_t, k) for k in lanes]))))

    unroll = min(SSD_UNROLL, nchunks)

    def direction(d, h0, finish):
        def step(i, hstate):
            cidx = [i * unroll + j for j in range(unroll)]
            if d == 1:
                cidx = [nchunks - 1 - c for c in cidx]
            t = [gate_terms(c, d) for c in cidx]
            cb = [_dot_nt(ch["cm"], ch["bm"]) for ch in t]
            states = [_dot(ch["xdec_t"], ch["bm"]) for ch in t]
            ys = []
            for ch, m in zip(t, cb):
                y = None
                for e in range(SSD_HPG):
                    part = _dot(m * ch["lmats"][e], jnp.where(head_of_lane == e, ch["xdt"], 0.0))
                    y = part if y is None else y + part
                ys.append(y)
            for j, ch in enumerate(t):
                ys[j] = ys[j] + _dot_nt(ch["cm"], hstate) * ch["ecs"]
                hstate = hstate * ch["cd"] + states[j]
            for ch, y in zip(t, ys):
                finish(ch, y)
            return hstate

        return lax.fori_loop(0, nchunks // unroll, step, h0.reshape(width, HEAD))

    def store_fwd(ch, y):
        y_ref[0, pl.ds(ch["r0"], CHUNK), :] = y

    def store_bwd(ch, y):
        tot = y_ref[0, pl.ds(ch["r0"], CHUNK), :] + y + ch["x"] * dsk_ref[...]
        y_ref[0, pl.ds(ch["r0"], CHUNK), :] = tot * _silu(z_ref[0, pl.ds(ch["r0"], CHUNK), :])

    hfin_ref[0, 0] = direction(0, h0_ref[0, 0], store_fwd).reshape(SSD_HPG, SSD_P, HEAD)
    hfin_ref[0, 1] = direction(1, h0_ref[0, 1], store_bwd).reshape(SSD_HPG, SSD_P, HEAD)


def _ssd(x, gates, conv_w, conv_b, par, d_skip, h0, layer):
    bsz, seq_len, _ = x.shape
    width = SSD_HPG * SSD_P
    spec = lambda wd, off: pl.BlockSpec((1, seq_len, wd), lambda b, g: (b, 0, off + g))
    cws = lambda rows, wd, off: pl.BlockSpec((rows, wd), lambda b, g: (0, off + g))
    st = pl.BlockSpec((1, 2, SSD_HPG, SSD_P, HEAD), lambda b, g: (b, 0, g, 0, 0))
    st_in = pl.BlockSpec((1, None, 2, SSD_HPG, SSD_P, HEAD), lambda b, g: (b, layer, 0, g, 0, 0))
    return pl.pallas_call(
        _ssd_kernel,
        grid=(bsz, SSD_GROUPS),
        in_specs=[spec(width, 0), spec(HEAD, 4), spec(HEAD, 6), spec(width, 4),
                  pl.BlockSpec((1, seq_len, HEAD), lambda b, g: (b, 0, g)),
                  cws(3, width, 0), cws(3, HEAD, 4), cws(3, HEAD, 6),
                  cws(1, width, 0), cws(1, HEAD, 4), cws(1, HEAD, 6),
                  pl.BlockSpec((8, HEAD), lambda b, g: (0, g)),
                  cws(1, width, 0),
                  st_in],
        out_specs=[pl.BlockSpec((1, seq_len, width), lambda b, g: (b, 0, g)), st],
        out_shape=[jax.ShapeDtypeStruct((bsz, seq_len, SSD_HEADS * SSD_P), F32),
                   jax.ShapeDtypeStruct((bsz, 2, SSD_HEADS, SSD_P, HEAD), F32)],
        scratch_shapes=[pltpu.VMEM((seq_len, width), F32), pltpu.VMEM((seq_len, HEAD), F32),
                        pltpu.VMEM((seq_len, HEAD), F32)],
        compiler_params=_cparams("parallel", "parallel"),
        name="ssd",
    )(x, x, x, x, gates, conv_w, conv_w, conv_w, conv_b, conv_b, conv_b, par, d_skip, h0)


def _rope_tables(n_tokens):
    half = HEAD // 2
    inv_freq = ROPE_THETA ** (-jnp.arange(0, half, 2, dtype=F32) / half)
    t = jnp.arange(n_tokens)
    ang_r = (t // GRID_W).astype(F32)[:, None] * inv_freq
    ang_c = (t % GRID_W).astype(F32)[:, None] * inv_freq
    cos = jnp.concatenate([jnp.cos(ang_r)] * 2 + [jnp.cos(ang_c)] * 2, axis=-1)
    sin = jnp.concatenate([-jnp.sin(ang_r), jnp.sin(ang_r), -jnp.sin(ang_c), jnp.sin(ang_c)], axis=-1)
    return cos, sin


def _gate_rows(a_log, dt_bias, offset):
    n = a_log.size
    rows = jnp.zeros((8, HEAD), F32)
    rows = rows.at[0, offset:offset + n].set(a_log.reshape(-1).astype(F32))
    return rows.at[1, offset:offset + n].set(dt_bias.reshape(-1).astype(F32))


def _ssd_group_lanes(t):
    lead = t.shape[:-1]
    t = t.reshape(lead + (2, SSD_GROUPS, SSD_HPG))
    t = jnp.swapaxes(t, -3, -2).reshape(lead + (SSD_GROUPS, 2 * SSD_HPG))
    pad = [(0, 0)] * (t.ndim - 1) + [(0, HEAD - 2 * SSD_HPG)]
    return jnp.pad(t, pad).reshape(lead + (SSD_GROUPS * HEAD,))


def _pad_cols(w, width):
    return jnp.pad(w, ((0, 0), (0, width - w.shape[1])))


def kernel(x_prompt, x_sample, state_gdn, cache_gqa_k, cache_gqa_v, state_ssd, cache_na_k, cache_na_v, c, c_ctx, ada_w, ada_b, norm_mix_pre, norm_mix_post, norm_mlp_pre, norm_mlp_post, mlp_w1, mlp_w2, ev_w_in, ev_w_out, gdn_conv, gdn_a_log, gdn_dt_bias, gdn_norm, gqa_q_norm, gqa_k_norm, od_w_in, od_w_out, ssd_conv, ssd_conv_b, ssd_a_log, ssd_dt_bias, ssd_d, ssd_norm, na_rpb):
    nb, ns, _ = x_prompt.shape
    db, dl, _ = x_sample.shape
    past = cache_gqa_k.shape[2]

    c16 = jnp.zeros((16, D_MODEL), F32).at[:db].set(c).at[db].set(c_ctx)
    mods = _ada(c16, ada_w, ada_b).reshape(DEPTH, 16, 6, D_MODEL)

    cos_l, sin_l = _rope_tables(dl)
    cos_c, sin_c = jnp.ones((ns, HEAD), F32), jnp.zeros((ns, HEAD), F32)

    xp = x_prompt.reshape(1, nb * ns, D_MODEL)
    xs = x_sample
    tm_c = min(1024, nb * ns)
    tm_in = min(512, nb * ns, dl)
    ctx_bpp = max(n for n in (4, 2, 1) if nb % n == 0)
    zero_gdn = jnp.zeros((nb, 1, 2, GDN_HEADS, HEAD, HEAD), F32)
    zero_ssd = jnp.zeros((nb, 1, 2, SSD_HEADS, SSD_P, HEAD), F32)
    new_gdn, new_gk, new_gv, new_ssd, new_nk, new_nv = [], [], [], [], [], []

    for i in range(DEPTH):
        j = i // 2
        mod_l, mod_c = mods[i, :db], mods[i, db:db + 1]
        if i % 2 == 0:
            wi = ev_w_in[j]
            w_in = jnp.concatenate([wi[:, :2048], wi[:, 2064:], _pad_cols(wi[:, 2048:2064], HEAD)], axis=1).astype(BF16)
            widths = (2048, 1024, HEAD)
            w_out = ev_w_out[j].astype(BF16)
            par = _gate_rows(gdn_a_log[j], gdn_dt_bias[j], 2 * GDN_HEADS)

            gx, ax, gt = _inproj(xp, mod_c, norm_mix_pre[i], w_in, widths, tm_in)
            gx, ax, gt = (t.reshape(nb, ns, -1) for t in (gx, ax, gt))
            o_a, s_a = _gdn(gx, gt, gdn_conv[j], par, gdn_norm[j], zero_gdn, 0, ctx_bpp)
            q, k, v, kf = _qkprep(ax, cos_c, sin_c, gqa_q_norm[j], gqa_k_norm[j], ns, True)
            o_b = _flash(q, k, v, GQA_KV_HEADS, 2, ns, ns, True)
            yc = _outproj(o_a.reshape(1, nb * ns, -1), o_b.reshape(1, nb * ns, -1), xp, mod_c,
                          w_out, norm_mix_post[i], tm_c)
            new_gdn.append(s_a)
            new_gk.append(kf.reshape(nb, ns, GQA_KV_HEADS, HEAD))
            new_gv.append(ax[..., 768:].reshape(nb, ns, GQA_KV_HEADS, HEAD))

            gx, ax, gt = _inproj(xs, mod_l, norm_mix_pre[i], w_in, widths, tm_in)
            o_a, _ = _gdn(gx, gt, gdn_conv[j], par, gdn_norm[j], state_gdn, j, 1)
            q, k, v = _qkprep(ax, cos_l, sin_l, gqa_q_norm[j], gqa_k_norm[j], 512, False)
            o_b = _flash(q, k, v, GQA_KV_HEADS, 2, 512, min(2048, dl), True,
                         cache=(cache_gqa_k.reshape(db, -1, past, GQA_KV_HEADS * HEAD),
                                cache_gqa_v.reshape(db, -1, past, GQA_KV_HEADS * HEAD), j))
            yl = _outproj(o_a, o_b, xs, mod_l, w_out, norm_mix_post[i], 512)
        else:
            wi = od_w_in[j]
            w_in = jnp.concatenate([wi[:, 512:1536], wi[:, :512], wi[:, 1552:], _ssd_group_lanes(wi[:, 1536:1552])], axis=1).astype(BF16)
            widths = (1536, 1536, SSD_GROUPS * HEAD)
            w_out = od_w_out[j].astype(BF16)
            par = jnp.zeros((8, SSD_GROUPS * HEAD), F32)
            par = par.at[0].set(_ssd_group_lanes(ssd_a_log[j].reshape(-1).astype(F32)))
            par = par.at[1].set(_ssd_group_lanes(ssd_dt_bias[j].reshape(-1).astype(F32)))
            dsk = jnp.repeat(ssd_d[j].astype(F32), SSD_P).reshape(1, -1)
            cvw, cvb = ssd_conv[j], ssd_conv_b[j].reshape(1, -1)

            sx, ax, gt = _inproj(xp, mod_c, norm_mix_pre[i], w_in, widths, tm_in)
            sx, ax, gt = (t.reshape(nb, ns, -1) for t in (sx, ax, gt))
            y_c, s_c = _ssd(sx, gt, cvw, cvb, par, dsk, zero_ssd, 0)
            o_d = _flash(ax, ax, ax, NA_HEADS, 1, ns, ns, False, (0, NA_HEADS, 2 * NA_HEADS))
            yc = _outproj(y_c.reshape(1, nb * ns, -1), o_d.reshape(1, nb * ns, -1), xp, mod_c,
                          w_out, norm_mix_post[i], tm_c, ssd_norm[j])
            new_ssd.append(s_c)
            new_nk.append(ax[..., 512:1024].reshape(nb, ns, NA_HEADS, HEAD))
            new_nv.append(ax[..., 1024:].reshape(nb, ns, NA_HEADS, HEAD))

            sx, ax, gt = _inproj(xs, mod_l, norm_mix_pre[i], w_in, widths, tm_in)
            y_c, _ = _ssd(sx, gt, cvw, cvb, par, dsk, state_ssd, j)
            o_d = _natten(ax, cache_na_k.reshape(db, -1, past, NA_HEADS * HEAD),
                          cache_na_v.reshape(db, -1, past, NA_HEADS * HEAD), j, _natten_bias(na_rpb[j]))
            yl = _outproj(y_c, o_d, xs, mod_l, w_out, norm_mix_post[i], 512, ssd_norm[j])
        xp, xs = yc, yl
        w1, w2 = mlp_w1[i].astype(BF16), mlp_w2[i].astype(BF16)
        xp = _mlp(xp, mod_c, norm_mlp_pre[i], w1, w2, norm_mlp_post[i], tm_c, 1024)
        xs = _mlp(xs, mod_l, norm_mlp_pre[i], w1, w2, norm_mlp_post[i], 1024, 1024)

    return (xp.reshape(nb, ns, D_MODEL), xs, jnp.stack(new_gdn, axis=1), jnp.stack(new_gk, axis=1),
            jnp.stack(new_gv, axis=1), jnp.stack(new_ssd, axis=1), jnp.stack(new_nk, axis=1),
            jnp.stack(new_nv, axis=1))
```

```python
import functools

import numpy as np
import jax
import jax.numpy as jnp
from jax import lax
from jax.experimental import pallas as pl
from jax.experimental.pallas import tpu as pltpu

F32 = jnp.float32
BF16 = jnp.bfloat16

D_MODEL = 1024
DEPTH = 4
GRID_W = 64
EPS = 1e-6
ROPE_THETA = 10000.0
D_FF = 4 * D_MODEL
HEAD = 128
GDN_HEADS = 4
GQA_HEADS = 4
GQA_KV_HEADS = 2
SSD_HEADS = 8
SSD_P = 64
SSD_GROUPS = 2
SSD_HPG = SSD_HEADS // SSD_GROUPS
NA_HEADS = 4
NA_WIN_R = 8
NA_WIN_C = 16
CHUNK = 128
GDN_UNROLL = 4
SSD_UNROLL = 4
TRI_PASSES = (1, 1, 2)
NEG = -1e30
QK_SCALE_LOG2 = HEAD ** -0.5 * float(np.log2(np.e))
FLASH_SAFE_BOUND = 60.0
FLASH_SHIFT = 64.0
VMEM_LIMIT_BYTES = 56 * 1024 * 1024


def _cparams(*sem):
    return pltpu.CompilerParams(dimension_semantics=sem, vmem_limit_bytes=VMEM_LIMIT_BYTES)


def _dot(a, b):
    return jnp.dot(a.astype(BF16), b.astype(BF16), preferred_element_type=F32)


def _dot_nt(a, b):
    return lax.dot_general(a.astype(BF16), b.astype(BF16), (((1,), (1,)), ((), ())),
                           preferred_element_type=F32)


def _split_bf16(a):
    hi = a.astype(BF16)
    lo = (a - hi.astype(F32)).astype(BF16)
    return hi, lo


def _dot3(a, b):
    ah, al = _split_bf16(a)
    bh, bl = _split_bf16(b)
    d = lambda x, y: jnp.dot(x, y, preferred_element_type=F32)
    return d(ah, bh) + (d(ah, bl) + d(al, bh))


def _silu(x):
    return x * jax.nn.sigmoid(x)


def _softplus(x):
    return jnp.maximum(x, 0.0) + jnp.log(1.0 + jnp.exp(-jnp.abs(x)))


def _rms(x, w):
    return x * lax.rsqrt(jnp.mean(x * x, axis=-1, keepdims=True) + EPS) * w


def _iota(shape, axis):
    return lax.broadcasted_iota(jnp.int32, shape, axis)


def _lane_col(tile, idx):
    return jnp.sum(jnp.where(_iota(tile.shape, 1) == idx, tile, 0.0), axis=-1, keepdims=True)


def _cumsum_rows(t):
    rows = _iota(t.shape, 0)
    s = 1
    while s < t.shape[0]:
        t = t + jnp.where(rows >= s, pltpu.roll(t, s, 0), 0.0)
        s *= 2
    return t


def _row_form(col):
    n = col.shape[0]
    return jnp.transpose(jnp.broadcast_to(col, (n, n)))


def _dot3_many(xs, ys):
    xs = [_split_bf16(x) for x in xs]
    ys = [_split_bf16(y) for y in ys]
    d = lambda x, y: jnp.dot(x, y, preferred_element_type=F32)
    hh = [d(x[0], y[0]) for x, y in zip(xs, ys)]
    hl = [d(x[0], y[1]) for x, y in zip(xs, ys)]
    lh = [d(x[1], y[0]) for x, y in zip(xs, ys)]
    return [a + (b + c) for a, b, c in zip(hh, hl, lh)]


def _dots_many(xs, ys, passes):
    if passes == 3:
        return _dot3_many(xs, ys)
    if passes == 2:
        d = lambda x, y: jnp.dot(x, y, preferred_element_type=F32)
        xs = [x.astype(BF16) for x in xs]
        ys = [_split_bf16(y) for y in ys]
        hi = [d(x, y[0]) for x, y in zip(xs, ys)]
        lo = [d(x, y[1]) for x, y in zip(xs, ys)]
        return [a + b for a, b in zip(hi, lo)]
    return [_dot(x, y) for x, y in zip(xs, ys)]


def _unit_tri_inverse_stages(a_list, ri, ci):
    same = lambda sh: jnp.right_shift(ri, sh) == jnp.right_shift(ci, sh)
    eye = jnp.where(ri == ci, 1.0, 0.0)
    p = [jnp.where(same(4), a, 0.0) for a in a_list]
    m = [eye - x for x in p]
    p = _dots_many(p, p, TRI_PASSES[0])
    yield
    for last in (False, False, True):
        mp = _dots_many(m, p, TRI_PASSES[0])
        if not last:
            p = _dots_many(p, p, TRI_PASSES[0])
        yield
        m = [x + y for x, y in zip(m, mp)]
    for sh in (4, 5, 6):
        off = [jnp.where(same(sh + 1), jnp.where(same(sh), 0.0, a), 0.0) for a in a_list]
        t = _dots_many(off, m, TRI_PASSES[1])
        yield
        mt = _dots_many(m, t, TRI_PASSES[1])
        yield
        m = [x - y for x, y in zip(m, mt)]
    return m


def _run_interleaved(gen_a, gen_b):
    results, live = [None, None], [gen_a, gen_b]
    while any(g is not None for g in live):
        for n, g in enumerate(live):
            if g is None:
                continue
            try:
                next(g)
            except StopIteration as stop:
                results[n], live[n] = stop.value, None
    return results


def _conv3(ref, r0, c, nchunks, seq_len, w, lead=()):
    x = ref[lead + (pl.ds(r0, CHUNK), slice(None))]
    prev = ref[lead + (pl.ds(jnp.maximum(r0 - 1, 0), 1), slice(None))]
    nxt = ref[lead + (pl.ds(jnp.minimum(r0 + CHUNK, seq_len - 1), 1), slice(None))]
    prev = jnp.where(c == 0, 0.0, prev)
    nxt = jnp.where(c == nchunks - 1, 0.0, nxt)
    rows = _iota(x.shape, 0)
    xm = jnp.where(rows == 0, prev, pltpu.roll(x, 1, 0))
    xp = jnp.where(rows == CHUNK - 1, nxt, pltpu.roll(x, CHUNK - 1, 0))
    return xm * w[0:1, :] + x * w[1:2, :] + xp * w[2:3, :]


def _ada_kernel(c_ref, w_ref, b_ref, o_ref):
    o_ref[0] = _dot(_silu(c_ref[...]), w_ref[0]) + b_ref[0]


def _ada(c16, ada_w, ada_b):
    tn = 1536
    n = ada_w.shape[-1]
    return pl.pallas_call(
        _ada_kernel,
        grid=(DEPTH, n // tn),
        in_specs=[pl.BlockSpec((16, D_MODEL), lambda i, j: (0, 0)),
                  pl.BlockSpec((1, D_MODEL, tn), lambda i, j: (i, 0, j)),
                  pl.BlockSpec((1, 1, tn), lambda i, j: (i, 0, j))],
        out_specs=pl.BlockSpec((1, 16, tn), lambda i, j: (i, 0, j)),
        out_shape=jax.ShapeDtypeStruct((DEPTH, 16, n), F32),
        compiler_params=_cparams("parallel", "parallel"),
        name="ada",
    )(c16, ada_w, ada_b.reshape(DEPTH, 1, n))


def _mod_spec(mod):
    if mod.shape[0] == 1:
        return pl.BlockSpec((1, 6, D_MODEL), lambda b, t: (0, 0, 0))
    return pl.BlockSpec((1, 6, D_MODEL), lambda b, t: (b, 0, 0))


def _inproj_kernel(x_ref, mod_ref, nw_ref, w_ref, *rest, splits, rope, plain_kv):
    if rope:
        cos_ref, sin_ref, qn_ref, kn_ref = rest[:4]
        rest = rest[4:]
    mix_ref, gate_ref, q_ref, k_ref, v_ref = rest[:5]
    h = _rms(x_ref[0], nw_ref[...])
    h = h * (1.0 + mod_ref[0, 1:2, :]) + mod_ref[0, 0:1, :]
    hb = h.astype(BF16)
    (m0, m1), (a0, a1), (g0, g1) = splits
    y = jnp.dot(hb, w_ref[:, a0:a1], preferred_element_type=F32)
    mix_ref[0] = jnp.dot(hb, w_ref[:, m0:m1], preferred_element_type=F32)
    gate_ref[0] = jnp.dot(hb, w_ref[:, g0:g1], preferred_element_type=F32)
    nq, nk = q_ref.shape[-1], k_ref.shape[-1]
    head = lambda off, hd: y[:, off + hd * HEAD:off + (hd + 1) * HEAD]
    if rope:
        cos, sin = cos_ref[...], sin_ref[...]
        first = (_iota(cos.shape, 1) & 32) == 0

        def normed_rotated(xh, w):
            n = _rms(xh, w)
            swapped = jnp.where(first, pltpu.roll(n, HEAD - 32, 1), pltpu.roll(n, 32, 1))
            return n, n * cos + swapped * sin

        for hd in range(nq // HEAD):
            _, r = normed_rotated(head(0, hd), qn_ref[...])
            q_ref[0, :, hd * HEAD:(hd + 1) * HEAD] = (r * QK_SCALE_LOG2).astype(BF16)
        for hd in range(nk // HEAD):
            n, r = normed_rotated(head(nq, hd), kn_ref[...])
            k_ref[0, :, hd * HEAD:(hd + 1) * HEAD] = r.astype(BF16)
            if plain_kv:
                rest[5][0, :, hd * HEAD:(hd + 1) * HEAD] = n
    else:
        q_ref[0] = (y[:, :nq] * QK_SCALE_LOG2).astype(BF16)
        k_ref[0] = y[:, nq:nq + nk].astype(BF16)
        if plain_kv:
            rest[5][0] = y[:, nq:nq + nk]
    v_ref[0] = y[:, nq + nk:].astype(BF16)
    if plain_kv:
        rest[6][0] = y[:, nq + nk:]


def _inproj(x, mod, norm_w, w, widths, tm, attn_widths, plain_kv, rope=None):
    bsz, seq_len, _ = x.shape
    splits, a = [], 0
    for wd in (widths[0], widths[1], widths[2]):
        splits.append((a, a + wd))
        a += wd
    row = lambda wd: pl.BlockSpec((1, tm, wd), lambda b, t: (b, t, 0))
    in_specs = [row(D_MODEL), _mod_spec(mod), pl.BlockSpec((1, D_MODEL), lambda b, t: (0, 0)),
                pl.BlockSpec(w.shape, lambda b, t: (0, 0))]
    args = [x, mod, norm_w.reshape(1, D_MODEL), w]
    if rope is not None:
        cos, sin, qn, kn, shared = rope
        tab = (pl.BlockSpec((tm, HEAD), lambda b, t: (0, 0)) if shared
               else pl.BlockSpec((tm, HEAD), lambda b, t: (t, 0)))
        in_specs += [tab, tab, pl.BlockSpec((1, HEAD), lambda b, t: (0, 0)), pl.BlockSpec((1, HEAD), lambda b, t: (0, 0))]
        args += [cos, sin, qn.reshape(1, HEAD), kn.reshape(1, HEAD)]
    out = [(widths[0], F32), (widths[2], F32)] + [(wd, BF16) for wd in attn_widths]
    if plain_kv:
        out += [(attn_widths[1], F32), (attn_widths[2], F32)]
    return pl.pallas_call(
        functools.partial(_inproj_kernel, splits=tuple(splits), rope=rope is not None, plain_kv=plain_kv),
        grid=(bsz, seq_len // tm),
        in_specs=in_specs,
        out_specs=[row(wd) for wd, _ in out],
        out_shape=[jax.ShapeDtypeStruct((bsz, seq_len, wd), dt) for wd, dt in out],
        compiler_params=_cparams("parallel", "parallel"),
        name="inproj",
    )(*args)


def _outproj_kernel(a_ref, b_ref, x_ref, mod_ref, w_ref, nw_ref, *rest, norm_a):
    a = a_ref[0]
    if norm_a:
        na_ref, o_ref = rest
        a = _rms(a, na_ref[...])
    else:
        (o_ref,) = rest
    half = a.shape[-1]
    y = (jnp.dot(a.astype(BF16), w_ref[:half, :], preferred_element_type=F32)
         + jnp.dot(b_ref[0].astype(BF16), w_ref[half:, :], preferred_element_type=F32))
    o_ref[0] = x_ref[0] + mod_ref[0, 2:3, :] * _rms(y, nw_ref[...])


def _outproj(a, b, x, mod, w, norm_w, tm, norm_a_w=None):
    bsz, seq_len, _ = x.shape
    half = a.shape[-1]
    in_specs = [pl.BlockSpec((1, tm, half), lambda bb, t: (bb, t, 0)),
                pl.BlockSpec((1, tm, half), lambda bb, t: (bb, t, 0)),
                pl.BlockSpec((1, tm, D_MODEL), lambda bb, t: (bb, t, 0)),
                _mod_spec(mod),
                pl.BlockSpec(w.shape, lambda bb, t: (0, 0)),
                pl.BlockSpec((1, D_MODEL), lambda bb, t: (0, 0))]
    args = [a, b, x, mod, w, norm_w.reshape(1, D_MODEL)]
    if norm_a_w is not None:
        in_specs.append(pl.BlockSpec((1, half), lambda bb, t: (0, 0)))
        args.append(norm_a_w.reshape(1, half))
    return pl.pallas_call(
        functools.partial(_outproj_kernel, norm_a=norm_a_w is not None),
        grid=(bsz, seq_len // tm),
        in_specs=in_specs,
        out_specs=pl.BlockSpec((1, tm, D_MODEL), lambda bb, t: (bb, t, 0)),
        out_shape=jax.ShapeDtypeStruct(x.shape, F32),
        compiler_params=_cparams("parallel", "parallel"),
        name="outproj",
    )(*args)


def _mlp_kernel(x_ref, mod_ref, npre_ref, w1_ref, w2_ref, npost_ref, o_ref, h_ref, acc_ref):
    f = pl.program_id(2)

    @pl.when(f == 0)
    def _():
        h = _rms(x_ref[0], npre_ref[...])
        h = h * (1.0 + mod_ref[0, 4:5, :]) + mod_ref[0, 3:4, :]
        h_ref[...] = h.astype(BF16)
        acc_ref[...] = jnp.zeros_like(acc_ref)

    u = jnp.maximum(jnp.dot(h_ref[...], w1_ref[...], preferred_element_type=F32), 0.0)
    acc_ref[...] += jnp.dot((u * u).astype(BF16), w2_ref[...], preferred_element_type=F32)

    @pl.when(f == pl.num_programs(2) - 1)
    def _():
        o_ref[0] = x_ref[0] + mod_ref[0, 5:6, :] * _rms(acc_ref[...], npost_ref[...])


def _mlp(x, mod, npre, w1, w2, npost, tm, tf):
    bsz, seq_len, _ = x.shape
    mspec = _mod_spec(mod)
    return pl.pallas_call(
        _mlp_kernel,
        grid=(bsz, seq_len // tm, D_FF // tf),
        in_specs=[pl.BlockSpec((1, tm, D_MODEL), lambda b, t, f: (b, t, 0)),
                  pl.BlockSpec((1, 6, D_MODEL), lambda b, t, f: mspec.index_map(b, t)),
                  pl.BlockSpec((1, D_MODEL), lambda b, t, f: (0, 0)),
                  pl.BlockSpec((D_MODEL, tf), lambda b, t, f: (0, f)),
                  pl.BlockSpec((tf, D_MODEL), lambda b, t, f: (f, 0)),
                  pl.BlockSpec((1, D_MODEL), lambda b, t, f: (0, 0))],
        out_specs=pl.BlockSpec((1, tm, D_MODEL), lambda b, t, f: (b, t, 0)),
        out_shape=jax.ShapeDtypeStruct(x.shape, F32),
        scratch_shapes=[pltpu.VMEM((tm, D_MODEL), BF16), pltpu.VMEM((tm, D_MODEL), F32)],
        compiler_params=_cparams("parallel", "parallel", "arbitrary"),
        name="mlp",
    )(x, mod, npre.reshape(1, D_MODEL), w1, w2, npost.reshape(1, D_MODEL))


def _flash_kernel(q_ref, k_ref, v_ref, *rest, rep, tk, with_cache):
    if with_cache:
        kc_ref, vc_ref, o_ref, knorm_ref = rest
    else:
        o_ref, knorm_ref = rest
    tq = q_ref.shape[1]
    if rep == 2:
        qs = [q_ref[0, :, r * HEAD:(r + 1) * HEAD] for r in range(rep)]
    else:
        qs = [q_ref[0, :tq // 2, :], q_ref[0, tq // 2:, :]]
    nk = k_ref.shape[1] // tk
    rows = qs[0].shape[0]

    def lane_chunks(s):
        return [s[:, j * HEAD:(j + 1) * HEAD] for j in range(s.shape[1] // HEAD)]

    def row_max(k, part):
        s = [_dot_nt(q, k) for q in qs]
        return tuple(functools.reduce(jnp.maximum, lane_chunks(sc), m) for sc, m in zip(s, part))

    def accumulate(k, v, mx, carry):
        s = [_dot_nt(q, k) for q in qs]
        ps, ls = [], []
        for sc, m, (l, _) in zip(s, mx, carry):
            pc = [jnp.exp2(c - m) for c in lane_chunks(sc)]
            ls.append(functools.reduce(jnp.add, pc, l))
            ps.append(jnp.concatenate([c.astype(BF16) for c in pc], axis=-1))
        pv = [jnp.dot(p, v.astype(BF16), preferred_element_type=F32) for p in ps]
        return tuple((l, acc + x) for l, (_, acc), x in zip(ls, carry, pv))

    def block(ref, i):
        return ref[0, pl.ds(pl.multiple_of(i * tk, tk), tk), :]

    def max_sq_norm(k, best):
        kf = k.astype(BF16).astype(F32)
        sq = jnp.broadcast_to(jnp.sum(kf * kf, axis=-1, keepdims=True), kf.shape)
        return jnp.maximum(best, jnp.max(sq, axis=0, keepdims=True))

    @pl.when(pl.program_id(2) == 0)
    def _():
        best = lax.fori_loop(0, nk, lambda i, b: max_sq_norm(block(k_ref, i), b), jnp.zeros((1, HEAD), F32))
        if with_cache:
            best = max_sq_norm(kc_ref[0], best)
        knorm_ref[...] = jnp.sqrt(best)

    def exact_row_max():
        part = tuple(jnp.full((rows, HEAD), NEG, F32) for _ in qs)
        part = lax.fori_loop(0, nk, lambda i, p: row_max(block(k_ref, i), p), part)
        if with_cache:
            part = row_max(kc_ref[0], part)
        return [jnp.broadcast_to(jnp.max(m, axis=-1, keepdims=True), (rows, HEAD)) for m in part]

    bounds = []
    for q in qs:
        qf = q.astype(F32)
        qn = jnp.sqrt(jnp.sum(qf * qf, axis=-1, keepdims=True))
        bounds.append(jnp.broadcast_to(qn, (rows, HEAD)) * knorm_ref[...])
    safe = jnp.max(functools.reduce(jnp.maximum, bounds)) <= FLASH_SAFE_BOUND
    mx = lax.cond(safe, lambda: [b - FLASH_SHIFT for b in bounds], exact_row_max)

    stats = tuple((jnp.zeros((rows, HEAD), F32), jnp.zeros((rows, HEAD), F32)) for _ in qs)
    stats = lax.fori_loop(0, nk, lambda i, c: accumulate(block(k_ref, i), block(v_ref, i), mx, c), stats)
    if with_cache:
        stats = accumulate(kc_ref[0], vc_ref[0], mx, stats)
    outs = [(acc / jnp.sum(l, axis=-1, keepdims=True)).astype(o_ref.dtype) for l, acc in stats]
    if rep == 2:
        for r in range(rep):
            o_ref[0, :, r * HEAD:(r + 1) * HEAD] = outs[r]
    else:
        o_ref[0, :tq // 2, :] = outs[0]
        o_ref[0, tq // 2:, :] = outs[1]


def _flash(q, k, v, kv_heads, rep, tq, tk, cache=None):
    bsz, seq_len, _ = q.shape
    assert rep in (1, 2)
    m_len = k.shape[1]
    in_specs = [pl.BlockSpec((1, tq, rep * HEAD), lambda b, g, t: (b, t, g)),
                pl.BlockSpec((1, m_len, HEAD), lambda b, g, t: (b, 0, g)),
                pl.BlockSpec((1, m_len, HEAD), lambda b, g, t: (b, 0, g))]
    args = [q, k, v]
    if cache is not None:
        layer = cache[2]
        spec = pl.BlockSpec((1, None, cache[0].shape[2], HEAD), lambda b, g, t: (b, layer, 0, g))
        in_specs += [spec, spec]
        args += [cache[0], cache[1]]
    return pl.pallas_call(
        functools.partial(_flash_kernel, rep=rep, tk=tk, with_cache=cache is not None),
        grid=(bsz, kv_heads, seq_len // tq),
        in_specs=in_specs,
        out_specs=pl.BlockSpec((1, tq, rep * HEAD), lambda b, g, t: (b, t, g)),
        out_shape=jax.ShapeDtypeStruct((bsz, seq_len, kv_heads * rep * HEAD), BF16),
        scratch_shapes=[pltpu.VMEM((1, HEAD), F32)],
        compiler_params=_cparams("parallel", "parallel", "arbitrary"),
        name="flash",
    )(*args)


NA_GROUP = 4
NA_BAND = NA_GROUP + NA_WIN_R
NA_STEP = 2


def _natten_kernel(q_ref, k_ref, v_ref, kc_ref, vc_ref, bias_ref, o_ref):
    rows = q_ref.shape[1] // GRID_W
    gq = NA_GROUP * GRID_W
    kc = kc_ref[0].astype(BF16)
    vc = vc_ref[0].astype(BF16)
    low_half = _iota((GRID_W, HEAD), 1) < GRID_W

    def group_bias(r_first, band_first):
        per_row = []
        for i in range(NA_GROUP):
            r = r_first + i
            r0 = jnp.clip(r - NA_WIN_R // 2, 0, rows - NA_WIN_R)
            blocks = []
            for jj in range(NA_BAND // 2):
                kr = band_first + 2 * jj
                blk = bias_ref[0, jnp.clip(kr - r + NA_WIN_R, 0, 2 * NA_WIN_R - 1)]
                ok_a = jnp.logical_and(kr >= r0, kr < r0 + NA_WIN_R)
                ok_b = jnp.logical_and(kr + 1 >= r0, kr + 1 < r0 + NA_WIN_R)
                blocks.append(jnp.where(low_half, jnp.where(ok_a, blk, NEG), jnp.where(ok_b, blk, NEG)))
            per_row.append(jnp.concatenate(blocks, axis=-1))
        return jnp.concatenate(per_row, axis=0)

    def body(it, _):
        firsts = [(it * NA_STEP + g) * NA_GROUP for g in range(NA_STEP)]
        bands = [jnp.clip(r - NA_WIN_R // 2, 0, rows - NA_BAND) for r in firsts]
        q0 = [pl.multiple_of(r * GRID_W, gq) for r in firsts]
        k0 = [pl.multiple_of(b * GRID_W, GRID_W) for b in bands]
        qs = [q_ref[0, pl.ds(a, gq), :] for a in q0]
        ks = [k_ref[0, pl.ds(a, NA_BAND * GRID_W), :] for a in k0]
        vs = [v_ref[0, pl.ds(a, NA_BAND * GRID_W), :] for a in k0]
        bias = [group_bias(r, b) for r, b in zip(firsts, bands)]
        s_loc = [_dot_nt(q, k) for q, k in zip(qs, ks)]
        s_ctx = [_dot_nt(q, kc) for q in qs]
        s_loc = [s + b for s, b in zip(s_loc, bias)]
        m = [jnp.maximum(jnp.max(a, axis=-1, keepdims=True), jnp.max(c, axis=-1, keepdims=True))
             for a, c in zip(s_loc, s_ctx)]
        p_loc = [jnp.exp2(a - mm) for a, mm in zip(s_loc, m)]
        p_ctx = [jnp.exp2(c - mm) for c, mm in zip(s_ctx, m)]
        l = [jnp.sum(a, axis=-1, keepdims=True) + jnp.sum(c, axis=-1, keepdims=True) for a, c in zip(p_loc, p_ctx)]
        o_loc = [_dot(p, v) for p, v in zip(p_loc, vs)]
        o_ctx = [_dot(p, vc) for p in p_ctx]
        for a, ol, oc, ll in zip(q0, o_loc, o_ctx, l):
            o_ref[0, pl.ds(a, gq), :] = ((ol + oc) / ll).astype(o_ref.dtype)
        return 0

    lax.fori_loop(0, rows // (NA_GROUP * NA_STEP), body, 0)


def _natten(q, k, v, kc, vc, layer, bias):
    bsz, seq_len, _ = q.shape
    m_len = kc.shape[2]
    col = pl.BlockSpec((1, seq_len, HEAD), lambda b, h: (b, 0, h))
    ctx = pl.BlockSpec((1, None, m_len, HEAD), lambda b, h: (b, layer, 0, h))
    return pl.pallas_call(
        _natten_kernel,
        grid=(bsz, NA_HEADS),
        in_specs=[col, col, col, ctx, ctx,
                  pl.BlockSpec((1,) + bias.shape[1:], lambda b, h: (h, 0, 0, 0))],
        out_specs=pl.BlockSpec((1, seq_len, HEAD), lambda b, h: (b, 0, h)),
        out_shape=jax.ShapeDtypeStruct((bsz, seq_len, NA_HEADS * HEAD), BF16),
        compiler_params=_cparams("parallel", "parallel"),
        name="natten",
    )(q, k, v, kc, vc, bias)


def _natten_bias(rpb):
    col = np.arange(GRID_W)
    col_start = np.clip(col - NA_WIN_C // 2, 0, GRID_W - NA_WIN_C)
    kc = np.arange(GRID_W)
    valid = (kc[None, :] >= col_start[:, None]) & (kc[None, :] < col_start[:, None] + NA_WIN_C)
    coff = np.clip(kc[None, :] - col[:, None] + NA_WIN_C - 1, 0, 2 * NA_WIN_C - 2)
    log2e = float(np.log2(np.e))
    t = jnp.where(valid[None, None], rpb.astype(F32)[:, :, coff] * log2e, NEG)
    t = jnp.pad(t, ((0, 0), (1, 1), (0, 0), (0, 0)), constant_values=NEG)
    return jnp.concatenate([t[:, :-1], t[:, 1:]], axis=-1)


def _gdn_kernel(q_ref, k_ref, v_ref, z_ref, g_ref, cwq_ref, cwk_ref, cwv_ref, par_ref, nrm_ref, s0_ref,
                o_ref, sfin_ref, qs, ks, vs, oacc):
    bpp, seq_len = q_ref.shape[0], q_ref.shape[1]
    nchunks = seq_len // CHUNK
    h = pl.program_id(1)

    def prep(c, _):
        r0 = pl.multiple_of(c * CHUNK, CHUNK)
        for bb in range(bpp):
            for src, cw, dst, kind in ((q_ref, cwq_ref, qs, "q"), (k_ref, cwk_ref, ks, "k"), (v_ref, cwv_ref, vs, "v")):
                y = _silu(_conv3(src, r0, c, nchunks, seq_len, cw, lead=(bb,)))
                if kind != "v":
                    y = y * lax.rsqrt(jnp.sum(y * y, axis=-1, keepdims=True) + EPS)
                if kind == "q":
                    y = y * HEAD ** -0.5
                dst[bb, pl.ds(r0, CHUNK), :] = y
        return 0

    lax.fori_loop(0, nchunks, prep, 0, unroll=2)

    ri = _iota((CHUNK, CHUNK), 0)
    ci = _iota((CHUNK, CHUNK), 1)
    neg_a = -jnp.exp(par_ref[0:1, :])
    dt_bias = par_ref[1:2, :]

    def load(bb, c):
        r0 = pl.multiple_of(c * CHUNK, CHUNK)
        return (qs[bb, pl.ds(r0, CHUNK), :], ks[bb, pl.ds(r0, CHUNK), :], vs[bb, pl.ds(r0, CHUNK), :],
                g_ref[bb, pl.ds(r0, CHUNK), :])

    def gate_terms(loaded, d):
        q, k, v, gates = loaded
        beta = _lane_col(jax.nn.sigmoid(gates), d * GDN_HEADS + h)
        g_t = neg_a * _softplus(gates + dt_bias)
        gc_t = _cumsum_rows(g_t)
        tot_t = jnp.broadcast_to(gc_t[CHUNK - 1:CHUNK, :], gc_t.shape)
        if d == 1:
            gc_t = tot_t - gc_t + g_t
        gidx = 2 * GDN_HEADS + d * GDN_HEADS + h
        gc = _lane_col(gc_t, gidx)
        tot = _lane_col(tot_t, gidx)
        incl = (ri >= ci) if d == 0 else (ri <= ci)
        decay = jnp.exp(jnp.where(incl, gc - _row_form(gc), NEG))
        egc = jnp.exp(gc)
        kb = k * beta
        return dict(q=q, k=k, kb=kb, incl=incl, strict=(ri > ci) if d == 0 else (ri < ci), decay=decay,
                    x=jnp.concatenate([v * beta, kb * egc], axis=-1), q_dec=q * egc,
                    k_dec=k * jnp.exp(tot - gc), etot=jnp.exp(tot))

    unroll = min(GDN_UNROLL, nchunks)
    nsteps = nchunks // unroll

    def chunk_ids(i):
        cf = [i * unroll + j for j in range(unroll)]
        return cf, [nchunks - 1 - c for c in cf]

    def solve(i):
        cf, cb = chunk_ids(i)
        t = [gate_terms(load(bb, c), d) for bb in range(bpp) for c, d in zip(cf + cb, [0] * unroll + [1] * unroll)]
        kk = [_dot_nt(ch["kb"], ch["k"]) for ch in t]
        qk = [_dot_nt(ch["q"], ch["k"]) for ch in t]
        yield
        a = [jnp.where(ch["strict"], m * ch["decay"], 0.0) for ch, m in zip(t, kk)]
        qk = [jnp.where(ch["incl"], m * ch["decay"], 0.0).astype(BF16) for ch, m in zip(t, qk)]
        inv = yield from _unit_tri_inverse_stages(a, ri, ci)
        x = _dots_many(inv, [ch["x"] for ch in t], TRI_PASSES[2])
        yield
        return tuple((x[n][:, :HEAD], x[n][:, HEAD:].astype(BF16), qk[n], t[n]["q_dec"].astype(BF16),
                      jnp.transpose(t[n]["k_dec"]).astype(BF16), jnp.broadcast_to(t[n]["etot"], (CHUNK, HEAD)))
                     for n in range(2 * unroll * bpp))

    def scan(i, pre, states):
        cf, cb = chunk_ids(i)
        states = list(states)
        outs = []
        for j in range(unroll):
            chains = [pre[bb * 2 * unroll + d * unroll + j] for bb in range(bpp) for d in range(2)]
            sbf = [s.astype(BF16) for s in states]
            ws = [jnp.dot(ch[1], s, preferred_element_type=F32) for ch, s in zip(chains, sbf)]
            o1 = [jnp.dot(ch[3], s, preferred_element_type=F32) for ch, s in zip(chains, sbf)]
            yield
            v_new = [(ch[0] - w).astype(BF16) for ch, w in zip(chains, ws)]
            o2 = [jnp.dot(ch[2], vn, preferred_element_type=F32) for ch, vn in zip(chains, v_new)]
            kv = [jnp.dot(ch[4], vn, preferred_element_type=F32) for ch, vn in zip(chains, v_new)]
            yield
            states = [s * ch[5] + m for ch, s, m in zip(chains, states, kv)]
            outs.append([a1 + a2 for a1, a2 in zip(o1, o2)])
        for j in range(unroll):
            for bb in range(bpp):
                oacc[bb, 0, pl.ds(pl.multiple_of(cf[j] * CHUNK, CHUNK), CHUNK), :] = outs[j][2 * bb]
                oacc[bb, 1, pl.ds(pl.multiple_of(cb[j] * CHUNK, CHUNK), CHUNK), :] = outs[j][2 * bb + 1]
        return tuple(states)

    def step(i, carry):
        pre, states = carry
        nxt, states = _run_interleaved(solve(i + 1), scan(i, pre, states))
        return nxt, states

    pre, _ = _run_interleaved(solve(0), iter(()))
    states = tuple(s0_ref[bb, d, 0] for bb in range(bpp) for d in range(2))
    pre, states = lax.fori_loop(0, nsteps - 1, step, (pre, states))
    _, states = _run_interleaved(iter(()), scan(nsteps - 1, pre, states))
    for bb in range(bpp):
        sfin_ref[bb, 0, 0] = states[2 * bb]
        sfin_ref[bb, 1, 0] = states[2 * bb + 1]

    def fin(c, _):
        r0 = pl.multiple_of(c * CHUNK, CHUNK)
        for bb in range(bpp):
            o = oacc[bb, 0, pl.ds(r0, CHUNK), :] + oacc[bb, 1, pl.ds(r0, CHUNK), :]
            y = _rms(o, nrm_ref[...]) * _silu(z_ref[bb, pl.ds(r0, CHUNK), :])
            o_ref[bb, pl.ds(r0, CHUNK), :] = y.astype(o_ref.dtype)
        return 0

    lax.fori_loop(0, nchunks, fin, 0, unroll=2)


def _gdn(x, gates, conv_w, par, norm_w, s0, layer, bpp):
    bsz, seq_len, _ = x.shape
    col = lambda off: pl.BlockSpec((bpp, seq_len, HEAD), lambda b, h: (b, 0, off + h))
    cw = lambda off: pl.BlockSpec((3, HEAD), lambda b, h: (0, off + h))
    st = pl.BlockSpec((bpp, 2, 1, HEAD, HEAD), lambda b, h: (b, 0, h, 0, 0))
    st_in = pl.BlockSpec((bpp, None, 2, 1, HEAD, HEAD), lambda b, h: (b, layer, 0, h, 0, 0))
    return pl.pallas_call(
        _gdn_kernel,
        grid=(bsz // bpp, GDN_HEADS),
        in_specs=[col(0), col(GDN_HEADS), col(2 * GDN_HEADS), col(3 * GDN_HEADS),
                  pl.BlockSpec((bpp, seq_len, HEAD), lambda b, h: (b, 0, 0)),
                  cw(0), cw(GDN_HEADS), cw(2 * GDN_HEADS),
                  pl.BlockSpec((8, HEAD), lambda b, h: (0, 0)),
                  pl.BlockSpec((1, HEAD), lambda b, h: (0, 0)),
                  st_in],
        out_specs=[pl.BlockSpec((bpp, seq_len, HEAD), lambda b, h: (b, 0, h)), st],
        out_shape=[jax.ShapeDtypeStruct((bsz, seq_len, GDN_HEADS * HEAD), BF16),
                   jax.ShapeDtypeStruct((bsz, 2, GDN_HEADS, HEAD, HEAD), F32)],
        scratch_shapes=[pltpu.VMEM((bpp, seq_len, HEAD), F32)] * 3 + [pltpu.VMEM((bpp, 2, seq_len, HEAD), F32)],
        compiler_params=_cparams("parallel", "parallel"),
        name="gdn",
    )(x, x, x, x, gates, conv_w, conv_w, conv_w, par, norm_w.reshape(1, HEAD), s0)


def _ssd_kernel(x_ref, bm_ref, cm_ref, z_ref, dt_ref, cwx_ref, cwb_ref, cwc_ref, cbx_ref, cbb_ref, cbc_ref,
                par_ref, dsk_ref, h0_ref, y_ref, hfin_ref, xs, bs, cs):
    seq_len = x_ref.shape[1]
    nchunks = seq_len // CHUNK
    width = SSD_HPG * SSD_P

    def prep(c, _):
        r0 = pl.multiple_of(c * CHUNK, CHUNK)
        for src, cw, cb, dst in ((x_ref, cwx_ref, cbx_ref, xs), (bm_ref, cwb_ref, cbb_ref, bs),
                                 (cm_ref, cwc_ref, cbc_ref, cs)):
            dst[pl.ds(r0, CHUNK), :] = _silu(_conv3(src, r0, c, nchunks, seq_len, cw, lead=(0,)) + cb[...])
        return 0

    lax.fori_loop(0, nchunks, prep, 0)

    ri = _iota((CHUNK, CHUNK), 0)
    ci = _iota((CHUNK, CHUNK), 1)
    head_of_lane = jnp.right_shift(_iota((CHUNK, width), 1), 6)
    neg_a = -jnp.exp(par_ref[0:1, :])
    dt_bias = par_ref[1:2, :]

    def expand(cols):
        out = jnp.broadcast_to(cols[0], (CHUNK, width))
        for e in range(1, SSD_HPG):
            out = jnp.where(head_of_lane == e, cols[e], out)
        return out

    def gate_terms(c, d):
        r0 = pl.multiple_of(c * CHUNK, CHUNK)
        x, bm, cm = xs[pl.ds(r0, CHUNK), :], bs[pl.ds(r0, CHUNK), :], cs[pl.ds(r0, CHUNK), :]
        dt_t = _softplus(dt_ref[0, pl.ds(r0, CHUNK), :] + dt_bias)
        la_t = dt_t * neg_a
        cs_t = _cumsum_rows(la_t)
        tot_row = cs_t[CHUNK - 1:CHUNK, :]
        tot_t = jnp.broadcast_to(tot_row, cs_t.shape)
        if d == 1:
            cs_t = tot_t - cs_t + la_t
        incl = (ri >= ci) if d == 0 else (ri <= ci)
        cs_rows = jnp.transpose(cs_t)
        rest_t = tot_t - cs_t
        tot_p = jnp.broadcast_to(tot_row, (SSD_P, HEAD))
        lanes = [d * SSD_HPG + e for e in range(SSD_HPG)]
        col = _lane_col
        lmats = [jnp.exp(jnp.where(incl, col(cs_t, k) - cs_rows[k:k + 1, :], NEG)) for k in lanes]
        xdt = x * expand([col(dt_t, k) for k in lanes])
        cs_full = expand([col(cs_t, k) for k in lanes])
        cd = jnp.concatenate([jnp.broadcast_to(jnp.exp(col(tot_p, k)), (SSD_P, HEAD)) for k in lanes], axis=0)
        return dict(r0=r0, x=x, bm=bm, cm=cm, xdt=xdt, lmats=lmats, ecs=jnp.exp(cs_full), cd=cd,
                    xdec_t=jnp.transpose(xdt * jnp.exp(expand([col(rest_t, k) for k in lanes]))))

    unroll = min(SSD_UNROLL, nchunks)

    def direction(d, h0, finish):
        def step(i, hstate):
            cidx = [i * unroll + j for j in range(unroll)]
            if d == 1:
                cidx = [nchunks - 1 - c for c in cidx]
            t = [gate_terms(c, d) for c in cidx]
            cb = [_dot_nt(ch["cm"], ch["bm"]) for ch in t]
            states = [_dot(ch["xdec_t"], ch["bm"]) for ch in t]
            ys = []
            for ch, m in zip(t, cb):
                y = None
                for e in range(SSD_HPG):
                    part = _dot(m * ch["lmats"][e], jnp.where(head_of_lane == e, ch["xdt"], 0.0))
                    y = part if y is None else y + part
                ys.append(y)
            for j, ch in enumerate(t):
                ys[j] = ys[j] + _dot_nt(ch["cm"], hstate) * ch["ecs"]
                hstate = hstate * ch["cd"] + states[j]
            for ch, y in zip(t, ys):
                finish(ch, y)
            return hstate

        return lax.fori_loop(0, nchunks // unroll, step, h0.reshape(width, HEAD))

    def store_fwd(ch, y):
        y_ref[0, pl.ds(ch["r0"], CHUNK), :] = y

    def store_bwd(ch, y):
        tot = y_ref[0, pl.ds(ch["r0"], CHUNK), :] + y + ch["x"] * dsk_ref[...]
        y_ref[0, pl.ds(ch["r0"], CHUNK), :] = tot * _silu(z_ref[0, pl.ds(ch["r0"], CHUNK), :])

    hfin_ref[0, 0] = direction(0, h0_ref[0, 0], store_fwd).reshape(SSD_HPG, SSD_P, HEAD)
    hfin_ref[0, 1] = direction(1, h0_ref[0, 1], store_bwd).reshape(SSD_HPG, SSD_P, HEAD)


def _ssd(x, gates, conv_w, conv_b, par, d_skip, h0, layer):
    bsz, seq_len, _ = x.shape
    width = SSD_HPG * SSD_P
    spec = lambda wd, off: pl.BlockSpec((1, seq_len, wd), lambda b, g: (b, 0, off + g))
    cws = lambda rows, wd, off: pl.BlockSpec((rows, wd), lambda b, g: (0, off + g))
    st = pl.BlockSpec((1, 2, SSD_HPG, SSD_P, HEAD), lambda b, g: (b, 0, g, 0, 0))
    st_in = pl.BlockSpec((1, None, 2, SSD_HPG, SSD_P, HEAD), lambda b, g: (b, layer, 0, g, 0, 0))
    return pl.pallas_call(
        _ssd_kernel,
        grid=(bsz, SSD_GROUPS),
        in_specs=[spec(width, 0), spec(HEAD, 4), spec(HEAD, 6), spec(width, 4),
                  pl.BlockSpec((1, seq_len, HEAD), lambda b, g: (b, 0, g)),
                  cws(3, width, 0), cws(3, HEAD, 4), cws(3, HEAD, 6),
                  cws(1, width, 0), cws(1, HEAD, 4), cws(1, HEAD, 6),
                  pl.BlockSpec((8, HEAD), lambda b, g: (0, g)),
                  cws(1, width, 0),
                  st_in],
        out_specs=[pl.BlockSpec((1, seq_len, width), lambda b, g: (b, 0, g)), st],
        out_shape=[jax.ShapeDtypeStruct((bsz, seq_len, SSD_HEADS * SSD_P), F32),
                   jax.ShapeDtypeStruct((bsz, 2, SSD_HEADS, SSD_P, HEAD), F32)],
        scratch_shapes=[pltpu.VMEM((seq_len, width), F32), pltpu.VMEM((seq_len, HEAD), F32),
                        pltpu.VMEM((seq_len, HEAD), F32)],
        compiler_params=_cparams("parallel", "parallel"),
        name="ssd",
    )(x, x, x, x, gates, conv_w, conv_w, conv_w, conv_b, conv_b, conv_b, par, d_skip, h0)


def _rope_tables(n_tokens):
    half = HEAD // 2
    inv_freq = ROPE_THETA ** (-jnp.arange(0, half, 2, dtype=F32) / half)
    t = jnp.arange(n_tokens)
    ang_r = (t // GRID_W).astype(F32)[:, None] * inv_freq
    ang_c = (t % GRID_W).astype(F32)[:, None] * inv_freq
    cos = jnp.concatenate([jnp.cos(ang_r)] * 2 + [jnp.cos(ang_c)] * 2, axis=-1)
    sin = jnp.concatenate([-jnp.sin(ang_r), jnp.sin(ang_r), -jnp.sin(ang_c), jnp.sin(ang_c)], axis=-1)
    return cos, sin


def _gate_rows(a_log, dt_bias, offset):
    n = a_log.size
    rows = jnp.zeros((8, HEAD), F32)
    rows = rows.at[0, offset:offset + n].set(a_log.reshape(-1).astype(F32))
    return rows.at[1, offset:offset + n].set(dt_bias.reshape(-1).astype(F32))


def _ssd_group_lanes(t):
    lead = t.shape[:-1]
    t = t.reshape(lead + (2, SSD_GROUPS, SSD_HPG))
    t = jnp.swapaxes(t, -3, -2).reshape(lead + (SSD_GROUPS, 2 * SSD_HPG))
    pad = [(0, 0)] * (t.ndim - 1) + [(0, HEAD - 2 * SSD_HPG)]
    return jnp.pad(t, pad).reshape(lead + (SSD_GROUPS * HEAD,))


def _pad_cols(w, width):
    return jnp.pad(w, ((0, 0), (0, width - w.shape[1])))


def kernel(x_prompt, x_sample, state_gdn, cache_gqa_k, cache_gqa_v, state_ssd, cache_na_k, cache_na_v, c, c_ctx, ada_w, ada_b, norm_mix_pre, norm_mix_post, norm_mlp_pre, norm_mlp_post, mlp_w1, mlp_w2, ev_w_in, ev_w_out, gdn_conv, gdn_a_log, gdn_dt_bias, gdn_norm, gqa_q_norm, gqa_k_norm, od_w_in, od_w_out, ssd_conv, ssd_conv_b, ssd_a_log, ssd_dt_bias, ssd_d, ssd_norm, na_rpb):
    nb, ns, _ = x_prompt.shape
    db, dl, _ = x_sample.shape
    past = cache_gqa_k.shape[2]

    c16 = jnp.zeros((16, D_MODEL), F32).at[:db].set(c).at[db].set(c_ctx)
    mods = _ada(c16, ada_w, ada_b).reshape(DEPTH, 16, 6, D_MODEL)

    xp = x_prompt.reshape(1, nb * ns, D_MODEL)
    xs = x_sample
    tm_c = min(1024, nb * ns)
    tm_in = min(512, nb * ns, dl)
    cos_l, sin_l = _rope_tables(dl)
    cos_c, sin_c = jnp.ones((tm_in, HEAD), F32), jnp.zeros((tm_in, HEAD), F32)
    ctx_bpp = max(n for n in (4, 2, 1) if nb % n == 0)
    zero_gdn = jnp.zeros((nb, 1, 2, GDN_HEADS, HEAD, HEAD), F32)
    zero_ssd = jnp.zeros((nb, 1, 2, SSD_HEADS, SSD_P, HEAD), F32)
    new_gdn, new_gk, new_gv, new_ssd, new_nk, new_nv = [], [], [], [], [], []

    for i in range(DEPTH):
        j = i // 2
        mod_l, mod_c = mods[i, :db], mods[i, db:db + 1]
        if i % 2 == 0:
            wi = ev_w_in[j]
            w_in = jnp.concatenate([wi[:, :2048], wi[:, 2064:], _pad_cols(wi[:, 2048:2064], HEAD)], axis=1).astype(BF16)
            widths = (2048, 1024, HEAD)
            w_out = ev_w_out[j].astype(BF16)
            par = _gate_rows(gdn_a_log[j], gdn_dt_bias[j], 2 * GDN_HEADS)

            attn_w = (GQA_HEADS * HEAD, GQA_KV_HEADS * HEAD, GQA_KV_HEADS * HEAD)
            outs = _inproj(xp, mod_c, norm_mix_pre[i], w_in, widths, tm_in, attn_w, True,
                           (cos_c, sin_c, gqa_q_norm[j], gqa_k_norm[j], True))
            gx, gt, q, k, v, kf, vf = (t.reshape(nb, ns, -1) for t in outs)
            o_a, s_a = _gdn(gx, gt, gdn_conv[j], par, gdn_norm[j], zero_gdn, 0, ctx_bpp)
            o_b = _flash(q, k, v, GQA_KV_HEADS, 2, ns, ns)
            yc = _outproj(o_a.reshape(1, nb * ns, -1), o_b.reshape(1, nb * ns, -1), xp, mod_c,
                          w_out, norm_mix_post[i], tm_c)
            new_gdn.append(s_a)
            new_gk.append(kf.reshape(nb, ns, GQA_KV_HEADS, HEAD))
            new_gv.append(vf.reshape(nb, ns, GQA_KV_HEADS, HEAD))

            gx, gt, q, k, v = _inproj(xs, mod_l, norm_mix_pre[i], w_in, widths, tm_in, attn_w, False,
                                      (cos_l, sin_l, gqa_q_norm[j], gqa_k_norm[j], False))
            o_a, _ = _gdn(gx, gt, gdn_conv[j], par, gdn_norm[j], state_gdn, j, 1)
            o_b = _flash(q, k, v, GQA_KV_HEADS, 2, 512, min(2048, dl),
                         cache=(cache_gqa_k.reshape(db, -1, past, GQA_KV_HEADS * HEAD),
                                cache_gqa_v.reshape(db, -1, past, GQA_KV_HEADS * HEAD), j))
            yl = _outproj(o_a, o_b, xs, mod_l, w_out, norm_mix_post[i], 512)
        else:
            wi = od_w_in[j]
            w_in = jnp.concatenate([wi[:, 512:1536], wi[:, :512], wi[:, 1552:], _ssd_group_lanes(wi[:, 1536:1552])], axis=1).astype(BF16)
            widths = (1536, 1536, SSD_GROUPS * HEAD)
            w_out = od_w_out[j].astype(BF16)
            par = jnp.zeros((8, SSD_GROUPS * HEAD), F32)
            par = par.at[0].set(_ssd_group_lanes(ssd_a_log[j].reshape(-1).astype(F32)))
            par = par.at[1].set(_ssd_group_lanes(ssd_dt_bias[j].reshape(-1).astype(F32)))
            dsk = jnp.repeat(ssd_d[j].astype(F32), SSD_P).reshape(1, -1)
            cvw, cvb = ssd_conv[j], ssd_conv_b[j].reshape(1, -1)

            attn_w = (NA_HEADS * HEAD,) * 3
            outs = _inproj(xp, mod_c, norm_mix_pre[i], w_in, widths, tm_in, attn_w, True)
            sx, gt, q, k, v, kf, vf = (t.reshape(nb, ns, -1) for t in outs)
            y_c, s_c = _ssd(sx, gt, cvw, cvb, par, dsk, zero_ssd, 0)
            o_d = _flash(q, k, v, NA_HEADS, 1, ns, ns)
            yc = _outproj(y_c.reshape(1, nb * ns, -1), o_d.reshape(1, nb * ns, -1), xp, mod_c,
                          w_out, norm_mix_post[i], tm_c, ssd_norm[j])
            new_ssd.append(s_c)
            new_nk.append(kf.reshape(nb, ns, NA_HEADS, HEAD))
            new_nv.append(vf.reshape(nb, ns, NA_HEADS, HEAD))

            sx, gt, q, k, v = _inproj(xs, mod_l, norm_mix_pre[i], w_in, widths, tm_in, attn_w, False)
            y_c, _ = _ssd(sx, gt, cvw, cvb, par, dsk, state_ssd, j)
            o_d = _natten(q, k, v, cache_na_k.reshape(db, -1, past, NA_HEADS * HEAD),
                          cache_na_v.reshape(db, -1, past, NA_HEADS * HEAD), j, _natten_bias(na_rpb[j]))
            yl = _outproj(y_c, o_d, xs, mod_l, w_out, norm_mix_post[i], 512, ssd_norm[j])
        xp, xs = yc, yl
        w1, w2 = mlp_w1[i].astype(BF16), mlp_w2[i].astype(BF16)
        xp = _mlp(xp, mod_c, norm_mlp_pre[i], w1, w2, norm_mlp_post[i], tm_c, 1024)
        xs = _mlp(xs, mod_l, norm_mlp_pre[i], w1, w2, norm_mlp_post[i], 1024, 1024)

    return (xp.reshape(nb, ns, D_MODEL), xs, jnp.stack(new_gdn, axis=1), jnp.stack(new_gk, axis=1),
            jnp.stack(new_gv, axis=1), jnp.stack(new_ssd, axis=1), jnp.stack(new_nk, axis=1),
            jnp.stack(new_nv, axis=1))
```

```python
import functools

import numpy as np
import jax
import jax.numpy as jnp
from jax import lax
from jax.experimental import pallas as pl
from jax.experimental.pallas import tpu as pltpu

F32 = jnp.float32
BF16 = jnp.bfloat16

D_MODEL = 1024
DEPTH = 4
GRID_W = 64
EPS = 1e-6
ROPE_THETA = 10000.0
D_FF = 4 * D_MODEL
HEAD = 128
GDN_HEADS = 4
GQA_HEADS = 4
GQA_KV_HEADS = 2
SSD_HEADS = 8
SSD_P = 64
SSD_GROUPS = 2
SSD_HPG = SSD_HEADS // SSD_GROUPS
NA_HEADS = 4
NA_WIN_R = 8
NA_WIN_C = 16
CHUNK = 128
GDN_UNROLL = 4
SSD_UNROLL = 4
TRI_PASSES = (1, 1, 2)
NEG = -1e30
QK_SCALE_LOG2 = HEAD ** -0.5 * float(np.log2(np.e))
FLASH_SAFE_BOUND = 60.0
FLASH_SHIFT = 64.0
VMEM_LIMIT_BYTES = 56 * 1024 * 1024

TM_INPROJ = 512
TM_OUTPROJ = 512
TM_DENSE_CTX = 1024
TM_MLP = 1024
TF_MLP = 1024
ADA_TN = 1536
FLASH_TQ = 512
FLASH_TK = 2048


def _cparams(*sem):
    return pltpu.CompilerParams(dimension_semantics=sem, vmem_limit_bytes=VMEM_LIMIT_BYTES)


def _dot(a, b):
    return jnp.dot(a.astype(BF16), b.astype(BF16), preferred_element_type=F32)


def _dot_nt(a, b):
    return lax.dot_general(a.astype(BF16), b.astype(BF16), (((1,), (1,)), ((), ())),
                           preferred_element_type=F32)


def _split_bf16(a):
    hi = a.astype(BF16)
    lo = (a - hi.astype(F32)).astype(BF16)
    return hi, lo


def _silu(x):
    return x * jax.nn.sigmoid(x)


def _softplus(x):
    return jnp.maximum(x, 0.0) + jnp.log(1.0 + jnp.exp(-jnp.abs(x)))


def _rms(x, w):
    return x * lax.rsqrt(jnp.mean(x * x, axis=-1, keepdims=True) + EPS) * w


def _iota(shape, axis):
    return lax.broadcasted_iota(jnp.int32, shape, axis)


def _lane_col(tile, idx):
    return jnp.sum(jnp.where(_iota(tile.shape, 1) == idx, tile, 0.0), axis=-1, keepdims=True)


def _cumsum_rows(t):
    rows = _iota(t.shape, 0)
    s = 1
    while s < t.shape[0]:
        t = t + jnp.where(rows >= s, pltpu.roll(t, s, 0), 0.0)
        s *= 2
    return t


def _row_form(col):
    n = col.shape[0]
    return jnp.transpose(jnp.broadcast_to(col, (n, n)))


def _dot3_many(xs, ys):
    xs = [_split_bf16(x) for x in xs]
    ys = [_split_bf16(y) for y in ys]
    d = lambda x, y: jnp.dot(x, y, preferred_element_type=F32)
    hh = [d(x[0], y[0]) for x, y in zip(xs, ys)]
    hl = [d(x[0], y[1]) for x, y in zip(xs, ys)]
    lh = [d(x[1], y[0]) for x, y in zip(xs, ys)]
    return [a + (b + c) for a, b, c in zip(hh, hl, lh)]


def _dots_many(xs, ys, passes):
    if passes == 3:
        return _dot3_many(xs, ys)
    if passes == 2:
        d = lambda x, y: jnp.dot(x, y, preferred_element_type=F32)
        xs = [x.astype(BF16) for x in xs]
        ys = [_split_bf16(y) for y in ys]
        hi = [d(x, y[0]) for x, y in zip(xs, ys)]
        lo = [d(x, y[1]) for x, y in zip(xs, ys)]
        return [a + b for a, b in zip(hi, lo)]
    return [_dot(x, y) for x, y in zip(xs, ys)]


def _unit_tri_inverse_stages(a_list, ri, ci):
    same = lambda sh: jnp.right_shift(ri, sh) == jnp.right_shift(ci, sh)
    eye = jnp.where(ri == ci, 1.0, 0.0)
    p = [jnp.where(same(4), a, 0.0) for a in a_list]
    m = [eye - x for x in p]
    p = _dots_many(p, p, TRI_PASSES[0])
    yield
    for last in (False, False, True):
        mp = _dots_many(m, p, TRI_PASSES[0])
        if not last:
            p = _dots_many(p, p, TRI_PASSES[0])
        yield
        m = [x + y for x, y in zip(m, mp)]
    for sh in (4, 5, 6):
        off = [jnp.where(same(sh + 1), jnp.where(same(sh), 0.0, a), 0.0) for a in a_list]
        t = _dots_many(off, m, TRI_PASSES[1])
        yield
        mt = _dots_many(m, t, TRI_PASSES[1])
        yield
        m = [x - y for x, y in zip(m, mt)]
    return m


def _run_interleaved(gen_a, gen_b):
    results, live = [None, None], [gen_a, gen_b]
    while any(g is not None for g in live):
        for n, g in enumerate(live):
            if g is None:
                continue
            try:
                next(g)
            except StopIteration as stop:
                results[n], live[n] = stop.value, None
    return results


def _conv3(ref, r0, c, nchunks, seq_len, w, lead=()):
    x = ref[lead + (pl.ds(r0, CHUNK), slice(None))]
    prev = ref[lead + (pl.ds(jnp.maximum(r0 - 1, 0), 1), slice(None))]
    nxt = ref[lead + (pl.ds(jnp.minimum(r0 + CHUNK, seq_len - 1), 1), slice(None))]
    prev = jnp.where(c == 0, 0.0, prev)
    nxt = jnp.where(c == nchunks - 1, 0.0, nxt)
    rows = _iota(x.shape, 0)
    xm = jnp.where(rows == 0, prev, pltpu.roll(x, 1, 0))
    xp = jnp.where(rows == CHUNK - 1, nxt, pltpu.roll(x, CHUNK - 1, 0))
    return xm * w[0:1, :] + x * w[1:2, :] + xp * w[2:3, :]


def _ada_kernel(c_ref, w_ref, b_ref, o_ref):
    o_ref[0] = _dot(_silu(c_ref[...]), w_ref[0]) + b_ref[0]


def _ada(c16, ada_w, ada_b):
    tn = ADA_TN
    n = ada_w.shape[-1]
    return pl.pallas_call(
        _ada_kernel,
        grid=(DEPTH, n // tn),
        in_specs=[pl.BlockSpec((16, D_MODEL), lambda i, j: (0, 0)),
                  pl.BlockSpec((1, D_MODEL, tn), lambda i, j: (i, 0, j)),
                  pl.BlockSpec((1, 1, tn), lambda i, j: (i, 0, j))],
        out_specs=pl.BlockSpec((1, 16, tn), lambda i, j: (i, 0, j)),
        out_shape=jax.ShapeDtypeStruct((DEPTH, 16, n), F32),
        compiler_params=_cparams("parallel", "parallel"),
        name="ada",
    )(c16, ada_w, ada_b.reshape(DEPTH, 1, n))


def _mod_spec(mod):
    if mod.shape[0] == 1:
        return pl.BlockSpec((1, 6, D_MODEL), lambda b, *_: (0, 0, 0))
    return pl.BlockSpec((1, 6, D_MODEL), lambda b, *_: (b, 0, 0))


def _inproj_kernel(x_ref, mod_ref, nw_ref, w_ref, *rest, splits, rope, plain_kv):
    if rope:
        cos_ref, sin_ref, qn_ref, kn_ref = rest[:4]
        rest = rest[4:]
    mix_ref, gate_ref, q_ref, k_ref, v_ref = rest[:5]
    h = _rms(x_ref[0], nw_ref[...] * (1.0 + mod_ref[0, 1:2, :])) + mod_ref[0, 0:1, :]
    hb = h.astype(BF16)
    (m0, m1), (a0, a1), (g0, g1) = splits
    y = jnp.dot(hb, w_ref[:, a0:a1], preferred_element_type=F32)
    mix_ref[0] = jnp.dot(hb, w_ref[:, m0:m1], preferred_element_type=F32)
    gate_ref[0] = jnp.dot(hb, w_ref[:, g0:g1], preferred_element_type=F32)
    nq, nk = q_ref.shape[-1], k_ref.shape[-1]
    head = lambda off, hd: y[:, off + hd * HEAD:off + (hd + 1) * HEAD]
    if rope:
        cos, sin = cos_ref[...], sin_ref[...]
        first = (_iota(cos.shape, 1) & 32) == 0

        def normed_rotated(xh, w):
            n = _rms(xh, w)
            swapped = jnp.where(first, pltpu.roll(n, HEAD - 32, 1), pltpu.roll(n, 32, 1))
            return n, n * cos + swapped * sin

        for hd in range(nq // HEAD):
            _, r = normed_rotated(head(0, hd), qn_ref[...])
            q_ref[0, :, hd * HEAD:(hd + 1) * HEAD] = (r * QK_SCALE_LOG2).astype(BF16)
        for hd in range(nk // HEAD):
            n, r = normed_rotated(head(nq, hd), kn_ref[...])
            k_ref[0, :, hd * HEAD:(hd + 1) * HEAD] = r.astype(BF16)
            if plain_kv:
                rest[5][0, :, hd * HEAD:(hd + 1) * HEAD] = n
    else:
        q_ref[0] = (y[:, :nq] * QK_SCALE_LOG2).astype(BF16)
        k_ref[0] = y[:, nq:nq + nk].astype(BF16)
        if plain_kv:
            rest[5][0] = y[:, nq:nq + nk]
    v_ref[0] = y[:, nq + nk:].astype(BF16)
    if plain_kv:
        rest[6][0] = y[:, nq + nk:]


def _inproj(x, mod, norm_w, w, widths, tm, attn_widths, plain_kv, rope=None):
    bsz, seq_len, _ = x.shape
    splits, a = [], 0
    for wd in (widths[0], widths[1], widths[2]):
        splits.append((a, a + wd))
        a += wd
    row = lambda wd: pl.BlockSpec((1, tm, wd), lambda b, t: (b, t, 0))
    in_specs = [row(D_MODEL), _mod_spec(mod), pl.BlockSpec((1, D_MODEL), lambda b, t: (0, 0)),
                pl.BlockSpec(w.shape, lambda b, t: (0, 0))]
    args = [x, mod, norm_w.reshape(1, D_MODEL), w]
    if rope is not None:
        cos, sin, qn, kn, shared = rope
        tab = (pl.BlockSpec((tm, HEAD), lambda b, t: (0, 0)) if shared
               else pl.BlockSpec((tm, HEAD), lambda b, t: (t, 0)))
        in_specs += [tab, tab, pl.BlockSpec((1, HEAD), lambda b, t: (0, 0)), pl.BlockSpec((1, HEAD), lambda b, t: (0, 0))]
        args += [cos, sin, qn.reshape(1, HEAD), kn.reshape(1, HEAD)]
    out = [(widths[0], F32), (widths[2], F32)] + [(wd, BF16) for wd in attn_widths]
    if plain_kv:
        out += [(attn_widths[1], F32), (attn_widths[2], F32)]
    return pl.pallas_call(
        functools.partial(_inproj_kernel, splits=tuple(splits), rope=rope is not None, plain_kv=plain_kv),
        grid=(bsz, seq_len // tm),
        in_specs=in_specs,
        out_specs=[row(wd) for wd, _ in out],
        out_shape=[jax.ShapeDtypeStruct((bsz, seq_len, wd), dt) for wd, dt in out],
        compiler_params=_cparams("parallel", "parallel"),
        name="inproj",
    )(*args)


def _outproj_kernel(a_ref, b_ref, x_ref, mod_ref, w_ref, nw_ref, *rest, norm_a):
    a = a_ref[0]
    if norm_a:
        na_ref, o_ref = rest
        a = _rms(a, na_ref[...])
    else:
        (o_ref,) = rest
    half = a.shape[-1]
    y = (jnp.dot(a.astype(BF16), w_ref[:half, :], preferred_element_type=F32)
         + jnp.dot(b_ref[0].astype(BF16), w_ref[half:, :], preferred_element_type=F32))
    o_ref[0] = x_ref[0] + _rms(y, nw_ref[...] * mod_ref[0, 2:3, :])


def _outproj(a, b, x, mod, w, norm_w, tm, norm_a_w=None):
    bsz, seq_len, _ = x.shape
    half = a.shape[-1]
    in_specs = [pl.BlockSpec((1, tm, half), lambda bb, t: (bb, t, 0)),
                pl.BlockSpec((1, tm, half), lambda bb, t: (bb, t, 0)),
                pl.BlockSpec((1, tm, D_MODEL), lambda bb, t: (bb, t, 0)),
                _mod_spec(mod),
                pl.BlockSpec(w.shape, lambda bb, t: (0, 0)),
                pl.BlockSpec((1, D_MODEL), lambda bb, t: (0, 0))]
    args = [a, b, x, mod, w, norm_w.reshape(1, D_MODEL)]
    if norm_a_w is not None:
        in_specs.append(pl.BlockSpec((1, half), lambda bb, t: (0, 0)))
        args.append(norm_a_w.reshape(1, half))
    return pl.pallas_call(
        functools.partial(_outproj_kernel, norm_a=norm_a_w is not None),
        grid=(bsz, seq_len // tm),
        in_specs=in_specs,
        out_specs=pl.BlockSpec((1, tm, D_MODEL), lambda bb, t: (bb, t, 0)),
        out_shape=jax.ShapeDtypeStruct(x.shape, F32),
        compiler_params=_cparams("parallel", "parallel"),
        name="outproj",
    )(*args)


def _mlp_kernel(x_ref, mod_ref, npre_ref, w1_ref, w2_ref, npost_ref, o_ref, h_ref, acc_ref):
    f = pl.program_id(2)

    @pl.when(f == 0)
    def _():
        h = _rms(x_ref[0], npre_ref[...] * (1.0 + mod_ref[0, 4:5, :])) + mod_ref[0, 3:4, :]
        h_ref[...] = h.astype(BF16)
        acc_ref[...] = jnp.zeros_like(acc_ref)

    u = jnp.maximum(jnp.dot(h_ref[...], w1_ref[...], preferred_element_type=F32), 0.0)
    acc_ref[...] += jnp.dot((u * u).astype(BF16), w2_ref[...], preferred_element_type=F32)

    @pl.when(f == pl.num_programs(2) - 1)
    def _():
        o_ref[0] = x_ref[0] + _rms(acc_ref[...], npost_ref[...] * mod_ref[0, 5:6, :])


def _mlp(x, mod, npre, w1, w2, npost, tm, tf):
    bsz, seq_len, _ = x.shape
    return pl.pallas_call(
        _mlp_kernel,
        grid=(bsz, seq_len // tm, D_FF // tf),
        in_specs=[pl.BlockSpec((1, tm, D_MODEL), lambda b, t, f: (b, t, 0)),
                  _mod_spec(mod),
                  pl.BlockSpec((1, D_MODEL), lambda b, t, f: (0, 0)),
                  pl.BlockSpec((D_MODEL, tf), lambda b, t, f: (0, f)),
                  pl.BlockSpec((tf, D_MODEL), lambda b, t, f: (f, 0)),
                  pl.BlockSpec((1, D_MODEL), lambda b, t, f: (0, 0))],
        out_specs=pl.BlockSpec((1, tm, D_MODEL), lambda b, t, f: (b, t, 0)),
        out_shape=jax.ShapeDtypeStruct(x.shape, F32),
        scratch_shapes=[pltpu.VMEM((tm, D_MODEL), BF16), pltpu.VMEM((tm, D_MODEL), F32)],
        compiler_params=_cparams("parallel", "parallel", "arbitrary"),
        name="mlp",
    )(x, mod, npre.reshape(1, D_MODEL), w1, w2, npost.reshape(1, D_MODEL))


def _flash_kernel(q_ref, k_ref, v_ref, *rest, rep, tk, with_cache):
    if with_cache:
        kc_ref, vc_ref, o_ref, knorm_ref = rest
    else:
        o_ref, knorm_ref = rest
    tq = q_ref.shape[1]
    if rep == 2:
        qs = [q_ref[0, :, r * HEAD:(r + 1) * HEAD] for r in range(rep)]
    else:
        qs = [q_ref[0, :tq // 2, :], q_ref[0, tq // 2:, :]]
    nk = k_ref.shape[1] // tk
    rows = qs[0].shape[0]

    def lane_chunks(s):
        return [s[:, j * HEAD:(j + 1) * HEAD] for j in range(s.shape[1] // HEAD)]

    def row_max(k, part):
        s = [_dot_nt(q, k) for q in qs]
        return tuple(functools.reduce(jnp.maximum, lane_chunks(sc), m) for sc, m in zip(s, part))

    def accumulate(k, v, mx, carry):
        s = [_dot_nt(q, k) for q in qs]
        ps, ls = [], []
        for sc, m, (l, _) in zip(s, mx, carry):
            pc = [jnp.exp2(c - m) for c in lane_chunks(sc)]
            ls.append(functools.reduce(jnp.add, pc, l))
            ps.append(jnp.concatenate([c.astype(BF16) for c in pc], axis=-1))
        pv = [jnp.dot(p, v.astype(BF16), preferred_element_type=F32) for p in ps]
        return tuple((l, acc + x) for l, (_, acc), x in zip(ls, carry, pv))

    def block(ref, i):
        return ref[0, pl.ds(pl.multiple_of(i * tk, tk), tk), :]

    def max_sq_norm(k, best):
        kf = k.astype(BF16).astype(F32)
        sq = jnp.broadcast_to(jnp.sum(kf * kf, axis=-1, keepdims=True), kf.shape)
        return jnp.maximum(best, jnp.max(sq, axis=0, keepdims=True))

    @pl.when(pl.program_id(2) == 0)
    def _():
        best = lax.fori_loop(0, nk, lambda i, b: max_sq_norm(block(k_ref, i), b), jnp.zeros((1, HEAD), F32))
        if with_cache:
            best = max_sq_norm(kc_ref[0], best)
        knorm_ref[...] = jnp.sqrt(best)

    def exact_row_max():
        part = tuple(jnp.full((rows, HEAD), NEG, F32) for _ in qs)
        part = lax.fori_loop(0, nk, lambda i, p: row_max(block(k_ref, i), p), part)
        if with_cache:
            part = row_max(kc_ref[0], part)
        return [jnp.broadcast_to(jnp.max(m, axis=-1, keepdims=True), (rows, HEAD)) for m in part]

    bounds = []
    for q in qs:
        qf = q.astype(F32)
        qn = jnp.sqrt(jnp.sum(qf * qf, axis=-1, keepdims=True))
        bounds.append(jnp.broadcast_to(qn, (rows, HEAD)) * knorm_ref[...])
    safe = jnp.max(functools.reduce(jnp.maximum, bounds)) <= FLASH_SAFE_BOUND
    mx = lax.cond(safe, lambda: [b - FLASH_SHIFT for b in bounds], exact_row_max)

    stats = tuple((jnp.zeros((rows, HEAD), F32), jnp.zeros((rows, HEAD), F32)) for _ in qs)
    stats = lax.fori_loop(0, nk, lambda i, c: accumulate(block(k_ref, i), block(v_ref, i), mx, c), stats)
    if with_cache:
        stats = accumulate(kc_ref[0], vc_ref[0], mx, stats)
    outs = [(acc / jnp.sum(l, axis=-1, keepdims=True)).astype(o_ref.dtype) for l, acc in stats]
    if rep == 2:
        for r in range(rep):
            o_ref[0, :, r * HEAD:(r + 1) * HEAD] = outs[r]
    else:
        o_ref[0, :tq // 2, :] = outs[0]
        o_ref[0, tq // 2:, :] = outs[1]


def _flash(q, k, v, kv_heads, rep, tq, tk, cache=None):
    bsz, seq_len, _ = q.shape
    assert rep in (1, 2)
    m_len = k.shape[1]
    in_specs = [pl.BlockSpec((1, tq, rep * HEAD), lambda b, g, t: (b, t, g)),
                pl.BlockSpec((1, m_len, HEAD), lambda b, g, t: (b, 0, g)),
                pl.BlockSpec((1, m_len, HEAD), lambda b, g, t: (b, 0, g))]
    args = [q, k, v]
    if cache is not None:
        layer = cache[2]
        spec = pl.BlockSpec((1, None, cache[0].shape[2], HEAD), lambda b, g, t: (b, layer, 0, g))
        in_specs += [spec, spec]
        args += [cache[0], cache[1]]
    return pl.pallas_call(
        functools.partial(_flash_kernel, rep=rep, tk=tk, with_cache=cache is not None),
        grid=(bsz, kv_heads, seq_len // tq),
        in_specs=in_specs,
        out_specs=pl.BlockSpec((1, tq, rep * HEAD), lambda b, g, t: (b, t, g)),
        out_shape=jax.ShapeDtypeStruct((bsz, seq_len, kv_heads * rep * HEAD), BF16),
        scratch_shapes=[pltpu.VMEM((1, HEAD), F32)],
        compiler_params=_cparams("parallel", "parallel", "arbitrary"),
        name="flash",
    )(*args)


NA_GROUP = 4
NA_BAND = NA_GROUP + NA_WIN_R
NA_STEP = 2


def _natten_kernel(q_ref, k_ref, v_ref, kc_ref, vc_ref, bias_ref, o_ref):
    rows = q_ref.shape[1] // GRID_W
    gq = NA_GROUP * GRID_W
    kc = kc_ref[0].astype(BF16)
    vc = vc_ref[0].astype(BF16)
    low_half = _iota((GRID_W, HEAD), 1) < GRID_W

    def group_bias(r_first, band_first):
        per_row = []
        for i in range(NA_GROUP):
            r = r_first + i
            r0 = jnp.clip(r - NA_WIN_R // 2, 0, rows - NA_WIN_R)
            blocks = []
            for jj in range(NA_BAND // 2):
                kr = band_first + 2 * jj
                blk = bias_ref[0, jnp.clip(kr - r + NA_WIN_R, 0, 2 * NA_WIN_R - 1)]
                ok_a = jnp.logical_and(kr >= r0, kr < r0 + NA_WIN_R)
                ok_b = jnp.logical_and(kr + 1 >= r0, kr + 1 < r0 + NA_WIN_R)
                blocks.append(jnp.where(low_half, jnp.where(ok_a, blk, NEG), jnp.where(ok_b, blk, NEG)))
            per_row.append(jnp.concatenate(blocks, axis=-1))
        return jnp.concatenate(per_row, axis=0)

    def body(it, _):
        firsts = [(it * NA_STEP + g) * NA_GROUP for g in range(NA_STEP)]
        bands = [jnp.clip(r - NA_WIN_R // 2, 0, rows - NA_BAND) for r in firsts]
        q0 = [pl.multiple_of(r * GRID_W, gq) for r in firsts]
        k0 = [pl.multiple_of(b * GRID_W, GRID_W) for b in bands]
        qs = [q_ref[0, pl.ds(a, gq), :] for a in q0]
        ks = [k_ref[0, pl.ds(a, NA_BAND * GRID_W), :] for a in k0]
        vs = [v_ref[0, pl.ds(a, NA_BAND * GRID_W), :] for a in k0]
        bias = [group_bias(r, b) for r, b in zip(firsts, bands)]
        s_loc = [_dot_nt(q, k) for q, k in zip(qs, ks)]
        s_ctx = [_dot_nt(q, kc) for q in qs]
        s_loc = [s + b for s, b in zip(s_loc, bias)]
        m = [jnp.maximum(jnp.max(a, axis=-1, keepdims=True), jnp.max(c, axis=-1, keepdims=True))
             for a, c in zip(s_loc, s_ctx)]
        p_loc = [jnp.exp2(a - mm) for a, mm in zip(s_loc, m)]
        p_ctx = [jnp.exp2(c - mm) for c, mm in zip(s_ctx, m)]
        l = [jnp.sum(a, axis=-1, keepdims=True) + jnp.sum(c, axis=-1, keepdims=True) for a, c in zip(p_loc, p_ctx)]
        o_loc = [_dot(p, v) for p, v in zip(p_loc, vs)]
        o_ctx = [_dot(p, vc) for p in p_ctx]
        for a, ol, oc, ll in zip(q0, o_loc, o_ctx, l):
            o_ref[0, pl.ds(a, gq), :] = ((ol + oc) / ll).astype(o_ref.dtype)
        return 0

    lax.fori_loop(0, rows // (NA_GROUP * NA_STEP), body, 0)


def _natten(q, k, v, kc, vc, layer, bias):
    bsz, seq_len, _ = q.shape
    m_len = kc.shape[2]
    col = pl.BlockSpec((1, seq_len, HEAD), lambda b, h: (b, 0, h))
    ctx = pl.BlockSpec((1, None, m_len, HEAD), lambda b, h: (b, layer, 0, h))
    return pl.pallas_call(
        _natten_kernel,
        grid=(bsz, NA_HEADS),
        in_specs=[col, col, col, ctx, ctx,
                  pl.BlockSpec((1,) + bias.shape[1:], lambda b, h: (h, 0, 0, 0))],
        out_specs=pl.BlockSpec((1, seq_len, HEAD), lambda b, h: (b, 0, h)),
        out_shape=jax.ShapeDtypeStruct((bsz, seq_len, NA_HEADS * HEAD), BF16),
        compiler_params=_cparams("parallel", "parallel"),
        name="natten",
    )(q, k, v, kc, vc, bias)


def _natten_bias(rpb):
    col = np.arange(GRID_W)
    col_start = np.clip(col - NA_WIN_C // 2, 0, GRID_W - NA_WIN_C)
    kc = np.arange(GRID_W)
    valid = (kc[None, :] >= col_start[:, None]) & (kc[None, :] < col_start[:, None] + NA_WIN_C)
    coff = np.clip(kc[None, :] - col[:, None] + NA_WIN_C - 1, 0, 2 * NA_WIN_C - 2)
    log2e = float(np.log2(np.e))
    t = jnp.where(valid[None, None], rpb.astype(F32)[:, :, coff] * log2e, NEG)
    t = jnp.pad(t, ((0, 0), (1, 1), (0, 0), (0, 0)), constant_values=NEG)
    return jnp.concatenate([t[:, :-1], t[:, 1:]], axis=-1)


def _gdn_kernel(q_ref, k_ref, v_ref, z_ref, g_ref, cwq_ref, cwk_ref, cwv_ref, par_ref, nrm_ref, s0_ref,
                o_ref, sfin_ref, qs, ks, vs, oacc):
    bpp, seq_len = q_ref.shape[0], q_ref.shape[1]
    nchunks = seq_len // CHUNK
    h = pl.program_id(1)

    def prep(c, _):
        r0 = pl.multiple_of(c * CHUNK, CHUNK)
        for bb in range(bpp):
            for src, cw, dst, kind in ((q_ref, cwq_ref, qs, "q"), (k_ref, cwk_ref, ks, "k"), (v_ref, cwv_ref, vs, "v")):
                y = _silu(_conv3(src, r0, c, nchunks, seq_len, cw, lead=(bb,)))
                if kind != "v":
                    y = y * lax.rsqrt(jnp.sum(y * y, axis=-1, keepdims=True) + EPS)
                if kind == "q":
                    y = y * HEAD ** -0.5
                dst[bb, pl.ds(r0, CHUNK), :] = y
        return 0

    lax.fori_loop(0, nchunks, prep, 0, unroll=2)

    ri = _iota((CHUNK, CHUNK), 0)
    ci = _iota((CHUNK, CHUNK), 1)
    neg_a = -jnp.exp(par_ref[0:1, :])
    dt_bias = par_ref[1:2, :]

    def load(bb, c):
        r0 = pl.multiple_of(c * CHUNK, CHUNK)
        return (qs[bb, pl.ds(r0, CHUNK), :], ks[bb, pl.ds(r0, CHUNK), :], vs[bb, pl.ds(r0, CHUNK), :],
                g_ref[bb, pl.ds(r0, CHUNK), :])

    def gate_terms(loaded, d):
        q, k, v, gates = loaded
        beta = _lane_col(jax.nn.sigmoid(gates), d * GDN_HEADS + h)
        g_t = neg_a * _softplus(gates + dt_bias)
        gc_t = _cumsum_rows(g_t)
        tot_t = jnp.broadcast_to(gc_t[CHUNK - 1:CHUNK, :], gc_t.shape)
        if d == 1:
            gc_t = tot_t - gc_t + g_t
        gidx = 2 * GDN_HEADS + d * GDN_HEADS + h
        gc = _lane_col(gc_t, gidx)
        tot = _lane_col(tot_t, gidx)
        incl = (ri >= ci) if d == 0 else (ri <= ci)
        decay = jnp.exp(jnp.where(incl, gc - _row_form(gc), NEG))
        egc = jnp.exp(gc)
        kb = k * beta
        return dict(q=q, k=k, kb=kb, incl=incl, strict=(ri > ci) if d == 0 else (ri < ci), decay=decay,
                    x=jnp.concatenate([v * beta, kb * egc], axis=-1), q_dec=q * egc,
                    k_dec=k * jnp.exp(tot - gc), etot=jnp.exp(tot))

    unroll = min(GDN_UNROLL, nchunks)
    nsteps = nchunks // unroll

    def chunk_ids(i):
        cf = [i * unroll + j for j in range(unroll)]
        return cf, [nchunks - 1 - c for c in cf]

    def solve(i):
        cf, cb = chunk_ids(i)
        t = [gate_terms(load(bb, c), d) for bb in range(bpp) for c, d in zip(cf + cb, [0] * unroll + [1] * unroll)]
        kk = [_dot_nt(ch["kb"], ch["k"]) for ch in t]
        qk = [_dot_nt(ch["q"], ch["k"]) for ch in t]
        yield
        a = [jnp.where(ch["strict"], m * ch["decay"], 0.0) for ch, m in zip(t, kk)]
        qk = [jnp.where(ch["incl"], m * ch["decay"], 0.0).astype(BF16) for ch, m in zip(t, qk)]
        inv = yield from _unit_tri_inverse_stages(a, ri, ci)
        x = _dots_many(inv, [ch["x"] for ch in t], TRI_PASSES[2])
        yield
        return tuple((x[n][:, :HEAD], x[n][:, HEAD:].astype(BF16), qk[n], t[n]["q_dec"].astype(BF16),
                      jnp.transpose(t[n]["k_dec"]).astype(BF16), jnp.broadcast_to(t[n]["etot"], (CHUNK, HEAD)))
                     for n in range(2 * unroll * bpp))

    def scan(i, pre, states):
        cf, cb = chunk_ids(i)
        states = list(states)
        outs = []
        for j in range(unroll):
            chains = [pre[bb * 2 * unroll + d * unroll + j] for bb in range(bpp) for d in range(2)]
            sbf = [s.astype(BF16) for s in states]
            ws = [jnp.dot(ch[1], s, preferred_element_type=F32) for ch, s in zip(chains, sbf)]
            o1 = [jnp.dot(ch[3], s, preferred_element_type=F32) for ch, s in zip(chains, sbf)]
            yield
            v_new = [(ch[0] - w).astype(BF16) for ch, w in zip(chains, ws)]
            o2 = [jnp.dot(ch[2], vn, preferred_element_type=F32) for ch, vn in zip(chains, v_new)]
            kv = [jnp.dot(ch[4], vn, preferred_element_type=F32) for ch, vn in zip(chains, v_new)]
            yield
            states = [s * ch[5] + m for ch, s, m in zip(chains, states, kv)]
            outs.append([a1 + a2 for a1, a2 in zip(o1, o2)])
        for j in range(unroll):
            for bb in range(bpp):
                oacc[bb, 0, pl.ds(pl.multiple_of(cf[j] * CHUNK, CHUNK), CHUNK), :] = outs[j][2 * bb]
                oacc[bb, 1, pl.ds(pl.multiple_of(cb[j] * CHUNK, CHUNK), CHUNK), :] = outs[j][2 * bb + 1]
        return tuple(states)

    def step(i, carry):
        pre, states = carry
        nxt, states = _run_interleaved(solve(i + 1), scan(i, pre, states))
        return nxt, states

    pre, _ = _run_interleaved(solve(0), iter(()))
    states = tuple(s0_ref[bb, d, 0] for bb in range(bpp) for d in range(2))
    pre, states = lax.fori_loop(0, nsteps - 1, step, (pre, states))
    _, states = _run_interleaved(iter(()), scan(nsteps - 1, pre, states))
    for bb in range(bpp):
        sfin_ref[bb, 0, 0] = states[2 * bb]
        sfin_ref[bb, 1, 0] = states[2 * bb + 1]

    def fin(c, _):
        r0 = pl.multiple_of(c * CHUNK, CHUNK)
        for bb in range(bpp):
            o = oacc[bb, 0, pl.ds(r0, CHUNK), :] + oacc[bb, 1, pl.ds(r0, CHUNK), :]
            y = _rms(o, nrm_ref[...]) * _silu(z_ref[bb, pl.ds(r0, CHUNK), :])
            o_ref[bb, pl.ds(r0, CHUNK), :] = y.astype(o_ref.dtype)
        return 0

    lax.fori_loop(0, nchunks, fin, 0, unroll=2)


def _gdn(x, gates, conv_w, par, norm_w, s0, layer, bpp):
    bsz, seq_len, _ = x.shape
    col = lambda off: pl.BlockSpec((bpp, seq_len, HEAD), lambda b, h: (b, 0, off + h))
    cw = lambda off: pl.BlockSpec((3, HEAD), lambda b, h: (0, off + h))
    st = pl.BlockSpec((bpp, 2, 1, HEAD, HEAD), lambda b, h: (b, 0, h, 0, 0))
    st_in = pl.BlockSpec((bpp, None, 2, 1, HEAD, HEAD), lambda b, h: (b, layer, 0, h, 0, 0))
    return pl.pallas_call(
        _gdn_kernel,
        grid=(bsz // bpp, GDN_HEADS),
        in_specs=[col(0), col(GDN_HEADS), col(2 * GDN_HEADS), col(3 * GDN_HEADS),
                  pl.BlockSpec((bpp, seq_len, HEAD), lambda b, h: (b, 0, 0)),
                  cw(0), cw(GDN_HEADS), cw(2 * GDN_HEADS),
                  pl.BlockSpec((8, HEAD), lambda b, h: (0, 0)),
                  pl.BlockSpec((1, HEAD), lambda b, h: (0, 0)),
                  st_in],
        out_specs=[pl.BlockSpec((bpp, seq_len, HEAD), lambda b, h: (b, 0, h)), st],
        out_shape=[jax.ShapeDtypeStruct((bsz, seq_len, GDN_HEADS * HEAD), BF16),
                   jax.ShapeDtypeStruct((bsz, 2, GDN_HEADS, HEAD, HEAD), F32)],
        scratch_shapes=[pltpu.VMEM((bpp, seq_len, HEAD), F32)] * 3 + [pltpu.VMEM((bpp, 2, seq_len, HEAD), F32)],
        compiler_params=_cparams("parallel", "parallel"),
        name="gdn",
    )(x, x, x, x, gates, conv_w, conv_w, conv_w, par, norm_w.reshape(1, HEAD), s0)


def _ssd_kernel(x_ref, bm_ref, cm_ref, z_ref, dt_ref, cwx_ref, cwb_ref, cwc_ref, cbx_ref, cbb_ref, cbc_ref,
                par_ref, dsk_ref, h0_ref, y_ref, hfin_ref, xs, bs, cs):
    seq_len = x_ref.shape[1]
    nchunks = seq_len // CHUNK
    width = SSD_HPG * SSD_P

    def prep(c, _):
        r0 = pl.multiple_of(c * CHUNK, CHUNK)
        for src, cw, cb, dst in ((x_ref, cwx_ref, cbx_ref, xs), (bm_ref, cwb_ref, cbb_ref, bs),
                                 (cm_ref, cwc_ref, cbc_ref, cs)):
            dst[pl.ds(r0, CHUNK), :] = _silu(_conv3(src, r0, c, nchunks, seq_len, cw, lead=(0,)) + cb[...])
        return 0

    lax.fori_loop(0, nchunks, prep, 0)

    ri = _iota((CHUNK, CHUNK), 0)
    ci = _iota((CHUNK, CHUNK), 1)
    head_of_lane = jnp.right_shift(_iota((CHUNK, width), 1), 6)
    neg_a = -jnp.exp(par_ref[0:1, :])
    dt_bias = par_ref[1:2, :]

    def expand(cols):
        out = jnp.broadcast_to(cols[0], (CHUNK, width))
        for e in range(1, SSD_HPG):
            out = jnp.where(head_of_lane == e, cols[e], out)
        return out

    def gate_terms(c, d):
        r0 = pl.multiple_of(c * CHUNK, CHUNK)
        x, bm, cm = xs[pl.ds(r0, CHUNK), :], bs[pl.ds(r0, CHUNK), :], cs[pl.ds(r0, CHUNK), :]
        dt_t = _softplus(dt_ref[0, pl.ds(r0, CHUNK), :] + dt_bias)
        la_t = dt_t * neg_a
        cs_t = _cumsum_rows(la_t)
        tot_row = cs_t[CHUNK - 1:CHUNK, :]
        tot_t = jnp.broadcast_to(tot_row, cs_t.shape)
        if d == 1:
            cs_t = tot_t - cs_t + la_t
        incl = (ri >= ci) if d == 0 else (ri <= ci)
        cs_rows = jnp.transpose(cs_t)
        rest_t = tot_t - cs_t
        tot_p = jnp.broadcast_to(tot_row, (SSD_P, HEAD))
        lanes = [d * SSD_HPG + e for e in range(SSD_HPG)]
        col = _lane_col
        lmats = [jnp.exp(jnp.where(incl, col(cs_t, k) - cs_rows[k:k + 1, :], NEG)) for k in lanes]
        xdt = x * expand([col(dt_t, k) for k in lanes])
        cs_full = expand([col(cs_t, k) for k in lanes])
        cd = jnp.concatenate([jnp.broadcast_to(jnp.exp(col(tot_p, k)), (SSD_P, HEAD)) for k in lanes], axis=0)
        return dict(r0=r0, x=x, bm=bm, cm=cm, xdt=xdt, lmats=lmats, ecs=jnp.exp(cs_full), cd=cd,
                    xdec_t=jnp.transpose(xdt * jnp.exp(expand([col(rest_t, k) for k in lanes]))))

    unroll = min(SSD_UNROLL, nchunks)

    def direction(d, h0, finish):
        def step(i, hstate):
            cidx = [i * unroll + j for j in range(unroll)]
            if d == 1:
                cidx = [nchunks - 1 - c for c in cidx]
            t = [gate_terms(c, d) for c in cidx]
            cb = [_dot_nt(ch["cm"], ch["bm"]) for ch in t]
            states = [_dot(ch["xdec_t"], ch["bm"]) for ch in t]
            ys = []
            for ch, m in zip(t, cb):
                y = None
                for e in range(SSD_HPG):
                    part = _dot(m * ch["lmats"][e], jnp.where(head_of_lane == e, ch["xdt"], 0.0))
                    y = part if y is None else y + part
                ys.append(y)
            for j, ch in enumerate(t):
                ys[j] = ys[j] + _dot_nt(ch["cm"], hstate) * ch["ecs"]
                hstate = hstate * ch["cd"] + states[j]
            for ch, y in zip(t, ys):
                finish(ch, y)
            return hstate

        return lax.fori_loop(0, nchunks // unroll, step, h0.reshape(width, HEAD))

    def store_fwd(ch, y):
        y_ref[0, pl.ds(ch["r0"], CHUNK), :] = y

    def store_bwd(ch, y):
        tot = y_ref[0, pl.ds(ch["r0"], CHUNK), :] + y + ch["x"] * dsk_ref[...]
        y_ref[0, pl.ds(ch["r0"], CHUNK), :] = tot * _silu(z_ref[0, pl.ds(ch["r0"], CHUNK), :])

    hfin_ref[0, 0] = direction(0, h0_ref[0, 0], store_fwd).reshape(SSD_HPG, SSD_P, HEAD)
    hfin_ref[0, 1] = direction(1, h0_ref[0, 1], store_bwd).reshape(SSD_HPG, SSD_P, HEAD)


def _ssd(x, gates, conv_w, conv_b, par, d_skip, h0, layer):
    bsz, seq_len, _ = x.shape
    width = SSD_HPG * SSD_P
    spec = lambda wd, off: pl.BlockSpec((1, seq_len, wd), lambda b, g: (b, 0, off + g))
    cws = lambda rows, wd, off: pl.BlockSpec((rows, wd), lambda b, g: (0, off + g))
    st = pl.BlockSpec((1, 2, SSD_HPG, SSD_P, HEAD), lambda b, g: (b, 0, g, 0, 0))
    st_in = pl.BlockSpec((1, None, 2, SSD_HPG, SSD_P, HEAD), lambda b, g: (b, layer, 0, g, 0, 0))
    return pl.pallas_call(
        _ssd_kernel,
        grid=(bsz, SSD_GROUPS),
        in_specs=[spec(width, 0), spec(HEAD, 4), spec(HEAD, 6), spec(width, 4),
                  pl.BlockSpec((1, seq_len, HEAD), lambda b, g: (b, 0, g)),
                  cws(3, width, 0), cws(3, HEAD, 4), cws(3, HEAD, 6),
                  cws(1, width, 0), cws(1, HEAD, 4), cws(1, HEAD, 6),
                  pl.BlockSpec((8, HEAD), lambda b, g: (0, g)),
                  cws(1, width, 0),
                  st_in],
        out_specs=[pl.BlockSpec((1, seq_len, width), lambda b, g: (b, 0, g)), st],
        out_shape=[jax.ShapeDtypeStruct((bsz, seq_len, SSD_HEADS * SSD_P), F32),
                   jax.ShapeDtypeStruct((bsz, 2, SSD_HEADS, SSD_P, HEAD), F32)],
        scratch_shapes=[pltpu.VMEM((seq_len, width), F32), pltpu.VMEM((seq_len, HEAD), F32),
                        pltpu.VMEM((seq_len, HEAD), F32)],
        compiler_params=_cparams("parallel", "parallel"),
        name="ssd",
    )(x, x, x, x, gates, conv_w, conv_w, conv_w, conv_b, conv_b, conv_b, par, d_skip, h0)


def _rope_tables(n_tokens):
    half = HEAD // 2
    inv_freq = ROPE_THETA ** (-jnp.arange(0, half, 2, dtype=F32) / half)
    t = jnp.arange(n_tokens)
    ang_r = (t // GRID_W).astype(F32)[:, None] * inv_freq
    ang_c = (t % GRID_W).astype(F32)[:, None] * inv_freq
    cos = jnp.concatenate([jnp.cos(ang_r)] * 2 + [jnp.cos(ang_c)] * 2, axis=-1)
    sin = jnp.concatenate([-jnp.sin(ang_r), jnp.sin(ang_r), -jnp.sin(ang_c), jnp.sin(ang_c)], axis=-1)
    return cos, sin


def _gate_rows(a_log, dt_bias, offset):
    n = a_log.size
    rows = jnp.zeros((8, HEAD), F32)
    rows = rows.at[0, offset:offset + n].set(a_log.reshape(-1).astype(F32))
    return rows.at[1, offset:offset + n].set(dt_bias.reshape(-1).astype(F32))


def _ssd_group_lanes(t):
    lead = t.shape[:-1]
    t = t.reshape(lead + (2, SSD_GROUPS, SSD_HPG))
    t = jnp.swapaxes(t, -3, -2).reshape(lead + (SSD_GROUPS, 2 * SSD_HPG))
    pad = [(0, 0)] * (t.ndim - 1) + [(0, HEAD - 2 * SSD_HPG)]
    return jnp.pad(t, pad).reshape(lead + (SSD_GROUPS * HEAD,))


def _pad_cols(w, width):
    return jnp.pad(w, ((0, 0), (0, width - w.shape[1])))


def kernel(x_prompt, x_sample, state_gdn, cache_gqa_k, cache_gqa_v, state_ssd, cache_na_k, cache_na_v, c, c_ctx, ada_w, ada_b, norm_mix_pre, norm_mix_post, norm_mlp_pre, norm_mlp_post, mlp_w1, mlp_w2, ev_w_in, ev_w_out, gdn_conv, gdn_a_log, gdn_dt_bias, gdn_norm, gqa_q_norm, gqa_k_norm, od_w_in, od_w_out, ssd_conv, ssd_conv_b, ssd_a_log, ssd_dt_bias, ssd_d, ssd_norm, na_rpb):
    nb, ns, _ = x_prompt.shape
    db, dl, _ = x_sample.shape
    past = cache_gqa_k.shape[2]

    c16 = jnp.zeros((16, D_MODEL), F32).at[:db].set(c).at[db].set(c_ctx)
    mods = _ada(c16, ada_w, ada_b).reshape(DEPTH, 16, 6, D_MODEL)

    xp = x_prompt.reshape(1, nb * ns, D_MODEL)
    xs = x_sample
    tm_c = min(TM_DENSE_CTX, nb * ns)
    tm_in = min(TM_INPROJ, nb * ns, dl)
    tm_out, tm_mlp = min(TM_OUTPROJ, dl), min(TM_MLP, dl)
    cos_l, sin_l = _rope_tables(dl)
    cos_c, sin_c = jnp.ones((tm_in, HEAD), F32), jnp.zeros((tm_in, HEAD), F32)
    ctx_bpp = max(n for n in (4, 2, 1) if nb % n == 0)
    zero_gdn = jnp.zeros((nb, 1, 2, GDN_HEADS, HEAD, HEAD), F32)
    zero_ssd = jnp.zeros((nb, 1, 2, SSD_HEADS, SSD_P, HEAD), F32)
    new_gdn, new_gk, new_gv, new_ssd, new_nk, new_nv = [], [], [], [], [], []

    for i in range(DEPTH):
        j = i // 2
        mod_l, mod_c = mods[i, :db], mods[i, db:db + 1]
        if i % 2 == 0:
            wi = ev_w_in[j]
            w_in = jnp.concatenate([wi[:, :2048], wi[:, 2064:], _pad_cols(wi[:, 2048:2064], HEAD)], axis=1).astype(BF16)
            widths = (2048, 1024, HEAD)
            w_out = ev_w_out[j].astype(BF16)
            par = _gate_rows(gdn_a_log[j], gdn_dt_bias[j], 2 * GDN_HEADS)

            attn_w = (GQA_HEADS * HEAD, GQA_KV_HEADS * HEAD, GQA_KV_HEADS * HEAD)
            outs = _inproj(xp, mod_c, norm_mix_pre[i], w_in, widths, tm_in, attn_w, True,
                           (cos_c, sin_c, gqa_q_norm[j], gqa_k_norm[j], True))
            gx, gt, q, k, v, kf, vf = (t.reshape(nb, ns, -1) for t in outs)
            o_a, s_a = _gdn(gx, gt, gdn_conv[j], par, gdn_norm[j], zero_gdn, 0, ctx_bpp)
            o_b = _flash(q, k, v, GQA_KV_HEADS, 2, ns, ns)
            yc = _outproj(o_a.reshape(1, nb * ns, -1), o_b.reshape(1, nb * ns, -1), xp, mod_c,
                          w_out, norm_mix_post[i], tm_c)
            new_gdn.append(s_a)
            new_gk.append(kf.reshape(nb, ns, GQA_KV_HEADS, HEAD))
            new_gv.append(vf.reshape(nb, ns, GQA_KV_HEADS, HEAD))

            gx, gt, q, k, v = _inproj(xs, mod_l, norm_mix_pre[i], w_in, widths, tm_in, attn_w, False,
                                      (cos_l, sin_l, gqa_q_norm[j], gqa_k_norm[j], False))
            o_a, _ = _gdn(gx, gt, gdn_conv[j], par, gdn_norm[j], state_gdn, j, 1)
            o_b = _flash(q, k, v, GQA_KV_HEADS, 2, min(FLASH_TQ, dl), min(FLASH_TK, dl),
                         cache=(cache_gqa_k.reshape(db, -1, past, GQA_KV_HEADS * HEAD),
                                cache_gqa_v.reshape(db, -1, past, GQA_KV_HEADS * HEAD), j))
            yl = _outproj(o_a, o_b, xs, mod_l, w_out, norm_mix_post[i], tm_out)
        else:
            wi = od_w_in[j]
            w_in = jnp.concatenate([wi[:, 512:1536], wi[:, :512], wi[:, 1552:], _ssd_group_lanes(wi[:, 1536:1552])], axis=1).astype(BF16)
            widths = (1536, 1536, SSD_GROUPS * HEAD)
            w_out = od_w_out[j].astype(BF16)
            par = jnp.zeros((8, SSD_GROUPS * HEAD), F32)
            par = par.at[0].set(_ssd_group_lanes(ssd_a_log[j].reshape(-1).astype(F32)))
            par = par.at[1].set(_ssd_group_lanes(ssd_dt_bias[j].reshape(-1).astype(F32)))
            dsk = jnp.repeat(ssd_d[j].astype(F32), SSD_P).reshape(1, -1)
            cvw, cvb = ssd_conv[j], ssd_conv_b[j].reshape(1, -1)

            attn_w = (NA_HEADS * HEAD,) * 3
            outs = _inproj(xp, mod_c, norm_mix_pre[i], w_in, widths, tm_in, attn_w, True)
            sx, gt, q, k, v, kf, vf = (t.reshape(nb, ns, -1) for t in outs)
            y_c, s_c = _ssd(sx, gt, cvw, cvb, par, dsk, zero_ssd, 0)
            o_d = _flash(q, k, v, NA_HEADS, 1, ns, ns)
            yc = _outproj(y_c.reshape(1, nb * ns, -1), o_d.reshape(1, nb * ns, -1), xp, mod_c,
                          w_out, norm_mix_post[i], tm_c, ssd_norm[j])
            new_ssd.append(s_c)
            new_nk.append(kf.reshape(nb, ns, NA_HEADS, HEAD))
            new_nv.append(vf.reshape(nb, ns, NA_HEADS, HEAD))

            sx, gt, q, k, v = _inproj(xs, mod_l, norm_mix_pre[i], w_in, widths, tm_in, attn_w, False)
            y_c, _ = _ssd(sx, gt, cvw, cvb, par, dsk, state_ssd, j)
            o_d = _natten(q, k, v, cache_na_k.reshape(db, -1, past, NA_HEADS * HEAD),
                          cache_na_v.reshape(db, -1, past, NA_HEADS * HEAD), j, _natten_bias(na_rpb[j]))
            yl = _outproj(y_c, o_d, xs, mod_l, w_out, norm_mix_post[i], tm_out, ssd_norm[j])
        xp, xs = yc, yl
        w1, w2 = mlp_w1[i].astype(BF16), mlp_w2[i].astype(BF16)
        xp = _mlp(xp, mod_c, norm_mlp_pre[i], w1, w2, norm_mlp_post[i], tm_c, TF_MLP)
        xs = _mlp(xs, mod_l, norm_mlp_pre[i], w1, w2, norm_mlp_post[i], tm_mlp, TF_MLP)

    return (xp.reshape(nb, ns, D_MODEL), xs, jnp.stack(new_gdn, axis=1), jnp.stack(new_gk, axis=1),
            jnp.stack(new_gv, axis=1), jnp.stack(new_ssd, axis=1), jnp.stack(new_nk, axis=1),
            jnp.stack(new_nv, axis=1))
```

```python
import functools

import numpy as np
import jax
import jax.numpy as jnp
from jax import lax
from jax.experimental import pallas as pl
from jax.experimental.pallas import tpu as pltpu

F32 = jnp.float32
BF16 = jnp.bfloat16

D_MODEL = 1024
DEPTH = 4
GRID_W = 64
EPS = 1e-6
ROPE_THETA = 10000.0
D_FF = 4 * D_MODEL
HEAD = 128
GDN_HEADS = 4
GQA_HEADS = 4
GQA_KV_HEADS = 2
SSD_HEADS = 8
SSD_P = 64
SSD_GROUPS = 2
SSD_HPG = SSD_HEADS // SSD_GROUPS
NA_HEADS = 4
NA_WIN_R = 8
NA_WIN_C = 16
CHUNK = 128
GDN_UNROLL = 4
SSD_UNROLL = 4
TRI_PASSES = (1, 1, 2)
NEG = -1e30
QK_SCALE_LOG2 = HEAD ** -0.5 * float(np.log2(np.e))
FLASH_SAFE_BOUND = 60.0
FLASH_SHIFT = 64.0
VMEM_LIMIT_BYTES = 56 * 1024 * 1024

TM_INPROJ = 512
TM_OUTPROJ = 512
TM_DENSE_CTX = 1024
TM_MLP = 1024
TF_MLP = 1024
ADA_TN = 1536
FLASH_TQ = 1024
FLASH_TK = 2048


def _cparams(*sem):
    return pltpu.CompilerParams(dimension_semantics=sem, vmem_limit_bytes=VMEM_LIMIT_BYTES)


def _dot(a, b):
    return jnp.dot(a.astype(BF16), b.astype(BF16), preferred_element_type=F32)


def _dot_nt(a, b):
    return lax.dot_general(a.astype(BF16), b.astype(BF16), (((1,), (1,)), ((), ())),
                           preferred_element_type=F32)


def _split_bf16(a):
    hi = a.astype(BF16)
    lo = (a - hi.astype(F32)).astype(BF16)
    return hi, lo


def _silu(x):
    return x * jax.nn.sigmoid(x)


def _softplus(x):
    return jnp.maximum(x, 0.0) + jnp.log(1.0 + jnp.exp(-jnp.abs(x)))


def _rms(x, w):
    return x * lax.rsqrt(jnp.mean(x * x, axis=-1, keepdims=True) + EPS) * w


def _iota(shape, axis):
    return lax.broadcasted_iota(jnp.int32, shape, axis)


def _lane_col(tile, idx):
    return jnp.sum(jnp.where(_iota(tile.shape, 1) == idx, tile, 0.0), axis=-1, keepdims=True)


def _cumsum_rows(t):
    rows = _iota(t.shape, 0)
    s = 1
    while s < t.shape[0]:
        t = t + jnp.where(rows >= s, pltpu.roll(t, s, 0), 0.0)
        s *= 2
    return t


def _row_form(col):
    n = col.shape[0]
    return jnp.transpose(jnp.broadcast_to(col, (n, n)))


def _dot3_many(xs, ys):
    xs = [_split_bf16(x) for x in xs]
    ys = [_split_bf16(y) for y in ys]
    d = lambda x, y: jnp.dot(x, y, preferred_element_type=F32)
    hh = [d(x[0], y[0]) for x, y in zip(xs, ys)]
    hl = [d(x[0], y[1]) for x, y in zip(xs, ys)]
    lh = [d(x[1], y[0]) for x, y in zip(xs, ys)]
    return [a + (b + c) for a, b, c in zip(hh, hl, lh)]


def _dots_many(xs, ys, passes):
    if passes == 3:
        return _dot3_many(xs, ys)
    if passes == 2:
        d = lambda x, y: jnp.dot(x, y, preferred_element_type=F32)
        xs = [x.astype(BF16) for x in xs]
        ys = [_split_bf16(y) for y in ys]
        hi = [d(x, y[0]) for x, y in zip(xs, ys)]
        lo = [d(x, y[1]) for x, y in zip(xs, ys)]
        return [a + b for a, b in zip(hi, lo)]
    return [_dot(x, y) for x, y in zip(xs, ys)]


def _unit_tri_inverse_stages(a_list, ri, ci):
    same = lambda sh: jnp.right_shift(ri, sh) == jnp.right_shift(ci, sh)
    eye = jnp.where(ri == ci, 1.0, 0.0)
    p = [jnp.where(same(4), a, 0.0) for a in a_list]
    m = [eye - x for x in p]
    p = _dots_many(p, p, TRI_PASSES[0])
    yield
    for last in (False, False, True):
        mp = _dots_many(m, p, TRI_PASSES[0])
        if not last:
            p = _dots_many(p, p, TRI_PASSES[0])
        yield
        m = [x + y for x, y in zip(m, mp)]
    for sh in (4, 5, 6):
        off = [jnp.where(same(sh + 1), jnp.where(same(sh), 0.0, a), 0.0) for a in a_list]
        t = _dots_many(off, m, TRI_PASSES[1])
        yield
        mt = _dots_many(m, t, TRI_PASSES[1])
        yield
        m = [x - y for x, y in zip(m, mt)]
    return m


def _run_interleaved(gen_a, gen_b):
    results, live = [None, None], [gen_a, gen_b]
    while any(g is not None for g in live):
        for n, g in enumerate(live):
            if g is None:
                continue
            try:
                next(g)
            except StopIteration as stop:
                results[n], live[n] = stop.value, None
    return results


def _conv3(ref, r0, c, nchunks, seq_len, w, lead=()):
    x = ref[lead + (pl.ds(r0, CHUNK), slice(None))]
    prev = ref[lead + (pl.ds(jnp.maximum(r0 - 1, 0), 1), slice(None))]
    nxt = ref[lead + (pl.ds(jnp.minimum(r0 + CHUNK, seq_len - 1), 1), slice(None))]
    prev = jnp.where(c == 0, 0.0, prev)
    nxt = jnp.where(c == nchunks - 1, 0.0, nxt)
    rows = _iota(x.shape, 0)
    xm = jnp.where(rows == 0, prev, pltpu.roll(x, 1, 0))
    xp = jnp.where(rows == CHUNK - 1, nxt, pltpu.roll(x, CHUNK - 1, 0))
    return xm * w[0:1, :] + x * w[1:2, :] + xp * w[2:3, :]


def _ada_kernel(c_ref, w_ref, b_ref, o_ref):
    o_ref[0] = _dot(_silu(c_ref[...]), w_ref[0]) + b_ref[0]


def _ada(c16, ada_w, ada_b):
    tn = ADA_TN
    n = ada_w.shape[-1]
    return pl.pallas_call(
        _ada_kernel,
        grid=(DEPTH, n // tn),
        in_specs=[pl.BlockSpec((16, D_MODEL), lambda i, j: (0, 0)),
                  pl.BlockSpec((1, D_MODEL, tn), lambda i, j: (i, 0, j)),
                  pl.BlockSpec((1, 1, tn), lambda i, j: (i, 0, j))],
        out_specs=pl.BlockSpec((1, 16, tn), lambda i, j: (i, 0, j)),
        out_shape=jax.ShapeDtypeStruct((DEPTH, 16, n), F32),
        compiler_params=_cparams("parallel", "parallel"),
        name="ada",
    )(c16, ada_w, ada_b.reshape(DEPTH, 1, n))


def _mod_spec(mod):
    if mod.shape[0] == 1:
        return pl.BlockSpec((1, 6, D_MODEL), lambda b, *_: (0, 0, 0))
    return pl.BlockSpec((1, 6, D_MODEL), lambda b, *_: (b, 0, 0))


def _inproj_kernel(x_ref, mod_ref, nw_ref, w_ref, *rest, splits, rope, plain_kv):
    if rope:
        cos_ref, sin_ref, qn_ref, kn_ref = rest[:4]
        rest = rest[4:]
    mix_ref, gate_ref, q_ref, k_ref, v_ref = rest[:5]
    h = _rms(x_ref[0], nw_ref[...] * (1.0 + mod_ref[0, 1:2, :])) + mod_ref[0, 0:1, :]
    hb = h.astype(BF16)
    (m0, m1), (a0, a1), (g0, g1) = splits
    y = jnp.dot(hb, w_ref[:, a0:a1], preferred_element_type=F32)
    mix_ref[0] = jnp.dot(hb, w_ref[:, m0:m1], preferred_element_type=F32)
    gate_ref[0] = jnp.dot(hb, w_ref[:, g0:g1], preferred_element_type=F32)
    nq, nk = q_ref.shape[-1], k_ref.shape[-1]
    head = lambda off, hd: y[:, off + hd * HEAD:off + (hd + 1) * HEAD]
    if rope:
        cos, sin = cos_ref[...], sin_ref[...]
        first = (_iota(cos.shape, 1) & 32) == 0

        def normed_rotated(xh, w):
            n = _rms(xh, w)
            swapped = jnp.where(first, pltpu.roll(n, HEAD - 32, 1), pltpu.roll(n, 32, 1))
            return n, n * cos + swapped * sin

        for hd in range(nq // HEAD):
            _, r = normed_rotated(head(0, hd), qn_ref[...])
            q_ref[0, :, hd * HEAD:(hd + 1) * HEAD] = (r * QK_SCALE_LOG2).astype(BF16)
        for hd in range(nk // HEAD):
            n, r = normed_rotated(head(nq, hd), kn_ref[...])
            k_ref[0, :, hd * HEAD:(hd + 1) * HEAD] = r.astype(BF16)
            if plain_kv:
                rest[5][0, :, hd * HEAD:(hd + 1) * HEAD] = n
    else:
        q_ref[0] = (y[:, :nq] * QK_SCALE_LOG2).astype(BF16)
        k_ref[0] = y[:, nq:nq + nk].astype(BF16)
        if plain_kv:
            rest[5][0] = y[:, nq:nq + nk]
    v_ref[0] = y[:, nq + nk:].astype(BF16)
    if plain_kv:
        rest[6][0] = y[:, nq + nk:]


def _inproj(x, mod, norm_w, w, widths, tm, attn_widths, plain_kv, rope=None):
    bsz, seq_len, _ = x.shape
    splits, a = [], 0
    for wd in (widths[0], widths[1], widths[2]):
        splits.append((a, a + wd))
        a += wd
    row = lambda wd: pl.BlockSpec((1, tm, wd), lambda b, t: (b, t, 0))
    in_specs = [row(D_MODEL), _mod_spec(mod), pl.BlockSpec((1, D_MODEL), lambda b, t: (0, 0)),
                pl.BlockSpec(w.shape, lambda b, t: (0, 0))]
    args = [x, mod, norm_w.reshape(1, D_MODEL), w]
    if rope is not None:
        cos, sin, qn, kn, shared = rope
        tab = (pl.BlockSpec((tm, HEAD), lambda b, t: (0, 0)) if shared
               else pl.BlockSpec((tm, HEAD), lambda b, t: (t, 0)))
        in_specs += [tab, tab, pl.BlockSpec((1, HEAD), lambda b, t: (0, 0)), pl.BlockSpec((1, HEAD), lambda b, t: (0, 0))]
        args += [cos, sin, qn.reshape(1, HEAD), kn.reshape(1, HEAD)]
    out = [(widths[0], F32), (widths[2], F32)] + [(wd, BF16) for wd in attn_widths]
    if plain_kv:
        out += [(attn_widths[1], F32), (attn_widths[2], F32)]
    return pl.pallas_call(
        functools.partial(_inproj_kernel, splits=tuple(splits), rope=rope is not None, plain_kv=plain_kv),
        grid=(bsz, seq_len // tm),
        in_specs=in_specs,
        out_specs=[row(wd) for wd, _ in out],
        out_shape=[jax.ShapeDtypeStruct((bsz, seq_len, wd), dt) for wd, dt in out],
        compiler_params=_cparams("parallel", "parallel"),
        name="inproj",
    )(*args)


def _outproj_kernel(a_ref, b_ref, x_ref, mod_ref, w_ref, nw_ref, *rest, norm_a):
    a = a_ref[0]
    if norm_a:
        na_ref, o_ref = rest
        a = _rms(a, na_ref[...])
    else:
        (o_ref,) = rest
    half = a.shape[-1]
    y = (jnp.dot(a.astype(BF16), w_ref[:half, :], preferred_element_type=F32)
         + jnp.dot(b_ref[0].astype(BF16), w_ref[half:, :], preferred_element_type=F32))
    o_ref[0] = x_ref[0] + _rms(y, nw_ref[...] * mod_ref[0, 2:3, :])


def _outproj(a, b, x, mod, w, norm_w, tm, norm_a_w=None):
    bsz, seq_len, _ = x.shape
    half = a.shape[-1]
    in_specs = [pl.BlockSpec((1, tm, half), lambda bb, t: (bb, t, 0)),
                pl.BlockSpec((1, tm, half), lambda bb, t: (bb, t, 0)),
                pl.BlockSpec((1, tm, D_MODEL), lambda bb, t: (bb, t, 0)),
                _mod_spec(mod),
                pl.BlockSpec(w.shape, lambda bb, t: (0, 0)),
                pl.BlockSpec((1, D_MODEL), lambda bb, t: (0, 0))]
    args = [a, b, x, mod, w, norm_w.reshape(1, D_MODEL)]
    if norm_a_w is not None:
        in_specs.append(pl.BlockSpec((1, half), lambda bb, t: (0, 0)))
        args.append(norm_a_w.reshape(1, half))
    return pl.pallas_call(
        functools.partial(_outproj_kernel, norm_a=norm_a_w is not None),
        grid=(bsz, seq_len // tm),
        in_specs=in_specs,
        out_specs=pl.BlockSpec((1, tm, D_MODEL), lambda bb, t: (bb, t, 0)),
        out_shape=jax.ShapeDtypeStruct(x.shape, F32),
        compiler_params=_cparams("parallel", "parallel"),
        name="outproj",
    )(*args)


def _mlp_kernel(x_ref, mod_ref, npre_ref, w1_ref, w2_ref, npost_ref, o_ref, h_ref, acc_ref):
    f = pl.program_id(2)

    @pl.when(f == 0)
    def _():
        h = _rms(x_ref[0], npre_ref[...] * (1.0 + mod_ref[0, 4:5, :])) + mod_ref[0, 3:4, :]
        h_ref[...] = h.astype(BF16)
        acc_ref[...] = jnp.zeros_like(acc_ref)

    u = jnp.maximum(jnp.dot(h_ref[...], w1_ref[...], preferred_element_type=F32), 0.0)
    acc_ref[...] += jnp.dot((u * u).astype(BF16), w2_ref[...], preferred_element_type=F32)

    @pl.when(f == pl.num_programs(2) - 1)
    def _():
        o_ref[0] = x_ref[0] + _rms(acc_ref[...], npost_ref[...] * mod_ref[0, 5:6, :])


def _mlp(x, mod, npre, w1, w2, npost, tm, tf):
    bsz, seq_len, _ = x.shape
    return pl.pallas_call(
        _mlp_kernel,
        grid=(bsz, seq_len // tm, D_FF // tf),
        in_specs=[pl.BlockSpec((1, tm, D_MODEL), lambda b, t, f: (b, t, 0)),
                  _mod_spec(mod),
                  pl.BlockSpec((1, D_MODEL), lambda b, t, f: (0, 0)),
                  pl.BlockSpec((D_MODEL, tf), lambda b, t, f: (0, f)),
                  pl.BlockSpec((tf, D_MODEL), lambda b, t, f: (f, 0)),
                  pl.BlockSpec((1, D_MODEL), lambda b, t, f: (0, 0))],
        out_specs=pl.BlockSpec((1, tm, D_MODEL), lambda b, t, f: (b, t, 0)),
        out_shape=jax.ShapeDtypeStruct(x.shape, F32),
        scratch_shapes=[pltpu.VMEM((tm, D_MODEL), BF16), pltpu.VMEM((tm, D_MODEL), F32)],
        compiler_params=_cparams("parallel", "parallel", "arbitrary"),
        name="mlp",
    )(x, mod, npre.reshape(1, D_MODEL), w1, w2, npost.reshape(1, D_MODEL))


def _flash_kernel(q_ref, k_ref, v_ref, *rest, rep, tk, with_cache):
    if with_cache:
        kc_ref, vc_ref, o_ref, knorm_ref = rest
    else:
        o_ref, knorm_ref = rest
    tq = q_ref.shape[1]
    if rep == 2:
        qs = [q_ref[0, :, r * HEAD:(r + 1) * HEAD] for r in range(rep)]
    else:
        qs = [q_ref[0, :tq // 2, :], q_ref[0, tq // 2:, :]]
    nk = k_ref.shape[1] // tk
    rows = qs[0].shape[0]

    def lane_chunks(s):
        return [s[:, j * HEAD:(j + 1) * HEAD] for j in range(s.shape[1] // HEAD)]

    def row_max(k, part):
        s = [_dot_nt(q, k) for q in qs]
        return tuple(functools.reduce(jnp.maximum, lane_chunks(sc), m) for sc, m in zip(s, part))

    def accumulate(k, v, mx, carry):
        s = [_dot_nt(q, k) for q in qs]
        ps, ls = [], []
        for sc, m, (l, _) in zip(s, mx, carry):
            pc = [jnp.exp2(c - m) for c in lane_chunks(sc)]
            ls.append(functools.reduce(jnp.add, pc, l))
            ps.append(jnp.concatenate([c.astype(BF16) for c in pc], axis=-1))
        pv = [jnp.dot(p, v.astype(BF16), preferred_element_type=F32) for p in ps]
        return tuple((l, acc + x) for l, (_, acc), x in zip(ls, carry, pv))

    def block(ref, i):
        return ref[0, pl.ds(pl.multiple_of(i * tk, tk), tk), :]

    def max_sq_norm(k, best):
        kf = k.astype(BF16).astype(F32)
        sq = jnp.broadcast_to(jnp.sum(kf * kf, axis=-1, keepdims=True), kf.shape)
        return jnp.maximum(best, jnp.max(sq, axis=0, keepdims=True))

    @pl.when(pl.program_id(2) == 0)
    def _():
        best = lax.fori_loop(0, nk, lambda i, b: max_sq_norm(block(k_ref, i), b), jnp.zeros((1, HEAD), F32))
        if with_cache:
            best = max_sq_norm(kc_ref[0], best)
        knorm_ref[...] = jnp.sqrt(best)

    def exact_row_max():
        part = tuple(jnp.full((rows, HEAD), NEG, F32) for _ in qs)
        part = lax.fori_loop(0, nk, lambda i, p: row_max(block(k_ref, i), p), part)
        if with_cache:
            part = row_max(kc_ref[0], part)
        return [jnp.broadcast_to(jnp.max(m, axis=-1, keepdims=True), (rows, HEAD)) for m in part]

    bounds = []
    for q in qs:
        qf = q.astype(F32)
        qn = jnp.sqrt(jnp.sum(qf * qf, axis=-1, keepdims=True))
        bounds.append(jnp.broadcast_to(qn, (rows, HEAD)) * knorm_ref[...])
    safe = jnp.max(functools.reduce(jnp.maximum, bounds)) <= FLASH_SAFE_BOUND
    mx = lax.cond(safe, lambda: [b - FLASH_SHIFT for b in bounds], exact_row_max)

    stats = tuple((jnp.zeros((rows, HEAD), F32), jnp.zeros((rows, HEAD), F32)) for _ in qs)
    stats = lax.fori_loop(0, nk, lambda i, c: accumulate(block(k_ref, i), block(v_ref, i), mx, c), stats)
    if with_cache:
        stats = accumulate(kc_ref[0], vc_ref[0], mx, stats)
    outs = [(acc / jnp.sum(l, axis=-1, keepdims=True)).astype(o_ref.dtype) for l, acc in stats]
    if rep == 2:
        for r in range(rep):
            o_ref[0, :, r * HEAD:(r + 1) * HEAD] = outs[r]
    else:
        o_ref[0, :tq // 2, :] = outs[0]
        o_ref[0, tq // 2:, :] = outs[1]


def _flash(q, k, v, kv_heads, rep, tq, tk, cache=None):
    bsz, seq_len, _ = q.shape
    assert rep in (1, 2)
    m_len = k.shape[1]
    in_specs = [pl.BlockSpec((1, tq, rep * HEAD), lambda b, g, t: (b, t, g)),
                pl.BlockSpec((1, m_len, HEAD), lambda b, g, t: (b, 0, g)),
                pl.BlockSpec((1, m_len, HEAD), lambda b, g, t: (b, 0, g))]
    args = [q, k, v]
    if cache is not None:
        layer = cache[2]
        spec = pl.BlockSpec((1, None, cache[0].shape[2], HEAD), lambda b, g, t: (b, layer, 0, g))
        in_specs += [spec, spec]
        args += [cache[0], cache[1]]
    return pl.pallas_call(
        functools.partial(_flash_kernel, rep=rep, tk=tk, with_cache=cache is not None),
        grid=(bsz, kv_heads, seq_len // tq),
        in_specs=in_specs,
        out_specs=pl.BlockSpec((1, tq, rep * HEAD), lambda b, g, t: (b, t, g)),
        out_shape=jax.ShapeDtypeStruct((bsz, seq_len, kv_heads * rep * HEAD), BF16),
        scratch_shapes=[pltpu.VMEM((1, HEAD), F32)],
        compiler_params=_cparams("parallel", "parallel", "arbitrary"),
        name="flash",
    )(*args)


NA_GROUP = 4
NA_BAND = NA_GROUP + NA_WIN_R
NA_STEP = 4


def _natten_kernel(q_ref, k_ref, v_ref, kc_ref, vc_ref, bias_ref, o_ref):
    rows = q_ref.shape[1] // GRID_W
    gq = NA_GROUP * GRID_W
    kc = kc_ref[0].astype(BF16)
    vc = vc_ref[0].astype(BF16)
    low_half = _iota((GRID_W, HEAD), 1) < GRID_W

    def group_bias(r_first, band_first):
        per_row = []
        for i in range(NA_GROUP):
            r = r_first + i
            r0 = jnp.clip(r - NA_WIN_R // 2, 0, rows - NA_WIN_R)
            blocks = []
            for jj in range(NA_BAND // 2):
                kr = band_first + 2 * jj
                blk = bias_ref[0, jnp.clip(kr - r + NA_WIN_R, 0, 2 * NA_WIN_R - 1)]
                ok_a = jnp.logical_and(kr >= r0, kr < r0 + NA_WIN_R)
                ok_b = jnp.logical_and(kr + 1 >= r0, kr + 1 < r0 + NA_WIN_R)
                blocks.append(jnp.where(low_half, jnp.where(ok_a, blk, NEG), jnp.where(ok_b, blk, NEG)))
            per_row.append(jnp.concatenate(blocks, axis=-1))
        return jnp.concatenate(per_row, axis=0)

    def body(it, _):
        firsts = [(it * NA_STEP + g) * NA_GROUP for g in range(NA_STEP)]
        bands = [jnp.clip(r - NA_WIN_R // 2, 0, rows - NA_BAND) for r in firsts]
        q0 = [pl.multiple_of(r * GRID_W, gq) for r in firsts]
        k0 = [pl.multiple_of(b * GRID_W, GRID_W) for b in bands]
        qs = [q_ref[0, pl.ds(a, gq), :] for a in q0]
        ks = [k_ref[0, pl.ds(a, NA_BAND * GRID_W), :] for a in k0]
        vs = [v_ref[0, pl.ds(a, NA_BAND * GRID_W), :] for a in k0]
        bias = [group_bias(r, b) for r, b in zip(firsts, bands)]
        s_loc = [_dot_nt(q, k) for q, k in zip(qs, ks)]
        s_ctx = [_dot_nt(q, kc) for q in qs]
        s_loc = [s + b for s, b in zip(s_loc, bias)]
        m = [jnp.maximum(jnp.max(a, axis=-1, keepdims=True), jnp.max(c, axis=-1, keepdims=True))
             for a, c in zip(s_loc, s_ctx)]
        p_loc = [jnp.exp2(a - mm) for a, mm in zip(s_loc, m)]
        p_ctx = [jnp.exp2(c - mm) for c, mm in zip(s_ctx, m)]
        l = [jnp.sum(a, axis=-1, keepdims=True) + jnp.sum(c, axis=-1, keepdims=True) for a, c in zip(p_loc, p_ctx)]
        o_loc = [_dot(p, v) for p, v in zip(p_loc, vs)]
        o_ctx = [_dot(p, vc) for p in p_ctx]
        for a, ol, oc, ll in zip(q0, o_loc, o_ctx, l):
            o_ref[0, pl.ds(a, gq), :] = ((ol + oc) / ll).astype(o_ref.dtype)
        return 0

    lax.fori_loop(0, rows // (NA_GROUP * NA_STEP), body, 0)


def _natten(q, k, v, kc, vc, layer, bias):
    bsz, seq_len, _ = q.shape
    m_len = kc.shape[2]
    col = pl.BlockSpec((1, seq_len, HEAD), lambda b, h: (b, 0, h))
    ctx = pl.BlockSpec((1, None, m_len, HEAD), lambda b, h: (b, layer, 0, h))
    return pl.pallas_call(
        _natten_kernel,
        grid=(bsz, NA_HEADS),
        in_specs=[col, col, col, ctx, ctx,
                  pl.BlockSpec((1,) + bias.shape[1:], lambda b, h: (h, 0, 0, 0))],
        out_specs=pl.BlockSpec((1, seq_len, HEAD), lambda b, h: (b, 0, h)),
        out_shape=jax.ShapeDtypeStruct((bsz, seq_len, NA_HEADS * HEAD), BF16),
        compiler_params=_cparams("parallel", "parallel"),
        name="natten",
    )(q, k, v, kc, vc, bias)


def _natten_bias(rpb):
    col = np.arange(GRID_W)
    col_start = np.clip(col - NA_WIN_C // 2, 0, GRID_W - NA_WIN_C)
    kc = np.arange(GRID_W)
    valid = (kc[None, :] >= col_start[:, None]) & (kc[None, :] < col_start[:, None] + NA_WIN_C)
    coff = np.clip(kc[None, :] - col[:, None] + NA_WIN_C - 1, 0, 2 * NA_WIN_C - 2)
    log2e = float(np.log2(np.e))
    t = jnp.where(valid[None, None], rpb.astype(F32)[:, :, coff] * log2e, NEG)
    t = jnp.pad(t, ((0, 0), (1, 1), (0, 0), (0, 0)), constant_values=NEG)
    return jnp.concatenate([t[:, :-1], t[:, 1:]], axis=-1)


def _gdn_kernel(q_ref, k_ref, v_ref, z_ref, g_ref, cwq_ref, cwk_ref, cwv_ref, par_ref, nrm_ref, s0_ref,
                o_ref, sfin_ref, qs, ks, vs, oacc):
    bpp, seq_len = q_ref.shape[0], q_ref.shape[1]
    nchunks = seq_len // CHUNK
    h = pl.program_id(1)

    def prep(c, _):
        r0 = pl.multiple_of(c * CHUNK, CHUNK)
        for bb in range(bpp):
            for src, cw, dst, kind in ((q_ref, cwq_ref, qs, "q"), (k_ref, cwk_ref, ks, "k"), (v_ref, cwv_ref, vs, "v")):
                y = _silu(_conv3(src, r0, c, nchunks, seq_len, cw, lead=(bb,)))
                if kind != "v":
                    y = y * lax.rsqrt(jnp.sum(y * y, axis=-1, keepdims=True) + EPS)
                if kind == "q":
                    y = y * HEAD ** -0.5
                dst[bb, pl.ds(r0, CHUNK), :] = y
        return 0

    lax.fori_loop(0, nchunks, prep, 0, unroll=2)

    ri = _iota((CHUNK, CHUNK), 0)
    ci = _iota((CHUNK, CHUNK), 1)
    neg_a = -jnp.exp(par_ref[0:1, :])
    dt_bias = par_ref[1:2, :]

    def load(bb, c):
        r0 = pl.multiple_of(c * CHUNK, CHUNK)
        return (qs[bb, pl.ds(r0, CHUNK), :], ks[bb, pl.ds(r0, CHUNK), :], vs[bb, pl.ds(r0, CHUNK), :],
                g_ref[bb, pl.ds(r0, CHUNK), :])

    def gate_terms(loaded, d):
        q, k, v, gates = loaded
        beta = _lane_col(jax.nn.sigmoid(gates), d * GDN_HEADS + h)
        g_t = neg_a * _softplus(gates + dt_bias)
        gc_t = _cumsum_rows(g_t)
        tot_t = jnp.broadcast_to(gc_t[CHUNK - 1:CHUNK, :], gc_t.shape)
        if d == 1:
            gc_t = tot_t - gc_t + g_t
        gidx = 2 * GDN_HEADS + d * GDN_HEADS + h
        gc = _lane_col(gc_t, gidx)
        tot = _lane_col(tot_t, gidx)
        incl = (ri >= ci) if d == 0 else (ri <= ci)
        decay = jnp.exp(jnp.where(incl, gc - _row_form(gc), NEG))
        egc = jnp.exp(gc)
        kb = k * beta
        return dict(q=q, k=k, kb=kb, incl=incl, strict=(ri > ci) if d == 0 else (ri < ci), decay=decay,
                    x=jnp.concatenate([v * beta, kb * egc], axis=-1), q_dec=q * egc,
                    k_dec=k * jnp.exp(tot - gc), etot=jnp.exp(tot))

    unroll = min(GDN_UNROLL, nchunks)
    nsteps = nchunks // unroll

    def chunk_ids(i):
        cf = [i * unroll + j for j in range(unroll)]
        return cf, [nchunks - 1 - c for c in cf]

    def solve(i):
        cf, cb = chunk_ids(i)
        t = [gate_terms(load(bb, c), d) for bb in range(bpp) for c, d in zip(cf + cb, [0] * unroll + [1] * unroll)]
        kk = [_dot_nt(ch["kb"], ch["k"]) for ch in t]
        qk = [_dot_nt(ch["q"], ch["k"]) for ch in t]
        yield
        a = [jnp.where(ch["strict"], m * ch["decay"], 0.0) for ch, m in zip(t, kk)]
        qk = [jnp.where(ch["incl"], m * ch["decay"], 0.0).astype(BF16) for ch, m in zip(t, qk)]
        inv = yield from _unit_tri_inverse_stages(a, ri, ci)
        x = _dots_many(inv, [ch["x"] for ch in t], TRI_PASSES[2])
        yield
        return tuple((x[n][:, :HEAD], x[n][:, HEAD:].astype(BF16), qk[n], t[n]["q_dec"].astype(BF16),
                      jnp.transpose(t[n]["k_dec"]).astype(BF16), jnp.broadcast_to(t[n]["etot"], (CHUNK, HEAD)))
                     for n in range(2 * unroll * bpp))

    def scan(i, pre, states):
        cf, cb = chunk_ids(i)
        states = list(states)
        outs = []
        for j in range(unroll):
            chains = [pre[bb * 2 * unroll + d * unroll + j] for bb in range(bpp) for d in range(2)]
            sbf = [s.astype(BF16) for s in states]
            ws = [jnp.dot(ch[1], s, preferred_element_type=F32) for ch, s in zip(chains, sbf)]
            o1 = [jnp.dot(ch[3], s, preferred_element_type=F32) for ch, s in zip(chains, sbf)]
            yield
            v_new = [(ch[0] - w).astype(BF16) for ch, w in zip(chains, ws)]
            o2 = [jnp.dot(ch[2], vn, preferred_element_type=F32) for ch, vn in zip(chains, v_new)]
            kv = [jnp.dot(ch[4], vn, preferred_element_type=F32) for ch, vn in zip(chains, v_new)]
            yield
            states = [s * ch[5] + m for ch, s, m in zip(chains, states, kv)]
            outs.append([a1 + a2 for a1, a2 in zip(o1, o2)])
        for j in range(unroll):
            for bb in range(bpp):
                oacc[bb, 0, pl.ds(pl.multiple_of(cf[j] * CHUNK, CHUNK), CHUNK), :] = outs[j][2 * bb]
                oacc[bb, 1, pl.ds(pl.multiple_of(cb[j] * CHUNK, CHUNK), CHUNK), :] = outs[j][2 * bb + 1]
        return tuple(states)

    def step(i, carry):
        pre, states = carry
        nxt, states = _run_interleaved(solve(i + 1), scan(i, pre, states))
        return nxt, states

    pre, _ = _run_interleaved(solve(0), iter(()))
    states = tuple(s0_ref[bb, d, 0] for bb in range(bpp) for d in range(2))
    pre, states = lax.fori_loop(0, nsteps - 1, step, (pre, states))
    _, states = _run_interleaved(iter(()), scan(nsteps - 1, pre, states))
    for bb in range(bpp):
        sfin_ref[bb, 0, 0] = states[2 * bb]
        sfin_ref[bb, 1, 0] = states[2 * bb + 1]

    def fin(c, _):
        r0 = pl.multiple_of(c * CHUNK, CHUNK)
        for bb in range(bpp):
            o = oacc[bb, 0, pl.ds(r0, CHUNK), :] + oacc[bb, 1, pl.ds(r0, CHUNK), :]
            y = _rms(o, nrm_ref[...]) * _silu(z_ref[bb, pl.ds(r0, CHUNK), :])
            o_ref[bb, pl.ds(r0, CHUNK), :] = y.astype(o_ref.dtype)
        return 0

    lax.fori_loop(0, nchunks, fin, 0, unroll=2)


def _gdn(x, gates, conv_w, par, norm_w, s0, layer, bpp):
    bsz, seq_len, _ = x.shape
    col = lambda off: pl.BlockSpec((bpp, seq_len, HEAD), lambda b, h: (b, 0, off + h))
    cw = lambda off: pl.BlockSpec((3, HEAD), lambda b, h: (0, off + h))
    st = pl.BlockSpec((bpp, 2, 1, HEAD, HEAD), lambda b, h: (b, 0, h, 0, 0))
    st_in = pl.BlockSpec((bpp, None, 2, 1, HEAD, HEAD), lambda b, h: (b, layer, 0, h, 0, 0))
    return pl.pallas_call(
        _gdn_kernel,
        grid=(bsz // bpp, GDN_HEADS),
        in_specs=[col(0), col(GDN_HEADS), col(2 * GDN_HEADS), col(3 * GDN_HEADS),
                  pl.BlockSpec((bpp, seq_len, HEAD), lambda b, h: (b, 0, 0)),
                  cw(0), cw(GDN_HEADS), cw(2 * GDN_HEADS),
                  pl.BlockSpec((8, HEAD), lambda b, h: (0, 0)),
                  pl.BlockSpec((1, HEAD), lambda b, h: (0, 0)),
                  st_in],
        out_specs=[pl.BlockSpec((bpp, seq_len, HEAD), lambda b, h: (b, 0, h)), st],
        out_shape=[jax.ShapeDtypeStruct((bsz, seq_len, GDN_HEADS * HEAD), BF16),
                   jax.ShapeDtypeStruct((bsz, 2, GDN_HEADS, HEAD, HEAD), F32)],
        scratch_shapes=[pltpu.VMEM((bpp, seq_len, HEAD), F32)] * 3 + [pltpu.VMEM((bpp, 2, seq_len, HEAD), F32)],
        compiler_params=_cparams("parallel", "parallel"),
        name="gdn",
    )(x, x, x, x, gates, conv_w, conv_w, conv_w, par, norm_w.reshape(1, HEAD), s0)


def _ssd_kernel(x_ref, bm_ref, cm_ref, z_ref, dt_ref, cwx_ref, cwb_ref, cwc_ref, cbx_ref, cbb_ref, cbc_ref,
                par_ref, dsk_ref, h0_ref, y_ref, hfin_ref, xs, bs, cs):
    seq_len = x_ref.shape[1]
    nchunks = seq_len // CHUNK
    width = SSD_HPG * SSD_P

    def prep(c, _):
        r0 = pl.multiple_of(c * CHUNK, CHUNK)
        for src, cw, cb, dst in ((x_ref, cwx_ref, cbx_ref, xs), (bm_ref, cwb_ref, cbb_ref, bs),
                                 (cm_ref, cwc_ref, cbc_ref, cs)):
            dst[pl.ds(r0, CHUNK), :] = _silu(_conv3(src, r0, c, nchunks, seq_len, cw, lead=(0,)) + cb[...])
        return 0

    lax.fori_loop(0, nchunks, prep, 0)

    ri = _iota((CHUNK, CHUNK), 0)
    ci = _iota((CHUNK, CHUNK), 1)
    head_of_lane = jnp.right_shift(_iota((CHUNK, width), 1), 6)
    neg_a = -jnp.exp(par_ref[0:1, :])
    dt_bias = par_ref[1:2, :]

    def expand(cols):
        out = jnp.broadcast_to(cols[0], (CHUNK, width))
        for e in range(1, SSD_HPG):
            out = jnp.where(head_of_lane == e, cols[e], out)
        return out

    def gate_terms(c, d):
        r0 = pl.multiple_of(c * CHUNK, CHUNK)
        x, bm, cm = xs[pl.ds(r0, CHUNK), :], bs[pl.ds(r0, CHUNK), :], cs[pl.ds(r0, CHUNK), :]
        dt_t = _softplus(dt_ref[0, pl.ds(r0, CHUNK), :] + dt_bias)
        la_t = dt_t * neg_a
        cs_t = _cumsum_rows(la_t)
        tot_row = cs_t[CHUNK - 1:CHUNK, :]
        tot_t = jnp.broadcast_to(tot_row, cs_t.shape)
        if d == 1:
            cs_t = tot_t - cs_t + la_t
        incl = (ri >= ci) if d == 0 else (ri <= ci)
        cs_rows = jnp.transpose(cs_t)
        rest_t = tot_t - cs_t
        tot_p = jnp.broadcast_to(tot_row, (SSD_P, HEAD))
        lanes = [d * SSD_HPG + e for e in range(SSD_HPG)]
        col = _lane_col
        lmats = [jnp.exp(jnp.where(incl, col(cs_t, k) - cs_rows[k:k + 1, :], NEG)) for k in lanes]
        xdt = x * expand([col(dt_t, k) for k in lanes])
        cs_full = expand([col(cs_t, k) for k in lanes])
        cd = jnp.concatenate([jnp.broadcast_to(jnp.exp(col(tot_p, k)), (SSD_P, HEAD)) for k in lanes], axis=0)
        return dict(r0=r0, x=x, bm=bm, cm=cm, xdt=xdt, lmats=lmats, ecs=jnp.exp(cs_full), cd=cd,
                    xdec_t=jnp.transpose(xdt * jnp.exp(expand([col(rest_t, k) for k in lanes]))))

    unroll = min(SSD_UNROLL, nchunks)

    def direction(d, h0, finish):
        def step(i, hstate):
            cidx = [i * unroll + j for j in range(unroll)]
            if d == 1:
                cidx = [nchunks - 1 - c for c in cidx]
            t = [gate_terms(c, d) for c in cidx]
            cb = [_dot_nt(ch["cm"], ch["bm"]) for ch in t]
            states = [_dot(ch["xdec_t"], ch["bm"]) for ch in t]
            ys = []
            for ch, m in zip(t, cb):
                y = None
                for e in range(SSD_HPG):
                    part = _dot(m * ch["lmats"][e], jnp.where(head_of_lane == e, ch["xdt"], 0.0))
                    y = part if y is None else y + part
                ys.append(y)
            for j, ch in enumerate(t):
                ys[j] = ys[j] + _dot_nt(ch["cm"], hstate) * ch["ecs"]
                hstate = hstate * ch["cd"] + states[j]
            for ch, y in zip(t, ys):
                finish(ch, y)
            return hstate

        return lax.fori_loop(0, nchunks // unroll, step, h0.reshape(width, HEAD))

    def store_fwd(ch, y):
        y_ref[0, pl.ds(ch["r0"], CHUNK), :] = y

    def store_bwd(ch, y):
        tot = y_ref[0, pl.ds(ch["r0"], CHUNK), :] + y + ch["x"] * dsk_ref[...]
        y_ref[0, pl.ds(ch["r0"], CHUNK), :] = tot * _silu(z_ref[0, pl.ds(ch["r0"], CHUNK), :])

    hfin_ref[0, 0] = direction(0, h0_ref[0, 0], store_fwd).reshape(SSD_HPG, SSD_P, HEAD)
    hfin_ref[0, 1] = direction(1, h0_ref[0, 1], store_bwd).reshape(SSD_HPG, SSD_P, HEAD)


def _ssd(x, gates, conv_w, conv_b, par, d_skip, h0, layer):
    bsz, seq_len, _ = x.shape
    width = SSD_HPG * SSD_P
    spec = lambda wd, off: pl.BlockSpec((1, seq_len, wd), lambda b, g: (b, 0, off + g))
    cws = lambda rows, wd, off: pl.BlockSpec((rows, wd), lambda b, g: (0, off + g))
    st = pl.BlockSpec((1, 2, SSD_HPG, SSD_P, HEAD), lambda b, g: (b, 0, g, 0, 0))
    st_in = pl.BlockSpec((1, None, 2, SSD_HPG, SSD_P, HEAD), lambda b, g: (b, layer, 0, g, 0, 0))
    return pl.pallas_call(
        _ssd_kernel,
        grid=(bsz, SSD_GROUPS),
        in_specs=[spec(width, 0), spec(HEAD, 4), spec(HEAD, 6), spec(width, 4),
                  pl.BlockSpec((1, seq_len, HEAD), lambda b, g: (b, 0, g)),
                  cws(3, width, 0), cws(3, HEAD, 4), cws(3, HEAD, 6),
                  cws(1, width, 0), cws(1, HEAD, 4), cws(1, HEAD, 6),
                  pl.BlockSpec((8, HEAD), lambda b, g: (0, g)),
                  cws(1, width, 0),
                  st_in],
        out_specs=[pl.BlockSpec((1, seq_len, width), lambda b, g: (b, 0, g)), st],
        out_shape=[jax.ShapeDtypeStruct((bsz, seq_len, SSD_HEADS * SSD_P), F32),
                   jax.ShapeDtypeStruct((bsz, 2, SSD_HEADS, SSD_P, HEAD), F32)],
        scratch_shapes=[pltpu.VMEM((seq_len, width), F32), pltpu.VMEM((seq_len, HEAD), F32),
                        pltpu.VMEM((seq_len, HEAD), F32)],
        compiler_params=_cparams("parallel", "parallel"),
        name="ssd",
    )(x, x, x, x, gates, conv_w, conv_w, conv_w, conv_b, conv_b, conv_b, par, d_skip, h0)


def _rope_tables(n_tokens):
    half = HEAD // 2
    inv_freq = ROPE_THETA ** (-jnp.arange(0, half, 2, dtype=F32) / half)
    t = jnp.arange(n_tokens)
    ang_r = (t // GRID_W).astype(F32)[:, None] * inv_freq
    ang_c = (t % GRID_W).astype(F32)[:, None] * inv_freq
    cos = jnp.concatenate([jnp.cos(ang_r)] * 2 + [jnp.cos(ang_c)] * 2, axis=-1)
    sin = jnp.concatenate([-jnp.sin(ang_r), jnp.sin(ang_r), -jnp.sin(ang_c), jnp.sin(ang_c)], axis=-1)
    return cos, sin


def _gate_rows(a_log, dt_bias, offset):
    n = a_log.size
    rows = jnp.zeros((8, HEAD), F32)
    rows = rows.at[0, offset:offset + n].set(a_log.reshape(-1).astype(F32))
    return rows.at[1, offset:offset + n].set(dt_bias.reshape(-1).astype(F32))


def _ssd_group_lanes(t):
    lead = t.shape[:-1]
    t = t.reshape(lead + (2, SSD_GROUPS, SSD_HPG))
    t = jnp.swapaxes(t, -3, -2).reshape(lead + (SSD_GROUPS, 2 * SSD_HPG))
    pad = [(0, 0)] * (t.ndim - 1) + [(0, HEAD - 2 * SSD_HPG)]
    return jnp.pad(t, pad).reshape(lead + (SSD_GROUPS * HEAD,))


def _pad_cols(w, width):
    return jnp.pad(w, ((0, 0), (0, width - w.shape[1])))


def kernel(x_prompt, x_sample, state_gdn, cache_gqa_k, cache_gqa_v, state_ssd, cache_na_k, cache_na_v, c, c_ctx, ada_w, ada_b, norm_mix_pre, norm_mix_post, norm_mlp_pre, norm_mlp_post, mlp_w1, mlp_w2, ev_w_in, ev_w_out, gdn_conv, gdn_a_log, gdn_dt_bias, gdn_norm, gqa_q_norm, gqa_k_norm, od_w_in, od_w_out, ssd_conv, ssd_conv_b, ssd_a_log, ssd_dt_bias, ssd_d, ssd_norm, na_rpb):
    nb, ns, _ = x_prompt.shape
    db, dl, _ = x_sample.shape
    past = cache_gqa_k.shape[2]

    c16 = jnp.zeros((16, D_MODEL), F32).at[:db].set(c).at[db].set(c_ctx)
    mods = _ada(c16, ada_w, ada_b).reshape(DEPTH, 16, 6, D_MODEL)

    xp = x_prompt.reshape(1, nb * ns, D_MODEL)
    xs = x_sample
    tm_c = min(TM_DENSE_CTX, nb * ns)
    tm_in = min(TM_INPROJ, nb * ns, dl)
    tm_out, tm_mlp = min(TM_OUTPROJ, dl), min(TM_MLP, dl)
    cos_l, sin_l = _rope_tables(dl)
    cos_c, sin_c = jnp.ones((tm_in, HEAD), F32), jnp.zeros((tm_in, HEAD), F32)
    ctx_bpp = max(n for n in (4, 2, 1) if nb % n == 0)
    zero_gdn = jnp.zeros((nb, 1, 2, GDN_HEADS, HEAD, HEAD), F32)
    zero_ssd = jnp.zeros((nb, 1, 2, SSD_HEADS, SSD_P, HEAD), F32)
    new_gdn, new_gk, new_gv, new_ssd, new_nk, new_nv = [], [], [], [], [], []

    for i in range(DEPTH):
        j = i // 2
        mod_l, mod_c = mods[i, :db], mods[i, db:db + 1]
        if i % 2 == 0:
            wi = ev_w_in[j]
            w_in = jnp.concatenate([wi[:, :2048], wi[:, 2064:], _pad_cols(wi[:, 2048:2064], HEAD)], axis=1).astype(BF16)
            widths = (2048, 1024, HEAD)
            w_out = ev_w_out[j].astype(BF16)
            par = _gate_rows(gdn_a_log[j], gdn_dt_bias[j], 2 * GDN_HEADS)

            attn_w = (GQA_HEADS * HEAD, GQA_KV_HEADS * HEAD, GQA_KV_HEADS * HEAD)
            outs = _inproj(xp, mod_c, norm_mix_pre[i], w_in, widths, tm_in, attn_w, True,
                           (cos_c, sin_c, gqa_q_norm[j], gqa_k_norm[j], True))
            gx, gt, q, k, v, kf, vf = (t.reshape(nb, ns, -1) for t in outs)
            o_a, s_a = _gdn(gx, gt, gdn_conv[j], par, gdn_norm[j], zero_gdn, 0, ctx_bpp)
            o_b = _flash(q, k, v, GQA_KV_HEADS, 2, ns, ns)
            yc = _outproj(o_a.reshape(1, nb * ns, -1), o_b.reshape(1, nb * ns, -1), xp, mod_c,
                          w_out, norm_mix_post[i], tm_c)
            new_gdn.append(s_a)
            new_gk.append(kf.reshape(nb, ns, GQA_KV_HEADS, HEAD))
            new_gv.append(vf.reshape(nb, ns, GQA_KV_HEADS, HEAD))

            gx, gt, q, k, v = _inproj(xs, mod_l, norm_mix_pre[i], w_in, widths, tm_in, attn_w, False,
                                      (cos_l, sin_l, gqa_q_norm[j], gqa_k_norm[j], False))
            o_a, _ = _gdn(gx, gt, gdn_conv[j], par, gdn_norm[j], state_gdn, j, 1)
            o_b = _flash(q, k, v, GQA_KV_HEADS, 2, min(FLASH_TQ, dl), min(FLASH_TK, dl),
                         cache=(cache_gqa_k.reshape(db, -1, past, GQA_KV_HEADS * HEAD),
                                cache_gqa_v.reshape(db, -1, past, GQA_KV_HEADS * HEAD), j))
            yl = _outproj(o_a, o_b, xs, mod_l, w_out, norm_mix_post[i], tm_out)
        else:
            wi = od_w_in[j]
            w_in = jnp.concatenate([wi[:, 512:1536], wi[:, :512], wi[:, 1552:], _ssd_group_lanes(wi[:, 1536:1552])], axis=1).astype(BF16)
            widths = (1536, 1536, SSD_GROUPS * HEAD)
            w_out = od_w_out[j].astype(BF16)
            par = jnp.zeros((8, SSD_GROUPS * HEAD), F32)
            par = par.at[0].set(_ssd_group_lanes(ssd_a_log[j].reshape(-1).astype(F32)))
            par = par.at[1].set(_ssd_group_lanes(ssd_dt_bias[j].reshape(-1).astype(F32)))
            dsk = jnp.repeat(ssd_d[j].astype(F32), SSD_P).reshape(1, -1)
            cvw, cvb = ssd_conv[j], ssd_conv_b[j].reshape(1, -1)

            attn_w = (NA_HEADS * HEAD,) * 3
            outs = _inproj(xp, mod_c, norm_mix_pre[i], w_in, widths, tm_in, attn_w, True)
            sx, gt, q, k, v, kf, vf = (t.reshape(nb, ns, -1) for t in outs)
            y_c, s_c = _ssd(sx, gt, cvw, cvb, par, dsk, zero_ssd, 0)
            o_d = _flash(q, k, v, NA_HEADS, 1, ns, ns)
            yc = _outproj(y_c.reshape(1, nb * ns, -1), o_d.reshape(1, nb * ns, -1), xp, mod_c,
                          w_out, norm_mix_post[i], tm_c, ssd_norm[j])
            new_ssd.append(s_c)
            new_nk.append(kf.reshape(nb, ns, NA_HEADS, HEAD))
            new_nv.append(vf.reshape(nb, ns, NA_HEADS, HEAD))

            sx, gt, q, k, v = _inproj(xs, mod_l, norm_mix_pre[i], w_in, widths, tm_in, attn_w, False)
            y_c, _ = _ssd(sx, gt, cvw, cvb, par, dsk, state_ssd, j)
            o_d = _natten(q, k, v, cache_na_k.reshape(db, -1, past, NA_HEADS * HEAD),
                          cache_na_v.reshape(db, -1, past, NA_HEADS * HEAD), j, _natten_bias(na_rpb[j]))
            yl = _outproj(y_c, o_d, xs, mod_l, w_out, norm_mix_post[i], tm_out, ssd_norm[j])
        xp, xs = yc, yl
        w1, w2 = mlp_w1[i].astype(BF16), mlp_w2[i].astype(BF16)
        xp = _mlp(xp, mod_c, norm_mlp_pre[i], w1, w2, norm_mlp_post[i], tm_c, TF_MLP)
        xs = _mlp(xs, mod_l, norm_mlp_pre[i], w1, w2, norm_mlp_post[i], tm_mlp, TF_MLP)

    return (xp.reshape(nb, ns, D_MODEL), xs, jnp.stack(new_gdn, axis=1), jnp.stack(new_gk, axis=1),
            jnp.stack(new_gv, axis=1), jnp.stack(new_ssd, axis=1), jnp.stack(new_nk, axis=1),
            jnp.stack(new_nv, axis=1))
```

```python
import functools

import numpy as np
import jax
import jax.numpy as jnp
from jax import lax
from jax.experimental import pallas as pl
from jax.experimental.pallas import tpu as pltpu

F32 = jnp.float32
BF16 = jnp.bfloat16

D_MODEL = 1024
DEPTH = 4
GRID_W = 64
EPS = 1e-6
ROPE_THETA = 10000.0
D_FF = 4 * D_MODEL
HEAD = 128
GDN_HEADS = 4
GQA_HEADS = 4
GQA_KV_HEADS = 2
SSD_HEADS = 8
SSD_P = 64
SSD_GROUPS = 2
SSD_HPG = SSD_HEADS // SSD_GROUPS
NA_HEADS = 4
NA_WIN_R = 8
NA_WIN_C = 16
CHUNK = 128
GDN_UNROLL = 4
SSD_UNROLL = 4
TRI_PASSES = (1, 1, 2)
NEG = -1e30
QK_SCALE_LOG2 = HEAD ** -0.5 * float(np.log2(np.e))
FLASH_SAFE_BOUND = 60.0
FLASH_SHIFT = 64.0
VMEM_LIMIT_BYTES = 56 * 1024 * 1024

TM_INPROJ = 512
TM_OUTPROJ = 512
TM_DENSE_CTX = 1024
TM_MLP = 1024
TF_MLP = 1024
ADA_TN = 1536
FLASH_TQ = 1024
FLASH_TK = 2048


def _cparams(*sem):
    return pltpu.CompilerParams(dimension_semantics=sem, vmem_limit_bytes=VMEM_LIMIT_BYTES)


def _dot(a, b):
    return jnp.dot(a.astype(BF16), b.astype(BF16), preferred_element_type=F32)


def _dot_nt(a, b):
    return lax.dot_general(a.astype(BF16), b.astype(BF16), (((1,), (1,)), ((), ())),
                           preferred_element_type=F32)


def _split_bf16(a):
    hi = a.astype(BF16)
    lo = (a - hi.astype(F32)).astype(BF16)
    return hi, lo


def _silu(x):
    return x * jax.nn.sigmoid(x)


def _softplus(x):
    return jnp.maximum(x, 0.0) + jnp.log(1.0 + jnp.exp(-jnp.abs(x)))


def _rms(x, w):
    return x * lax.rsqrt(jnp.mean(x * x, axis=-1, keepdims=True) + EPS) * w


def _iota(shape, axis):
    return lax.broadcasted_iota(jnp.int32, shape, axis)


def _lane_col(tile, idx):
    return jnp.sum(jnp.where(_iota(tile.shape, 1) == idx, tile, 0.0), axis=-1, keepdims=True)


def _cumsum_rows(t):
    rows = _iota(t.shape, 0)
    s = 1
    while s < t.shape[0]:
        t = t + jnp.where(rows >= s, pltpu.roll(t, s, 0), 0.0)
        s *= 2
    return t


def _row_form(col):
    n = col.shape[0]
    return jnp.transpose(jnp.broadcast_to(col, (n, n)))


def _dot3_many(xs, ys):
    xs = [_split_bf16(x) for x in xs]
    ys = [_split_bf16(y) for y in ys]
    d = lambda x, y: jnp.dot(x, y, preferred_element_type=F32)
    hh = [d(x[0], y[0]) for x, y in zip(xs, ys)]
    hl = [d(x[0], y[1]) for x, y in zip(xs, ys)]
    lh = [d(x[1], y[0]) for x, y in zip(xs, ys)]
    return [a + (b + c) for a, b, c in zip(hh, hl, lh)]


def _dots_many(xs, ys, passes):
    if passes == 3:
        return _dot3_many(xs, ys)
    if passes == 2:
        d = lambda x, y: jnp.dot(x, y, preferred_element_type=F32)
        xs = [x.astype(BF16) for x in xs]
        ys = [_split_bf16(y) for y in ys]
        hi = [d(x, y[0]) for x, y in zip(xs, ys)]
        lo = [d(x, y[1]) for x, y in zip(xs, ys)]
        return [a + b for a, b in zip(hi, lo)]
    return [_dot(x, y) for x, y in zip(xs, ys)]


def _unit_tri_inverse_stages(a_list, ri, ci):
    same = lambda sh: jnp.right_shift(ri, sh) == jnp.right_shift(ci, sh)
    eye = jnp.where(ri == ci, 1.0, 0.0)
    p = [jnp.where(same(4), a, 0.0) for a in a_list]
    m = [eye - x for x in p]
    p = _dots_many(p, p, TRI_PASSES[0])
    yield
    for last in (False, False, True):
        mp = _dots_many(m, p, TRI_PASSES[0])
        if not last:
            p = _dots_many(p, p, TRI_PASSES[0])
        yield
        m = [x + y for x, y in zip(m, mp)]
    for sh in (4, 5, 6):
        off = [jnp.where(same(sh + 1), jnp.where(same(sh), 0.0, a), 0.0) for a in a_list]
        t = _dots_many(off, m, TRI_PASSES[1])
        yield
        mt = _dots_many(m, t, TRI_PASSES[1])
        yield
        m = [x - y for x, y in zip(m, mt)]
    return m


def _run_interleaved(gen_a, gen_b):
    results, live = [None, None], [gen_a, gen_b]
    while any(g is not None for g in live):
        for n, g in enumerate(live):
            if g is None:
                continue
            try:
                next(g)
            except StopIteration as stop:
                results[n], live[n] = stop.value, None
    return results


def _conv3(ref, r0, c, nchunks, seq_len, w, lead=()):
    x = ref[lead + (pl.ds(r0, CHUNK), slice(None))]
    prev = ref[lead + (pl.ds(jnp.maximum(r0 - 1, 0), 1), slice(None))]
    nxt = ref[lead + (pl.ds(jnp.minimum(r0 + CHUNK, seq_len - 1), 1), slice(None))]
    prev = jnp.where(c == 0, 0.0, prev)
    nxt = jnp.where(c == nchunks - 1, 0.0, nxt)
    rows = _iota(x.shape, 0)
    xm = jnp.where(rows == 0, prev, pltpu.roll(x, 1, 0))
    xp = jnp.where(rows == CHUNK - 1, nxt, pltpu.roll(x, CHUNK - 1, 0))
    return xm * w[0:1, :] + x * w[1:2, :] + xp * w[2:3, :]


def _ada_kernel(c_ref, w_ref, b_ref, o_ref):
    o_ref[0] = _dot(_silu(c_ref[...]), w_ref[0]) + b_ref[0]


def _ada(c16, ada_w, ada_b):
    tn = ADA_TN
    n = ada_w.shape[-1]
    return pl.pallas_call(
        _ada_kernel,
        grid=(DEPTH, n // tn),
        in_specs=[pl.BlockSpec((16, D_MODEL), lambda i, j: (0, 0)),
                  pl.BlockSpec((1, D_MODEL, tn), lambda i, j: (i, 0, j)),
                  pl.BlockSpec((1, 1, tn), lambda i, j: (i, 0, j))],
        out_specs=pl.BlockSpec((1, 16, tn), lambda i, j: (i, 0, j)),
        out_shape=jax.ShapeDtypeStruct((DEPTH, 16, n), F32),
        compiler_params=_cparams("parallel", "parallel"),
        name="ada",
    )(c16, ada_w, ada_b.reshape(DEPTH, 1, n))


def _mod_spec(mod):
    if mod.shape[0] == 1:
        return pl.BlockSpec((1, 6, D_MODEL), lambda b, *_: (0, 0, 0))
    return pl.BlockSpec((1, 6, D_MODEL), lambda b, *_: (b, 0, 0))


def _inproj_kernel(x_ref, mod_ref, nw_ref, w_ref, *rest, splits, rope, plain_kv):
    if rope:
        cos_ref, sin_ref, qn_ref, kn_ref = rest[:4]
        rest = rest[4:]
    mix_ref, gate_ref, q_ref, k_ref, v_ref = rest[:5]
    h = _rms(x_ref[0], nw_ref[...] * (1.0 + mod_ref[0, 1:2, :])) + mod_ref[0, 0:1, :]
    hb = h.astype(BF16)
    (m0, m1), (a0, a1), (g0, g1) = splits
    y = jnp.dot(hb, w_ref[:, a0:a1], preferred_element_type=F32)
    mix_ref[0] = jnp.dot(hb, w_ref[:, m0:m1], preferred_element_type=F32)
    gate_ref[0] = jnp.dot(hb, w_ref[:, g0:g1], preferred_element_type=F32)
    nq, nk = q_ref.shape[-1], k_ref.shape[-1]
    head = lambda off, hd: y[:, off + hd * HEAD:off + (hd + 1) * HEAD]
    if rope:
        cos, sin = cos_ref[...], sin_ref[...]
        first = (_iota(cos.shape, 1) & 32) == 0

        def normed_rotated(xh, w):
            n = _rms(xh, w)
            swapped = jnp.where(first, pltpu.roll(n, HEAD - 32, 1), pltpu.roll(n, 32, 1))
            return n, n * cos + swapped * sin

        for hd in range(nq // HEAD):
            _, r = normed_rotated(head(0, hd), qn_ref[...])
            q_ref[0, :, hd * HEAD:(hd + 1) * HEAD] = (r * QK_SCALE_LOG2).astype(BF16)
        for hd in range(nk // HEAD):
            n, r = normed_rotated(head(nq, hd), kn_ref[...])
            k_ref[0, :, hd * HEAD:(hd + 1) * HEAD] = r.astype(BF16)
            if plain_kv:
                rest[5][0, :, hd * HEAD:(hd + 1) * HEAD] = n
    else:
        q_ref[0] = (y[:, :nq] * QK_SCALE_LOG2).astype(BF16)
        k_ref[0] = y[:, nq:nq + nk].astype(BF16)
        if plain_kv:
            rest[5][0] = y[:, nq:nq + nk]
    v_ref[0] = y[:, nq + nk:].astype(BF16)
    if plain_kv:
        rest[6][0] = y[:, nq + nk:]


def _inproj(x, mod, norm_w, w, widths, tm, attn_widths, plain_kv, rope=None):
    bsz, seq_len, _ = x.shape
    splits, a = [], 0
    for wd in (widths[0], widths[1], widths[2]):
        splits.append((a, a + wd))
        a += wd
    row = lambda wd: pl.BlockSpec((1, tm, wd), lambda b, t: (b, t, 0))
    in_specs = [row(D_MODEL), _mod_spec(mod), pl.BlockSpec((1, D_MODEL), lambda b, t: (0, 0)),
                pl.BlockSpec(w.shape, lambda b, t: (0, 0))]
    args = [x, mod, norm_w.reshape(1, D_MODEL), w]
    if rope is not None:
        cos, sin, qn, kn, shared = rope
        tab = (pl.BlockSpec((tm, HEAD), lambda b, t: (0, 0)) if shared
               else pl.BlockSpec((tm, HEAD), lambda b, t: (t, 0)))
        in_specs += [tab, tab, pl.BlockSpec((1, HEAD), lambda b, t: (0, 0)), pl.BlockSpec((1, HEAD), lambda b, t: (0, 0))]
        args += [cos, sin, qn.reshape(1, HEAD), kn.reshape(1, HEAD)]
    out = [(widths[0], F32), (widths[2], F32)] + [(wd, BF16) for wd in attn_widths]
    if plain_kv:
        out += [(attn_widths[1], F32), (attn_widths[2], F32)]
    return pl.pallas_call(
        functools.partial(_inproj_kernel, splits=tuple(splits), rope=rope is not None, plain_kv=plain_kv),
        grid=(bsz, seq_len // tm),
        in_specs=in_specs,
        out_specs=[row(wd) for wd, _ in out],
        out_shape=[jax.ShapeDtypeStruct((bsz, seq_len, wd), dt) for wd, dt in out],
        compiler_params=_cparams("parallel", "parallel"),
        name="inproj",
    )(*args)


def _outproj_kernel(a_ref, b_ref, x_ref, mod_ref, w_ref, nw_ref, *rest, norm_a):
    a = a_ref[0]
    if norm_a:
        na_ref, o_ref = rest
        a = _rms(a, na_ref[...])
    else:
        (o_ref,) = rest
    half = a.shape[-1]
    y = (jnp.dot(a.astype(BF16), w_ref[:half, :], preferred_element_type=F32)
         + jnp.dot(b_ref[0].astype(BF16), w_ref[half:, :], preferred_element_type=F32))
    o_ref[0] = x_ref[0] + _rms(y, nw_ref[...] * mod_ref[0, 2:3, :])


def _outproj(a, b, x, mod, w, norm_w, tm, norm_a_w=None):
    bsz, seq_len, _ = x.shape
    half = a.shape[-1]
    in_specs = [pl.BlockSpec((1, tm, half), lambda bb, t: (bb, t, 0)),
                pl.BlockSpec((1, tm, half), lambda bb, t: (bb, t, 0)),
                pl.BlockSpec((1, tm, D_MODEL), lambda bb, t: (bb, t, 0)),
                _mod_spec(mod),
                pl.BlockSpec(w.shape, lambda bb, t: (0, 0)),
                pl.BlockSpec((1, D_MODEL), lambda bb, t: (0, 0))]
    args = [a, b, x, mod, w, norm_w.reshape(1, D_MODEL)]
    if norm_a_w is not None:
        in_specs.append(pl.BlockSpec((1, half), lambda bb, t: (0, 0)))
        args.append(norm_a_w.reshape(1, half))
    return pl.pallas_call(
        functools.partial(_outproj_kernel, norm_a=norm_a_w is not None),
        grid=(bsz, seq_len // tm),
        in_specs=in_specs,
        out_specs=pl.BlockSpec((1, tm, D_MODEL), lambda bb, t: (bb, t, 0)),
        out_shape=jax.ShapeDtypeStruct(x.shape, F32),
        compiler_params=_cparams("parallel", "parallel"),
        name="outproj",
    )(*args)


def _mlp_kernel(x_ref, mod_ref, npre_ref, w1_ref, w2_ref, npost_ref, o_ref, h_ref, acc_ref):
    f = pl.program_id(2)

    @pl.when(f == 0)
    def _():
        h = _rms(x_ref[0], npre_ref[...] * (1.0 + mod_ref[0, 4:5, :])) + mod_ref[0, 3:4, :]
        h_ref[...] = h.astype(BF16)
        acc_ref[...] = jnp.zeros_like(acc_ref)

    u = jnp.maximum(jnp.dot(h_ref[...], w1_ref[...], preferred_element_type=F32), 0.0)
    acc_ref[...] += jnp.dot((u * u).astype(BF16), w2_ref[...], preferred_element_type=F32)

    @pl.when(f == pl.num_programs(2) - 1)
    def _():
        o_ref[0] = x_ref[0] + _rms(acc_ref[...], npost_ref[...] * mod_ref[0, 5:6, :])


def _mlp(x, mod, npre, w1, w2, npost, tm, tf):
    bsz, seq_len, _ = x.shape
    return pl.pallas_call(
        _mlp_kernel,
        grid=(bsz, seq_len // tm, D_FF // tf),
        in_specs=[pl.BlockSpec((1, tm, D_MODEL), lambda b, t, f: (b, t, 0)),
                  _mod_spec(mod),
                  pl.BlockSpec((1, D_MODEL), lambda b, t, f: (0, 0)),
                  pl.BlockSpec((D_MODEL, tf), lambda b, t, f: (0, f)),
                  pl.BlockSpec((tf, D_MODEL), lambda b, t, f: (f, 0)),
                  pl.BlockSpec((1, D_MODEL), lambda b, t, f: (0, 0))],
        out_specs=pl.BlockSpec((1, tm, D_MODEL), lambda b, t, f: (b, t, 0)),
        out_shape=jax.ShapeDtypeStruct(x.shape, F32),
        scratch_shapes=[pltpu.VMEM((tm, D_MODEL), BF16), pltpu.VMEM((tm, D_MODEL), F32)],
        compiler_params=_cparams("parallel", "parallel", "arbitrary"),
        name="mlp",
    )(x, mod, npre.reshape(1, D_MODEL), w1, w2, npost.reshape(1, D_MODEL))


def _flash_kernel(q_ref, k_ref, v_ref, *rest, rep, tk, with_cache):
    if with_cache:
        kc_ref, vc_ref, o_ref, knorm_ref = rest
    else:
        o_ref, knorm_ref = rest
    tq = q_ref.shape[1]
    if rep == 2:
        qs = [q_ref[0, :, r * HEAD:(r + 1) * HEAD] for r in range(rep)]
    else:
        qs = [q_ref[0, :tq // 2, :], q_ref[0, tq // 2:, :]]
    nk = k_ref.shape[1] // tk
    rows = qs[0].shape[0]

    def lane_chunks(s):
        return [s[:, j * HEAD:(j + 1) * HEAD] for j in range(s.shape[1] // HEAD)]

    def row_max(k, part):
        s = [_dot_nt(q, k) for q in qs]
        return tuple(functools.reduce(jnp.maximum, lane_chunks(sc), m) for sc, m in zip(s, part))

    def accumulate(k, v, mx, carry):
        s = [_dot_nt(q, k) for q in qs]
        ps, ls = [], []
        for sc, m, (l, _) in zip(s, mx, carry):
            pc = [jnp.exp2(c - m) for c in lane_chunks(sc)]
            ls.append(functools.reduce(jnp.add, pc, l))
            ps.append(jnp.concatenate([c.astype(BF16) for c in pc], axis=-1))
        pv = [jnp.dot(p, v.astype(BF16), preferred_element_type=F32) for p in ps]
        return tuple((l, acc + x) for l, (_, acc), x in zip(ls, carry, pv))

    def block(ref, i):
        return ref[0, pl.ds(pl.multiple_of(i * tk, tk), tk), :]

    def max_sq_norm(k, best):
        kf = k.astype(BF16).astype(F32)
        sq = jnp.broadcast_to(jnp.sum(kf * kf, axis=-1, keepdims=True), kf.shape)
        return jnp.maximum(best, jnp.max(sq, axis=0, keepdims=True))

    @pl.when(pl.program_id(2) == 0)
    def _():
        best = lax.fori_loop(0, nk, lambda i, b: max_sq_norm(block(k_ref, i), b), jnp.zeros((1, HEAD), F32))
        if with_cache:
            best = max_sq_norm(kc_ref[0], best)
        knorm_ref[...] = jnp.sqrt(best)

    def exact_row_max():
        part = tuple(jnp.full((rows, HEAD), NEG, F32) for _ in qs)
        part = lax.fori_loop(0, nk, lambda i, p: row_max(block(k_ref, i), p), part)
        if with_cache:
            part = row_max(kc_ref[0], part)
        return [jnp.broadcast_to(jnp.max(m, axis=-1, keepdims=True), (rows, HEAD)) for m in part]

    bounds = []
    for q in qs:
        qf = q.astype(F32)
        qn = jnp.sqrt(jnp.sum(qf * qf, axis=-1, keepdims=True))
        bounds.append(jnp.broadcast_to(qn, (rows, HEAD)) * knorm_ref[...])
    safe = jnp.max(functools.reduce(jnp.maximum, bounds)) <= FLASH_SAFE_BOUND
    mx = lax.cond(safe, lambda: [b - FLASH_SHIFT for b in bounds], exact_row_max)

    stats = tuple((jnp.zeros((rows, HEAD), F32), jnp.zeros((rows, HEAD), F32)) for _ in qs)
    stats = lax.fori_loop(0, nk, lambda i, c: accumulate(block(k_ref, i), block(v_ref, i), mx, c), stats)
    if with_cache:
        stats = accumulate(kc_ref[0], vc_ref[0], mx, stats)
    outs = [(acc / jnp.sum(l, axis=-1, keepdims=True)).astype(o_ref.dtype) for l, acc in stats]
    if rep == 2:
        for r in range(rep):
            o_ref[0, :, r * HEAD:(r + 1) * HEAD] = outs[r]
    else:
        o_ref[0, :tq // 2, :] = outs[0]
        o_ref[0, tq // 2:, :] = outs[1]


def _flash(q, k, v, kv_heads, rep, tq, tk, cache=None):
    bsz, seq_len, _ = q.shape
    assert rep in (1, 2)
    m_len = k.shape[1]
    in_specs = [pl.BlockSpec((1, tq, rep * HEAD), lambda b, g, t: (b, t, g)),
                pl.BlockSpec((1, m_len, HEAD), lambda b, g, t: (b, 0, g)),
                pl.BlockSpec((1, m_len, HEAD), lambda b, g, t: (b, 0, g))]
    args = [q, k, v]
    if cache is not None:
        layer = cache[2]
        spec = pl.BlockSpec((1, None, cache[0].shape[2], HEAD), lambda b, g, t: (b, layer, 0, g))
        in_specs += [spec, spec]
        args += [cache[0], cache[1]]
    return pl.pallas_call(
        functools.partial(_flash_kernel, rep=rep, tk=tk, with_cache=cache is not None),
        grid=(bsz, kv_heads, seq_len // tq),
        in_specs=in_specs,
        out_specs=pl.BlockSpec((1, tq, rep * HEAD), lambda b, g, t: (b, t, g)),
        out_shape=jax.ShapeDtypeStruct((bsz, seq_len, kv_heads * rep * HEAD), BF16),
        scratch_shapes=[pltpu.VMEM((1, HEAD), F32)],
        compiler_params=_cparams("parallel", "parallel", "arbitrary"),
        name="flash",
    )(*args)


NA_GROUP = 4
NA_BAND = NA_GROUP + NA_WIN_R
NA_STEP = 4


def _natten_kernel(q_ref, k_ref, v_ref, kc_ref, vc_ref, bias_ref, o_ref):
    rows = q_ref.shape[1] // GRID_W
    gq = NA_GROUP * GRID_W
    kc = kc_ref[0].astype(BF16)
    vc = vc_ref[0].astype(BF16)
    low_half = _iota((GRID_W, HEAD), 1) < GRID_W

    def group_bias(r_first, band_first):
        per_row = []
        for i in range(NA_GROUP):
            r = r_first + i
            r0 = jnp.clip(r - NA_WIN_R // 2, 0, rows - NA_WIN_R)
            blocks = []
            for jj in range(NA_BAND // 2):
                kr = band_first + 2 * jj
                blk = bias_ref[0, jnp.clip(kr - r + NA_WIN_R, 0, 2 * NA_WIN_R - 1)]
                ok_a = jnp.logical_and(kr >= r0, kr < r0 + NA_WIN_R)
                ok_b = jnp.logical_and(kr + 1 >= r0, kr + 1 < r0 + NA_WIN_R)
                blocks.append(jnp.where(low_half, jnp.where(ok_a, blk, NEG), jnp.where(ok_b, blk, NEG)))
            per_row.append(jnp.concatenate(blocks, axis=-1))
        return jnp.concatenate(per_row, axis=0)

    def body(it, _):
        firsts = [(it * NA_STEP + g) * NA_GROUP for g in range(NA_STEP)]
        bands = [jnp.clip(r - NA_WIN_R // 2, 0, rows - NA_BAND) for r in firsts]
        q0 = [pl.multiple_of(r * GRID_W, gq) for r in firsts]
        k0 = [pl.multiple_of(b * GRID_W, GRID_W) for b in bands]
        qs = [q_ref[0, pl.ds(a, gq), :] for a in q0]
        ks = [k_ref[0, pl.ds(a, NA_BAND * GRID_W), :] for a in k0]
        vs = [v_ref[0, pl.ds(a, NA_BAND * GRID_W), :] for a in k0]
        bias = [group_bias(r, b) for r, b in zip(firsts, bands)]
        s_loc = [_dot_nt(q, k) for q, k in zip(qs, ks)]
        s_ctx = [_dot_nt(q, kc) for q in qs]
        s_loc = [s + b for s, b in zip(s_loc, bias)]
        m = [jnp.maximum(jnp.max(a, axis=-1, keepdims=True), jnp.max(c, axis=-1, keepdims=True))
             for a, c in zip(s_loc, s_ctx)]
        p_loc = [jnp.exp2(a - mm) for a, mm in zip(s_loc, m)]
        p_ctx = [jnp.exp2(c - mm) for c, mm in zip(s_ctx, m)]
        l = [jnp.sum(a, axis=-1, keepdims=True) + jnp.sum(c, axis=-1, keepdims=True) for a, c in zip(p_loc, p_ctx)]
        o_loc = [_dot(p, v) for p, v in zip(p_loc, vs)]
        o_ctx = [_dot(p, vc) for p in p_ctx]
        for a, ol, oc, ll in zip(q0, o_loc, o_ctx, l):
            o_ref[0, pl.ds(a, gq), :] = ((ol + oc) / ll).astype(o_ref.dtype)
        return 0

    lax.fori_loop(0, rows // (NA_GROUP * NA_STEP), body, 0)


def _natten(q, k, v, kc, vc, layer, bias):
    bsz, seq_len, _ = q.shape
    m_len = kc.shape[2]
    col = pl.BlockSpec((1, seq_len, HEAD), lambda b, h: (b, 0, h))
    ctx = pl.BlockSpec((1, None, m_len, HEAD), lambda b, h: (b, layer, 0, h))
    return pl.pallas_call(
        _natten_kernel,
        grid=(bsz, NA_HEADS),
        in_specs=[col, col, col, ctx, ctx,
                  pl.BlockSpec((1,) + bias.shape[1:], lambda b, h: (h, 0, 0, 0))],
        out_specs=pl.BlockSpec((1, seq_len, HEAD), lambda b, h: (b, 0, h)),
        out_shape=jax.ShapeDtypeStruct((bsz, seq_len, NA_HEADS * HEAD), BF16),
        compiler_params=_cparams("parallel", "parallel"),
        name="natten",
    )(q, k, v, kc, vc, bias)


def _natten_bias(rpb):
    col = np.arange(GRID_W)
    col_start = np.clip(col - NA_WIN_C // 2, 0, GRID_W - NA_WIN_C)
    kc = np.arange(GRID_W)
    valid = (kc[None, :] >= col_start[:, None]) & (kc[None, :] < col_start[:, None] + NA_WIN_C)
    coff = np.clip(kc[None, :] - col[:, None] + NA_WIN_C - 1, 0, 2 * NA_WIN_C - 2)
    log2e = float(np.log2(np.e))
    t = jnp.where(valid[None, None], rpb.astype(F32)[:, :, coff] * log2e, NEG)
    t = jnp.pad(t, ((0, 0), (1, 1), (0, 0), (0, 0)), constant_values=NEG)
    return jnp.concatenate([t[:, :-1], t[:, 1:]], axis=-1)


def _gdn_kernel(q_ref, k_ref, v_ref, z_ref, g_ref, cwq_ref, cwk_ref, cwv_ref, par_ref, nrm_ref, s0_ref,
                o_ref, sfin_ref, qs, ks, vs, oacc):
    bpp, seq_len = q_ref.shape[0], q_ref.shape[1]
    nchunks = seq_len // CHUNK
    h = pl.program_id(1)

    def prep(c, _):
        r0 = pl.multiple_of(c * CHUNK, CHUNK)
        for bb in range(bpp):
            for src, cw, dst, kind in ((q_ref, cwq_ref, qs, "q"), (k_ref, cwk_ref, ks, "k"), (v_ref, cwv_ref, vs, "v")):
                y = _silu(_conv3(src, r0, c, nchunks, seq_len, cw, lead=(bb,)))
                if kind != "v":
                    y = y * lax.rsqrt(jnp.sum(y * y, axis=-1, keepdims=True) + EPS)
                if kind == "q":
                    y = y * HEAD ** -0.5
                dst[bb, pl.ds(r0, CHUNK), :] = y
        return 0

    lax.fori_loop(0, nchunks, prep, 0, unroll=2)

    ri = _iota((CHUNK, CHUNK), 0)
    ci = _iota((CHUNK, CHUNK), 1)
    neg_a = -jnp.exp(par_ref[0:1, :])
    dt_bias = par_ref[1:2, :]

    def load(bb, c):
        r0 = pl.multiple_of(c * CHUNK, CHUNK)
        return (qs[bb, pl.ds(r0, CHUNK), :], ks[bb, pl.ds(r0, CHUNK), :], vs[bb, pl.ds(r0, CHUNK), :],
                g_ref[bb, pl.ds(r0, CHUNK), :])

    def gate_terms(loaded, d):
        q, k, v, gates = loaded
        beta = _lane_col(jax.nn.sigmoid(gates), d * GDN_HEADS + h)
        g_t = neg_a * _softplus(gates + dt_bias)
        gc_t = _cumsum_rows(g_t)
        tot_t = jnp.broadcast_to(gc_t[CHUNK - 1:CHUNK, :], gc_t.shape)
        if d == 1:
            gc_t = tot_t - gc_t + g_t
        gidx = 2 * GDN_HEADS + d * GDN_HEADS + h
        gc = _lane_col(gc_t, gidx)
        tot = _lane_col(tot_t, gidx)
        incl = (ri >= ci) if d == 0 else (ri <= ci)
        decay = jnp.exp(jnp.where(incl, gc - _row_form(gc), NEG))
        egc = jnp.exp(gc)
        kb = k * beta
        return dict(q=q, k=k, kb=kb, incl=incl, strict=(ri > ci) if d == 0 else (ri < ci), decay=decay,
                    x=jnp.concatenate([v * beta, kb * egc], axis=-1), q_dec=q * egc,
                    k_dec=k * jnp.exp(tot - gc), etot=jnp.exp(tot))

    unroll = min(GDN_UNROLL, nchunks)
    nsteps = nchunks // unroll

    def chunk_ids(i):
        cf = [i * unroll + j for j in range(unroll)]
        return cf, [nchunks - 1 - c for c in cf]

    def solve(i):
        cf, cb = chunk_ids(i)
        t = [gate_terms(load(bb, c), d) for bb in range(bpp) for c, d in zip(cf + cb, [0] * unroll + [1] * unroll)]
        kk = [_dot_nt(ch["kb"], ch["k"]) for ch in t]
        qk = [_dot_nt(ch["q"], ch["k"]) for ch in t]
        yield
        a = [jnp.where(ch["strict"], m * ch["decay"], 0.0) for ch, m in zip(t, kk)]
        qk = [jnp.where(ch["incl"], m * ch["decay"], 0.0).astype(BF16) for ch, m in zip(t, qk)]
        inv = yield from _unit_tri_inverse_stages(a, ri, ci)
        x = _dots_many(inv, [ch["x"] for ch in t], TRI_PASSES[2])
        yield
        return tuple((x[n][:, :HEAD], x[n][:, HEAD:].astype(BF16), qk[n], t[n]["q_dec"].astype(BF16),
                      jnp.transpose(t[n]["k_dec"]).astype(BF16), jnp.broadcast_to(t[n]["etot"], (CHUNK, HEAD)))
                     for n in range(2 * unroll * bpp))

    def scan(i, pre, states):
        cf, cb = chunk_ids(i)
        states = list(states)
        outs = []
        for j in range(unroll):
            chains = [pre[bb * 2 * unroll + d * unroll + j] for bb in range(bpp) for d in range(2)]
            sbf = [s.astype(BF16) for s in states]
            ws = [jnp.dot(ch[1], s, preferred_element_type=F32) for ch, s in zip(chains, sbf)]
            o1 = [jnp.dot(ch[3], s, preferred_element_type=F32) for ch, s in zip(chains, sbf)]
            yield
            v_new = [(ch[0] - w).astype(BF16) for ch, w in zip(chains, ws)]
            o2 = [jnp.dot(ch[2], vn, preferred_element_type=F32) for ch, vn in zip(chains, v_new)]
            kv = [jnp.dot(ch[4], vn, preferred_element_type=F32) for ch, vn in zip(chains, v_new)]
            yield
            states = [s * ch[5] + m for ch, s, m in zip(chains, states, kv)]
            outs.append([a1 + a2 for a1, a2 in zip(o1, o2)])
        for j in range(unroll):
            for bb in range(bpp):
                oacc[bb, 0, pl.ds(pl.multiple_of(cf[j] * CHUNK, CHUNK), CHUNK), :] = outs[j][2 * bb]
                oacc[bb, 1, pl.ds(pl.multiple_of(cb[j] * CHUNK, CHUNK), CHUNK), :] = outs[j][2 * bb + 1]
        return tuple(states)

    def step(i, carry):
        pre, states = carry
        nxt, states = _run_interleaved(solve(i + 1), scan(i, pre, states))
        return nxt, states

    pre, _ = _run_interleaved(solve(0), iter(()))
    states = tuple(s0_ref[bb, d, 0] for bb in range(bpp) for d in range(2))
    pre, states = lax.fori_loop(0, nsteps - 1, step, (pre, states))
    _, states = _run_interleaved(iter(()), scan(nsteps - 1, pre, states))
    for bb in range(bpp):
        sfin_ref[bb, 0, 0] = states[2 * bb]
        sfin_ref[bb, 1, 0] = states[2 * bb + 1]

    def fin(c, _):
        r0 = pl.multiple_of(c * CHUNK, CHUNK)
        for bb in range(bpp):
            o = oacc[bb, 0, pl.ds(r0, CHUNK), :] + oacc[bb, 1, pl.ds(r0, CHUNK), :]
            y = _rms(o, nrm_ref[...]) * _silu(z_ref[bb, pl.ds(r0, CHUNK), :])
            o_ref[bb, pl.ds(r0, CHUNK), :] = y.astype(o_ref.dtype)
        return 0

    lax.fori_loop(0, nchunks, fin, 0, unroll=min(4, nchunks))


def _gdn(x, gates, conv_w, par, norm_w, s0, layer, bpp):
    bsz, seq_len, _ = x.shape
    col = lambda off: pl.BlockSpec((bpp, seq_len, HEAD), lambda b, h: (b, 0, off + h))
    cw = lambda off: pl.BlockSpec((3, HEAD), lambda b, h: (0, off + h))
    st = pl.BlockSpec((bpp, 2, 1, HEAD, HEAD), lambda b, h: (b, 0, h, 0, 0))
    st_in = pl.BlockSpec((bpp, None, 2, 1, HEAD, HEAD), lambda b, h: (b, layer, 0, h, 0, 0))
    return pl.pallas_call(
        _gdn_kernel,
        grid=(bsz // bpp, GDN_HEADS),
        in_specs=[col(0), col(GDN_HEADS), col(2 * GDN_HEADS), col(3 * GDN_HEADS),
                  pl.BlockSpec((bpp, seq_len, HEAD), lambda b, h: (b, 0, 0)),
                  cw(0), cw(GDN_HEADS), cw(2 * GDN_HEADS),
                  pl.BlockSpec((8, HEAD), lambda b, h: (0, 0)),
                  pl.BlockSpec((1, HEAD), lambda b, h: (0, 0)),
                  st_in],
        out_specs=[pl.BlockSpec((bpp, seq_len, HEAD), lambda b, h: (b, 0, h)), st],
        out_shape=[jax.ShapeDtypeStruct((bsz, seq_len, GDN_HEADS * HEAD), BF16),
                   jax.ShapeDtypeStruct((bsz, 2, GDN_HEADS, HEAD, HEAD), F32)],
        scratch_shapes=[pltpu.VMEM((bpp, seq_len, HEAD), F32)] * 3 + [pltpu.VMEM((bpp, 2, seq_len, HEAD), F32)],
        compiler_params=_cparams("parallel", "parallel"),
        name="gdn",
    )(x, x, x, x, gates, conv_w, conv_w, conv_w, par, norm_w.reshape(1, HEAD), s0)


def _ssd_kernel(x_ref, bm_ref, cm_ref, z_ref, dt_ref, cwx_ref, cwb_ref, cwc_ref, cbx_ref, cbb_ref, cbc_ref,
                par_ref, dsk_ref, h0_ref, y_ref, hfin_ref, xs, bs, cs):
    seq_len = x_ref.shape[1]
    nchunks = seq_len // CHUNK
    width = SSD_HPG * SSD_P

    def prep(c, _):
        r0 = pl.multiple_of(c * CHUNK, CHUNK)
        for src, cw, cb, dst in ((x_ref, cwx_ref, cbx_ref, xs), (bm_ref, cwb_ref, cbb_ref, bs),
                                 (cm_ref, cwc_ref, cbc_ref, cs)):
            dst[pl.ds(r0, CHUNK), :] = _silu(_conv3(src, r0, c, nchunks, seq_len, cw, lead=(0,)) + cb[...])
        return 0

    lax.fori_loop(0, nchunks, prep, 0, unroll=2)

    ri = _iota((CHUNK, CHUNK), 0)
    ci = _iota((CHUNK, CHUNK), 1)
    head_of_lane = jnp.right_shift(_iota((CHUNK, width), 1), 6)
    neg_a = -jnp.exp(par_ref[0:1, :])
    dt_bias = par_ref[1:2, :]

    def expand(cols):
        out = jnp.broadcast_to(cols[0], (CHUNK, width))
        for e in range(1, SSD_HPG):
            out = jnp.where(head_of_lane == e, cols[e], out)
        return out

    def gate_terms(c, d):
        r0 = pl.multiple_of(c * CHUNK, CHUNK)
        x, bm, cm = xs[pl.ds(r0, CHUNK), :], bs[pl.ds(r0, CHUNK), :], cs[pl.ds(r0, CHUNK), :]
        dt_t = _softplus(dt_ref[0, pl.ds(r0, CHUNK), :] + dt_bias)
        la_t = dt_t * neg_a
        cs_t = _cumsum_rows(la_t)
        tot_row = cs_t[CHUNK - 1:CHUNK, :]
        tot_t = jnp.broadcast_to(tot_row, cs_t.shape)
        if d == 1:
            cs_t = tot_t - cs_t + la_t
        incl = (ri >= ci) if d == 0 else (ri <= ci)
        cs_rows = jnp.transpose(cs_t)
        rest_t = tot_t - cs_t
        tot_p = jnp.broadcast_to(tot_row, (SSD_P, HEAD))
        lanes = [d * SSD_HPG + e for e in range(SSD_HPG)]
        col = _lane_col
        lmats = [jnp.exp(jnp.where(incl, col(cs_t, k) - cs_rows[k:k + 1, :], NEG)) for k in lanes]
        xdt = x * expand([col(dt_t, k) for k in lanes])
        cs_full = expand([col(cs_t, k) for k in lanes])
        cd = jnp.concatenate([jnp.broadcast_to(jnp.exp(col(tot_p, k)), (SSD_P, HEAD)) for k in lanes], axis=0)
        return dict(r0=r0, x=x, bm=bm, cm=cm, xdt=xdt, lmats=lmats, ecs=jnp.exp(cs_full), cd=cd,
                    xdec_t=jnp.transpose(xdt * jnp.exp(expand([col(rest_t, k) for k in lanes]))))

    unroll = min(SSD_UNROLL, nchunks)

    def direction(d, h0, finish):
        def step(i, hstate):
            cidx = [i * unroll + j for j in range(unroll)]
            if d == 1:
                cidx = [nchunks - 1 - c for c in cidx]
            t = [gate_terms(c, d) for c in cidx]
            cb = [_dot_nt(ch["cm"], ch["bm"]) for ch in t]
            states = [_dot(ch["xdec_t"], ch["bm"]) for ch in t]
            ys = []
            for ch, m in zip(t, cb):
                y = None
                for e in range(SSD_HPG):
                    part = _dot(m * ch["lmats"][e], jnp.where(head_of_lane == e, ch["xdt"], 0.0))
                    y = part if y is None else y + part
                ys.append(y)
            for j, ch in enumerate(t):
                ys[j] = ys[j] + _dot_nt(ch["cm"], hstate) * ch["ecs"]
                hstate = hstate * ch["cd"] + states[j]
            for ch, y in zip(t, ys):
                finish(ch, y)
            return hstate

        return lax.fori_loop(0, nchunks // unroll, step, h0.reshape(width, HEAD))

    def store_fwd(ch, y):
        y_ref[0, pl.ds(ch["r0"], CHUNK), :] = y

    def store_bwd(ch, y):
        tot = y_ref[0, pl.ds(ch["r0"], CHUNK), :] + y + ch["x"] * dsk_ref[...]
        y_ref[0, pl.ds(ch["r0"], CHUNK), :] = tot * _silu(z_ref[0, pl.ds(ch["r0"], CHUNK), :])

    hfin_ref[0, 0] = direction(0, h0_ref[0, 0], store_fwd).reshape(SSD_HPG, SSD_P, HEAD)
    hfin_ref[0, 1] = direction(1, h0_ref[0, 1], store_bwd).reshape(SSD_HPG, SSD_P, HEAD)


def _ssd(x, gates, conv_w, conv_b, par, d_skip, h0, layer):
    bsz, seq_len, _ = x.shape
    width = SSD_HPG * SSD_P
    spec = lambda wd, off: pl.BlockSpec((1, seq_len, wd), lambda b, g: (b, 0, off + g))
    cws = lambda rows, wd, off: pl.BlockSpec((rows, wd), lambda b, g: (0, off + g))
    st = pl.BlockSpec((1, 2, SSD_HPG, SSD_P, HEAD), lambda b, g: (b, 0, g, 0, 0))
    st_in = pl.BlockSpec((1, None, 2, SSD_HPG, SSD_P, HEAD), lambda b, g: (b, layer, 0, g, 0, 0))
    return pl.pallas_call(
        _ssd_kernel,
        grid=(bsz, SSD_GROUPS),
        in_specs=[spec(width, 0), spec(HEAD, 4), spec(HEAD, 6), spec(width, 4),
                  pl.BlockSpec((1, seq_len, HEAD), lambda b, g: (b, 0, g)),
                  cws(3, width, 0), cws(3, HEAD, 4), cws(3, HEAD, 6),
                  cws(1, width, 0), cws(1, HEAD, 4), cws(1, HEAD, 6),
                  pl.BlockSpec((8, HEAD), lambda b, g: (0, g)),
                  cws(1, width, 0),
                  st_in],
        out_specs=[pl.BlockSpec((1, seq_len, width), lambda b, g: (b, 0, g)), st],
        out_shape=[jax.ShapeDtypeStruct((bsz, seq_len, SSD_HEADS * SSD_P), F32),
                   jax.ShapeDtypeStruct((bsz, 2, SSD_HEADS, SSD_P, HEAD), F32)],
        scratch_shapes=[pltpu.VMEM((seq_len, width), F32), pltpu.VMEM((seq_len, HEAD), F32),
                        pltpu.VMEM((seq_len, HEAD), F32)],
        compiler_params=_cparams("parallel", "parallel"),
        name="ssd",
    )(x, x, x, x, gates, conv_w, conv_w, conv_w, conv_b, conv_b, conv_b, par, d_skip, h0)


def _rope_tables(n_tokens):
    half = HEAD // 2
    inv_freq = ROPE_THETA ** (-jnp.arange(0, half, 2, dtype=F32) / half)
    t = jnp.arange(n_tokens)
    ang_r = (t // GRID_W).astype(F32)[:, None] * inv_freq
    ang_c = (t % GRID_W).astype(F32)[:, None] * inv_freq
    cos = jnp.concatenate([jnp.cos(ang_r)] * 2 + [jnp.cos(ang_c)] * 2, axis=-1)
    sin = jnp.concatenate([-jnp.sin(ang_r), jnp.sin(ang_r), -jnp.sin(ang_c), jnp.sin(ang_c)], axis=-1)
    return cos, sin


def _gate_rows(a_log, dt_bias, offset):
    n = a_log.size
    rows = jnp.zeros((8, HEAD), F32)
    rows = rows.at[0, offset:offset + n].set(a_log.reshape(-1).astype(F32))
    return rows.at[1, offset:offset + n].set(dt_bias.reshape(-1).astype(F32))


def _ssd_group_lanes(t):
    lead = t.shape[:-1]
    t = t.reshape(lead + (2, SSD_GROUPS, SSD_HPG))
    t = jnp.swapaxes(t, -3, -2).reshape(lead + (SSD_GROUPS, 2 * SSD_HPG))
    pad = [(0, 0)] * (t.ndim - 1) + [(0, HEAD - 2 * SSD_HPG)]
    return jnp.pad(t, pad).reshape(lead + (SSD_GROUPS * HEAD,))


def _pad_cols(w, width):
    return jnp.pad(w, ((0, 0), (0, width - w.shape[1])))


def kernel(x_prompt, x_sample, state_gdn, cache_gqa_k, cache_gqa_v, state_ssd, cache_na_k, cache_na_v, c, c_ctx, ada_w, ada_b, norm_mix_pre, norm_mix_post, norm_mlp_pre, norm_mlp_post, mlp_w1, mlp_w2, ev_w_in, ev_w_out, gdn_conv, gdn_a_log, gdn_dt_bias, gdn_norm, gqa_q_norm, gqa_k_norm, od_w_in, od_w_out, ssd_conv, ssd_conv_b, ssd_a_log, ssd_dt_bias, ssd_d, ssd_norm, na_rpb):
    nb, ns, _ = x_prompt.shape
    db, dl, _ = x_sample.shape
    past = cache_gqa_k.shape[2]

    c16 = jnp.zeros((16, D_MODEL), F32).at[:db].set(c).at[db].set(c_ctx)
    mods = _ada(c16, ada_w, ada_b).reshape(DEPTH, 16, 6, D_MODEL)

    xp = x_prompt.reshape(1, nb * ns, D_MODEL)
    xs = x_sample
    tm_c = min(TM_DENSE_CTX, nb * ns)
    tm_in = min(TM_INPROJ, nb * ns, dl)
    tm_out, tm_mlp = min(TM_OUTPROJ, dl), min(TM_MLP, dl)
    cos_l, sin_l = _rope_tables(dl)
    cos_c, sin_c = jnp.ones((tm_in, HEAD), F32), jnp.zeros((tm_in, HEAD), F32)
    ctx_bpp = max(n for n in (4, 2, 1) if nb % n == 0)
    zero_gdn = jnp.zeros((nb, 1, 2, GDN_HEADS, HEAD, HEAD), F32)
    zero_ssd = jnp.zeros((nb, 1, 2, SSD_HEADS, SSD_P, HEAD), F32)
    new_gdn, new_gk, new_gv, new_ssd, new_nk, new_nv = [], [], [], [], [], []

    for i in range(DEPTH):
        j = i // 2
        mod_l, mod_c = mods[i, :db], mods[i, db:db + 1]
        if i % 2 == 0:
            wi = ev_w_in[j]
            w_in = jnp.concatenate([wi[:, :2048], wi[:, 2064:], _pad_cols(wi[:, 2048:2064], HEAD)], axis=1).astype(BF16)
            widths = (2048, 1024, HEAD)
            w_out = ev_w_out[j].astype(BF16)
            par = _gate_rows(gdn_a_log[j], gdn_dt_bias[j], 2 * GDN_HEADS)

            attn_w = (GQA_HEADS * HEAD, GQA_KV_HEADS * HEAD, GQA_KV_HEADS * HEAD)
            outs = _inproj(xp, mod_c, norm_mix_pre[i], w_in, widths, tm_in, attn_w, True,
                           (cos_c, sin_c, gqa_q_norm[j], gqa_k_norm[j], True))
            gx, gt, q, k, v, kf, vf = (t.reshape(nb, ns, -1) for t in outs)
            o_a, s_a = _gdn(gx, gt, gdn_conv[j], par, gdn_norm[j], zero_gdn, 0, ctx_bpp)
            o_b = _flash(q, k, v, GQA_KV_HEADS, 2, ns, ns)
            yc = _outproj(o_a.reshape(1, nb * ns, -1), o_b.reshape(1, nb * ns, -1), xp, mod_c,
                          w_out, norm_mix_post[i], tm_c)
            new_gdn.append(s_a)
            new_gk.append(kf.reshape(nb, ns, GQA_KV_HEADS, HEAD))
            new_gv.append(vf.reshape(nb, ns, GQA_KV_HEADS, HEAD))

            gx, gt, q, k, v = _inproj(xs, mod_l, norm_mix_pre[i], w_in, widths, tm_in, attn_w, False,
                                      (cos_l, sin_l, gqa_q_norm[j], gqa_k_norm[j], False))
            o_a, _ = _gdn(gx, gt, gdn_conv[j], par, gdn_norm[j], state_gdn, j, 1)
            o_b = _flash(q, k, v, GQA_KV_HEADS, 2, min(FLASH_TQ, dl), min(FLASH_TK, dl),
                         cache=(cache_gqa_k.reshape(db, -1, past, GQA_KV_HEADS * HEAD),
                                cache_gqa_v.reshape(db, -1, past, GQA_KV_HEADS * HEAD), j))
            yl = _outproj(o_a, o_b, xs, mod_l, w_out, norm_mix_post[i], tm_out)
        else:
            wi = od_w_in[j]
            w_in = jnp.concatenate([wi[:, 512:1536], wi[:, :512], wi[:, 1552:], _ssd_group_lanes(wi[:, 1536:1552])], axis=1).astype(BF16)
            widths = (1536, 1536, SSD_GROUPS * HEAD)
            w_out = od_w_out[j].astype(BF16)
            par = jnp.zeros((8, SSD_GROUPS * HEAD), F32)
            par = par.at[0].set(_ssd_group_lanes(ssd_a_log[j].reshape(-1).astype(F32)))
            par = par.at[1].set(_ssd_group_lanes(ssd_dt_bias[j].reshape(-1).astype(F32)))
            dsk = jnp.repeat(ssd_d[j].astype(F32), SSD_P).reshape(1, -1)
            cvw, cvb = ssd_conv[j], ssd_conv_b[j].reshape(1, -1)

            attn_w = (NA_HEADS * HEAD,) * 3
            outs = _inproj(xp, mod_c, norm_mix_pre[i], w_in, widths, tm_in, attn_w, True)
            sx, gt, q, k, v, kf, vf = (t.reshape(nb, ns, -1) for t in outs)
            y_c, s_c = _ssd(sx, gt, cvw, cvb, par, dsk, zero_ssd, 0)
            o_d = _flash(q, k, v, NA_HEADS, 1, ns, ns)
            yc = _outproj(y_c.reshape(1, nb * ns, -1), o_d.reshape(1, nb * ns, -1), xp, mod_c,
                          w_out, norm_mix_post[i], tm_c, ssd_norm[j])
            new_ssd.append(s_c)
            new_nk.append(kf.reshape(nb, ns, NA_HEADS, HEAD))
            new_nv.append(vf.reshape(nb, ns, NA_HEADS, HEAD))

            sx, gt, q, k, v = _inproj(xs, mod_l, norm_mix_pre[i], w_in, widths, tm_in, attn_w, False)
            y_c, _ = _ssd(sx, gt, cvw, cvb, par, dsk, state_ssd, j)
            o_d = _natten(q, k, v, cache_na_k.reshape(db, -1, past, NA_HEADS * HEAD),
                          cache_na_v.reshape(db, -1, past, NA_HEADS * HEAD), j, _natten_bias(na_rpb[j]))
            yl = _outproj(y_c, o_d, xs, mod_l, w_out, norm_mix_post[i], tm_out, ssd_norm[j])
        xp, xs = yc, yl
        w1, w2 = mlp_w1[i].astype(BF16), mlp_w2[i].astype(BF16)
        xp = _mlp(xp, mod_c, norm_mlp_pre[i], w1, w2, norm_mlp_post[i], tm_c, TF_MLP)
        xs = _mlp(xs, mod_l, norm_mlp_pre[i], w1, w2, norm_mlp_post[i], tm_mlp, TF_MLP)

    return (xp.reshape(nb, ns, D_MODEL), xs, jnp.stack(new_gdn, axis=1), jnp.stack(new_gk, axis=1),
            jnp.stack(new_gv, axis=1), jnp.stack(new_ssd, axis=1), jnp.stack(new_nk, axis=1),
            jnp.stack(new_nv, axis=1))
```
